```python
import math
import jax, jax.numpy as jnp
from jax import lax
import numpy as np

D_MODEL = 1024
BATCH = 8
SEQ = 2048
DEPTH = 1

CONV_DIM = D_MODEL // 2
CONV_WIDTH = 3
N_HEADS = 8
HEAD_DIM = 64
N_KV_GROUPS = 2
HEADS_PER_GROUP = N_HEADS // N_KV_GROUPS
NSA_DIM = N_HEADS * HEAD_DIM
KV_DIM = N_KV_GROUPS * HEAD_DIM
ROPE_DIM = HEAD_DIM // 4
ROPE_THETA = 500000.0
CMP_BLOCK = 32
CMP_STRIDE = 16
CMP_HIDDEN = 2 * HEAD_DIM
SEL_BLOCK = 64
N_SEL = 16
WINDOW = 512
Q_BLOCK = 128
SEL_Q_CHUNK = 64
N_NSA_BRANCHES = 3
D_FF = 4 * D_MODEL
ALPHA = (2 * DEPTH) ** 0.25
BETA = (8 * DEPTH) ** -0.25
LN_EPS = 1e-5
NEG = -1e30
FORCE = 1e9

IN_SPLITS = (CONV_DIM, CONV_DIM, CONV_DIM, NSA_DIM,
             KV_DIM, KV_DIM, KV_DIM, KV_DIM, KV_DIM, KV_DIM,
             N_HEADS * N_NSA_BRANCHES, D_MODEL, D_MODEL)
N_IN = sum(IN_SPLITS)

kernel_name = "hybrid_conv_nsa_deepnorm_block"


def layer_norm(x, g, b):
    xf = x.astype(jnp.float32)
    mu = xf.mean(-1, keepdims=True)
    var = jnp.square(xf - mu).mean(-1, keepdims=True)
    y = (xf - mu) * lax.rsqrt(var + LN_EPS)
    return (y * g.astype(jnp.float32) + b.astype(jnp.float32)).astype(x.dtype)


def masked_softmax(s, mask):
    s = jnp.where(mask, s.astype(jnp.float32), NEG)
    m = s.max(-1, keepdims=True)
    e = jnp.where(mask, jnp.exp(s - m), 0.0)
    return e / jnp.maximum(e.sum(-1, keepdims=True), 1e-30)


def rope_tables(seq):
    inv = ROPE_THETA ** (-jnp.arange(0, ROPE_DIM, 2, dtype=jnp.float32) / ROPE_DIM)
    ang = jnp.arange(seq, dtype=jnp.float32)[:, None] * inv[None, :]
    return jnp.cos(ang), jnp.sin(ang)


def partial_rope(x, cos, sin):
    half = ROPE_DIM // 2
    shape = (cos.shape[0],) + (1,) * (x.ndim - 3) + (half,)
    c = cos.reshape(shape).astype(x.dtype)
    s = sin.reshape(shape).astype(x.dtype)
    x1, x2, rest = x[..., :half], x[..., half:ROPE_DIM], x[..., ROPE_DIM:]
    return jnp.concatenate([x1 * c - x2 * s, x2 * c + x1 * s, rest], axis=-1)


def short_conv_mixer(h, b_gate, c_gate, conv_w):
    u = c_gate * h
    y = lax.conv_general_dilated(u, conv_w, window_strides=(1,),
                                 padding=[(CONV_WIDTH - 1, 0)],
                                 dimension_numbers=('NWC', 'WIO', 'NWC'),
                                 feature_group_count=CONV_DIM)
    return b_gate * y


def compress(kv, pe, w1, w2):
    nc = (kv.shape[1] - CMP_BLOCK) // CMP_STRIDE + 1
    idx = jnp.arange(nc)[:, None] * CMP_STRIDE + jnp.arange(CMP_BLOCK)[None, :]
    blocks = kv[:, idx] + pe[:, None, :].astype(kv.dtype)
    hid = jax.nn.gelu(jnp.einsum('bnlgd,ldh->bngh', blocks, w1))
    return jnp.einsum('bngh,hd->bngd', hid, w2)


def cmp_to_sel_matrix(nc, ns):
    start = np.arange(nc)[:, None] * CMP_STRIDE
    s0 = np.arange(ns)[None, :] * SEL_BLOCK
    return jnp.asarray(((start < s0 + SEL_BLOCK) & (start + CMP_BLOCK > s0)).astype(np.float32))


def nsa_attention(q, k_cmp, v_cmp, k_sel, v_sel, k_win, v_win, gate_logits,
                  pe_k, wk1, wk2, pe_v, wv1, wv2, cos, sin):
    B, S = q.shape[0], q.shape[1]
    G, Z, Dh = N_KV_GROUPS, HEADS_PER_GROUP, HEAD_DIM
    scale = Dh ** -0.5
    t = jnp.arange(S)
    q = q.reshape(B, S, G, Z, Dh)
    k_cmp, v_cmp, k_sel, v_sel, k_win, v_win = [
        a.reshape(B, S, G, Dh) for a in (k_cmp, v_cmp, k_sel, v_sel, k_win, v_win)]
    q_rot = partial_rope(q, cos, sin)
    k_sel = partial_rope(k_sel, cos, sin)
    k_win = partial_rope(k_win, cos, sin)

    kc = compress(k_cmp, pe_k, wk1, wk2)
    vc = compress(v_cmp, pe_v, wv1, wv2)
    nc = kc.shape[1]
    s_c = jnp.einsum('bsgzd,bngd->bgzsn', q, kc) * scale
    blk_end = jnp.arange(nc) * CMP_STRIDE + CMP_BLOCK - 1
    p_c = masked_softmax(s_c, blk_end[None, :] <= t[:, None])
    o_cmp = jnp.einsum('bgzsn,bngd->bsgzd', p_c, vc)

    ns = S // SEL_BLOCK
    n_sel = min(N_SEL, ns)
    imp = jnp.einsum('bgzsn,nj->bgsj', p_c, cmp_to_sel_matrix(nc, ns))
    jblk = jnp.arange(ns)[None, :]
    cur = (t // SEL_BLOCK)[:, None]
    forced = (jblk == 0) | (jblk == cur) | (jblk == cur - 1)
    imp = jnp.where(jblk <= cur, jnp.where(forced, FORCE, imp), NEG)
    vals, sel_idx = lax.top_k(imp, n_sel)
    sel_ok = vals > 0.5 * NEG

    kb = k_sel.reshape(B, ns, SEL_BLOCK, G, Dh).transpose(0, 3, 1, 2, 4)
    vb = v_sel.reshape(B, ns, SEL_BLOCK, G, Dh).transpose(0, 3, 1, 2, 4)
    nq = S // SEL_Q_CHUNK
    gather = jax.vmap(jax.vmap(lambda a, i: a[i]))

    def sel_chunk(args):
        qc, ic, okc, tc = args
        kg = gather(kb, ic)
        vg = gather(vb, ic)
        s = jnp.einsum('bcgzd,bgcnld->bgzcnl', qc, kg) * scale
        kpos = ic[..., None] * SEL_BLOCK + jnp.arange(SEL_BLOCK)
        mask = okc[..., None] & (kpos <= tc[:, None, None])
        nk = n_sel * SEL_BLOCK
        p = masked_softmax(s.reshape(B, G, Z, SEL_Q_CHUNK, nk),
                           mask.reshape(B, G, 1, SEL_Q_CHUNK, nk))
        return jnp.einsum('bgzck,bgckd->bcgzd', p, vg.reshape(B, G, SEL_Q_CHUNK, nk, Dh))

    o_sel = lax.map(sel_chunk, (
        q_rot.reshape(B, nq, SEL_Q_CHUNK, G, Z, Dh).transpose(1, 0, 2, 3, 4, 5),
        sel_idx.reshape(B, G, nq, SEL_Q_CHUNK, n_sel).transpose(2, 0, 1, 3, 4),
        sel_ok.reshape(B, G, nq, SEL_Q_CHUNK, n_sel).transpose(2, 0, 1, 3, 4),
        t.reshape(nq, SEL_Q_CHUNK)))
    o_sel = o_sel.transpose(1, 0, 2, 3, 4, 5).reshape(B, S, G, Z, Dh)

    nqb = S // Q_BLOCK
    nb = WINDOW // Q_BLOCK + 1

    def band(a):
        ap = jnp.pad(a, ((0, 0), (WINDOW, 0), (0, 0), (0, 0)))
        ab = ap.reshape(B, (S + WINDOW) // Q_BLOCK, Q_BLOCK, G, Dh)
        return jnp.concatenate([ab[:, j:j + nqb] for j in range(nb)], axis=2)

    kw = band(k_win)
    vw = band(v_win)
    qw = q_rot.reshape(B, nqb, Q_BLOCK, G, Z, Dh)
    s_w = jnp.einsum('biqgzd,bikgd->bgziqk', qw, kw) * scale
    qpos = t.reshape(nqb, Q_BLOCK)
    kpos = jnp.arange(nqb)[:, None] * Q_BLOCK - WINDOW + jnp.arange(nb * Q_BLOCK)[None, :]
    diff = qpos[:, :, None] - kpos[:, None, :]
    wmask = (kpos[:, None, :] >= 0) & (diff >= 0) & (diff < WINDOW)
    p_w = masked_softmax(s_w, wmask)
    o_win = jnp.einsum('bgziqk,bikgd->biqgzd', p_w, vw).reshape(B, S, G, Z, Dh)

    g = jax.nn.sigmoid(gate_logits.astype(jnp.float32)).reshape(B, S, G, Z, N_NSA_BRANCHES)
    o = g[..., 0:1] * o_cmp + g[..., 1:2] * o_sel + g[..., 2:3] * o_win
    return o.reshape(B, S, NSA_DIM).astype(q.dtype)


def hybrid_layer(x, w_in, conv_w, w_conv_out, pe_k_cmp, w_k_cmp1, w_k_cmp2,
                 pe_v_cmp, w_v_cmp1, w_v_cmp2, w_nsa_out, w_o, ln1_g, ln1_b,
                 w_up, w_down, ln2_g, ln2_b, cos, sin):
    proj = x @ w_in
    offs = np.cumsum(IN_SPLITS)[:-1].tolist()
    (h, b_gate, c_gate, q, k_cmp, v_cmp, k_sel, v_sel, k_win, v_win,
     nsa_gates, g_conv, g_nsa) = jnp.split(proj, offs, axis=-1)
    y_conv = short_conv_mixer(h, b_gate, c_gate, conv_w) @ w_conv_out
    y_nsa = nsa_attention(q, k_cmp, v_cmp, k_sel, v_sel, k_win, v_win, nsa_gates,
                          pe_k_cmp, w_k_cmp1, w_k_cmp2, pe_v_cmp, w_v_cmp1, w_v_cmp2,
                          cos, sin) @ w_nsa_out
    mixed = jax.nn.sigmoid(g_conv) * y_conv + jax.nn.sigmoid(g_nsa) * y_nsa
    x = layer_norm(ALPHA * x + mixed @ w_o, ln1_g, ln1_b)
    ff = jnp.square(jax.nn.relu(x @ w_up)) @ w_down
    return layer_norm(ALPHA * x + ff, ln2_g, ln2_b)


def setup_inputs(seed: int = 0) -> dict:
    key = jax.random.key(seed)
    ks = jax.random.split(key, 20)
    nrm = lambda k, shape, s: jax.random.normal(k, shape, jnp.float32) * s
    L = DEPTH
    return {
        "x": nrm(ks[0], (BATCH, SEQ, D_MODEL), 1.0),
        "w_in": nrm(ks[1], (L, D_MODEL, N_IN), D_MODEL ** -0.5),
        "conv_w": nrm(ks[2], (L, CONV_WIDTH, 1, CONV_DIM), CONV_WIDTH ** -0.5),
        "w_conv_out": nrm(ks[3], (L, CONV_DIM, D_MODEL), CONV_DIM ** -0.5),
        "pe_k_cmp": nrm(ks[4], (L, CMP_BLOCK, HEAD_DIM), 0.5),
        "w_k_cmp1": nrm(ks[5], (L, CMP_BLOCK, HEAD_DIM, CMP_HIDDEN), (CMP_BLOCK * HEAD_DIM) ** -0.5),
        "w_k_cmp2": nrm(ks[6], (L, CMP_HIDDEN, HEAD_DIM), CMP_HIDDEN ** -0.5),
        "pe_v_cmp": nrm(ks[7], (L, CMP_BLOCK, HEAD_DIM), 0.5),
        "w_v_cmp1": nrm(ks[8], (L, CMP_BLOCK, HEAD_DIM, CMP_HIDDEN), (CMP_BLOCK * HEAD_DIM) ** -0.5),
        "w_v_cmp2": nrm(ks[9], (L, CMP_HIDDEN, HEAD_DIM), CMP_HIDDEN ** -0.5),
        "w_nsa_out": nrm(ks[10], (L, NSA_DIM, D_MODEL), NSA_DIM ** -0.5),
        "w_o": nrm(ks[11], (L, D_MODEL, D_MODEL), BETA * D_MODEL ** -0.5),
        "ln1_g": 1.0 + nrm(ks[12], (L, D_MODEL), 0.02),
        "ln1_b": nrm(ks[13], (L, D_MODEL), 0.02),
        "w_up": nrm(ks[14], (L, D_MODEL, D_FF), D_MODEL ** -0.5),
        "w_down": nrm(ks[15], (L, D_FF, D_MODEL), BETA * D_FF ** -0.5),
        "ln2_g": 1.0 + nrm(ks[16], (L, D_MODEL), 0.02),
        "ln2_b": nrm(ks[17], (L, D_MODEL), 0.02),
    }


def reference(x, w_in, conv_w, w_conv_out, pe_k_cmp, w_k_cmp1, w_k_cmp2,
              pe_v_cmp, w_v_cmp1, w_v_cmp2, w_nsa_out, w_o, ln1_g, ln1_b,
              w_up, w_down, ln2_g, ln2_b):
    cos, sin = rope_tables(x.shape[1])
    for l in range(DEPTH):
        x = hybrid_layer(x, w_in[l], conv_w[l], w_conv_out[l], pe_k_cmp[l], w_k_cmp1[l],
                         w_k_cmp2[l], pe_v_cmp[l], w_v_cmp1[l], w_v_cmp2[l], w_nsa_out[l],
                         w_o[l], ln1_g[l], ln1_b[l], w_up[l], w_down[l], ln2_g[l], ln2_b[l],
                         cos, sin)
    return x
```

```python
import functools

import numpy as np
import jax
import jax.numpy as jnp
from jax import lax
from jax.experimental import pallas as pl
from jax.experimental.pallas import tpu as pltpu

F32 = jnp.float32
BF16 = jnp.bfloat16

D_MODEL = 1024
CONV_DIM = D_MODEL // 2
CONV_WIDTH = 3
N_HEADS = 8
HEAD_DIM = 64
N_KV_GROUPS = 2
HEADS_PER_GROUP = N_HEADS // N_KV_GROUPS
NSA_DIM = N_HEADS * HEAD_DIM
KV_DIM = N_KV_GROUPS * HEAD_DIM
ROPE_DIM = HEAD_DIM // 4
ROPE_HALF = ROPE_DIM // 2
ROPE_THETA = 500000.0
CMP_BLOCK = 32
CMP_STRIDE = 16
CMP_HIDDEN = 2 * HEAD_DIM
SEL_BLOCK = 64
N_SEL = 16
WINDOW = 512
N_NSA_BRANCHES = 3
D_FF = 4 * D_MODEL
LN_EPS = 1e-5
NEG = -1e30
FORCE = 1e9
SCALE = HEAD_DIM ** -0.5

_SPLITS = (CONV_DIM, CONV_DIM, CONV_DIM, NSA_DIM, KV_DIM, KV_DIM, KV_DIM, KV_DIM, KV_DIM, KV_DIM,
           N_HEADS * N_NSA_BRANCHES, D_MODEL, D_MODEL)
_OFFS = np.concatenate([[0], np.cumsum(_SPLITS)]).tolist()
(O_H, O_B, O_C, O_Q, O_KCMP, O_VCMP, O_KSEL, O_VSEL, O_KWIN, O_VWIN, O_GATE, O_GCONV, O_GNSA, O_END) = _OFFS

GATE_ROWS = 16
N_STD = 3 * CONV_DIM + 4 * KV_DIM
N_TR = NSA_DIM + 2 * KV_DIM + N_KV_GROUPS * GATE_ROWS

PROJ_TM = 256
ATT_TQ = 128
ATT_TK = 128
TAIL_TM = 256
FF_CHUNK = 1024
VMEM_LIMIT = 56 * 1024 * 1024


def _dot(a, b):
    return jnp.dot(a, b, preferred_element_type=F32)


def _dot_nt(a, b):
    return lax.dot_general(a, b, (((1,), (1,)), ((), ())), preferred_element_type=F32)


def _proj_kernel(x_ref, wstd_ref, wtr_ref, convw_ref, cosk_ref, sink_ref, cosq_ref, sinq_ref,
                 mix_ref, kcmp_ref, vcmp_ref, ksel_ref, kwin_ref,
                 qT_ref, qrT_ref, vselT_ref, vwinT_ref, gT_ref, ubuf):
    tm = x_ref.shape[0]
    s = pl.program_id(1)
    xb = x_ref[...].astype(BF16)

    hbc = _dot(xb, wstd_ref[:, 0:3 * CONV_DIM])
    u = hbc[:, 2 * CONV_DIM:3 * CONV_DIM] * hbc[:, 0:CONV_DIM]

    @pl.when(s == 0)
    def _():
        ubuf[0:8, :] = jnp.zeros((8, CONV_DIM), F32)

    ubuf[8:8 + tm, :] = u
    u1 = ubuf[7:7 + tm, :]
    u2 = ubuf[6:6 + tm, :]
    cw = convw_ref[...]
    y = cw[0:1, :] * u2 + cw[1:2, :] * u1 + cw[2:3, :] * u
    mix_ref[...] = (hbc[:, CONV_DIM:2 * CONV_DIM] * y).astype(BF16)
    ubuf[0:8, :] = ubuf[tm:tm + 8, :]

    kk = _dot(xb, wstd_ref[:, 3 * CONV_DIM:N_STD])
    kcmp_ref[...] = kk[:, 0:KV_DIM].astype(BF16)
    vcmp_ref[...] = kk[:, KV_DIM:2 * KV_DIM].astype(BF16)
    lane = lax.broadcasted_iota(jnp.int32, (tm, KV_DIM), 1)
    first_half = (lane % ROPE_DIM) < ROPE_HALF
    ck = cosk_ref[...]
    sk = sink_ref[...]
    for j, out in ((2, ksel_ref), (3, kwin_ref)):
        k = kk[:, j * KV_DIM:(j + 1) * KV_DIM]
        partner = jnp.where(first_half, pltpu.roll(k, KV_DIM - ROPE_HALF, 1), pltpu.roll(k, ROPE_HALF, 1))
        out[...] = (k * ck + partner * sk).astype(BF16)

    yT = _dot_nt(wtr_ref[...], xb)
    q = yT[0:NSA_DIM, :] * SCALE
    qT_ref[...] = q.astype(BF16)
    qrT_ref[...] = q.astype(BF16)
    cq = cosq_ref[...]
    sq = sinq_ref[...]
    for h in range(N_HEADS):
        r0 = h * HEAD_DIM
        x1 = q[r0:r0 + ROPE_HALF, :]
        x2 = q[r0 + ROPE_HALF:r0 + ROPE_DIM, :]
        rot = jnp.concatenate([x1 * cq - x2 * sq, x2 * cq + x1 * sq], axis=0)
        qrT_ref[r0:r0 + ROPE_DIM, :] = rot.astype(BF16)
    tk = vselT_ref.shape[-1]
    for i in range(tm // tk):
        vselT_ref[i] = yT[NSA_DIM:NSA_DIM + KV_DIM, i * tk:(i + 1) * tk].astype(BF16)
        vwinT_ref[i] = yT[NSA_DIM + KV_DIM:NSA_DIM + 2 * KV_DIM, i * tk:(i + 1) * tk].astype(BF16)
    gT_ref[...] = jax.nn.sigmoid(yT[NSA_DIM + 2 * KV_DIM:N_TR, :])


def _proj(x2, wstd, wtr, convw, cosk, sink, cosq, sinq, B, S):
    tm, tk = PROJ_TM, ATT_TK
    ns = S // tm
    row = lambda b, s: (b * ns + s, 0)
    const = lambda b, s: (0, 0)
    tok = lambda c, dt: jax.ShapeDtypeStruct((B * S, c), dt)
    out_shape = (
        tok(CONV_DIM, BF16), tok(KV_DIM, BF16), tok(KV_DIM, BF16), tok(KV_DIM, BF16), tok(KV_DIM, BF16),
        jax.ShapeDtypeStruct((B, NSA_DIM, S), BF16), jax.ShapeDtypeStruct((B, NSA_DIM, S), BF16),
        jax.ShapeDtypeStruct((B, S // tk, KV_DIM, tk), BF16), jax.ShapeDtypeStruct((B, S // tk, KV_DIM, tk), BF16),
        jax.ShapeDtypeStruct((B, N_KV_GROUPS * GATE_ROWS, S), F32),
    )
    feat = lambda r: pl.BlockSpec((None, r, tm), lambda b, s: (b, 0, s))
    vt = pl.BlockSpec((None, tm // tk, KV_DIM, tk), lambda b, s: (b, s, 0, 0))
    return pl.pallas_call(
        _proj_kernel,
        grid=(B, ns),
        in_specs=[
            pl.BlockSpec((tm, D_MODEL), row),
            pl.BlockSpec((D_MODEL, N_STD), const),
            pl.BlockSpec((N_TR, D_MODEL), const),
            pl.BlockSpec((CONV_WIDTH, CONV_DIM), const),
            pl.BlockSpec((tm, KV_DIM), lambda b, s: (s, 0)),
            pl.BlockSpec((tm, KV_DIM), lambda b, s: (s, 0)),
            pl.BlockSpec((ROPE_HALF, tm), lambda b, s: (0, s)),
            pl.BlockSpec((ROPE_HALF, tm), lambda b, s: (0, s)),
        ],
        out_specs=(
            pl.BlockSpec((tm, CONV_DIM), row), pl.BlockSpec((tm, KV_DIM), row), pl.BlockSpec((tm, KV_DIM), row),
            pl.BlockSpec((tm, KV_DIM), row), pl.BlockSpec((tm, KV_DIM), row),
            feat(NSA_DIM), feat(NSA_DIM), vt, vt, feat(N_KV_GROUPS * GATE_ROWS),
        ),
        out_shape=out_shape,
        scratch_shapes=[pltpu.VMEM((tm + 8, CONV_DIM), F32)],
        compiler_params=pltpu.CompilerParams(
            dimension_semantics=("arbitrary", "arbitrary"), vmem_limit_bytes=VMEM_LIMIT),
        name="proj",
    )(x2, wstd, wtr, convw, cosk, sink, cosq, sinq)


def _gelu_tanh(x):
    return x * (0.5 * (1.0 + jnp.tanh(np.sqrt(2.0 / np.pi).astype(np.float32) * (x + 0.044715 * (x * x * x)))))


def _compress_kernel(kch_ref, vch_ref, pek_ref, pev_ref, wk1_ref, wv1_ref, wk2_ref, wv2t_ref, kc_ref, vcT_ref):
    def hidden(ch_ref, pe_ref, w1_ref):
        ch = ch_ref[...]
        a = _dot(ch, w1_ref[0])
        b = _dot(ch, w1_ref[1])
        pe = _dot(pe_ref[0], w1_ref[0]) + _dot(pe_ref[1], w1_ref[1])
        b_next = pltpu.roll(b, b.shape[0] - 1, 0)
        return _gelu_tanh(a + b_next + pe[0:1, :]).astype(BF16)

    kc_ref[...] = _dot(hidden(kch_ref, pek_ref, wk1_ref), wk2_ref[...]).astype(BF16)
    vcT_ref[...] = _dot_nt(wv2t_ref[...], hidden(vch_ref, pev_ref, wv1_ref)).astype(BF16)


def _compress(kch, vch, pek, pev, wk1, wv1, wk2, wv2t, B, ncp):
    cw = CMP_STRIDE * KV_DIM
    gh = N_KV_GROUPS * CMP_HIDDEN
    c3 = lambda b: (0, 0, 0)
    c2 = lambda b: (0, 0)
    return pl.pallas_call(
        _compress_kernel,
        grid=(B,),
        in_specs=[
            pl.BlockSpec((None, ncp, cw), lambda b: (b, 0, 0)),
            pl.BlockSpec((None, ncp, cw), lambda b: (b, 0, 0)),
            pl.BlockSpec((2, 8, cw), c3), pl.BlockSpec((2, 8, cw), c3),
            pl.BlockSpec((2, cw, gh), c3), pl.BlockSpec((2, cw, gh), c3),
            pl.BlockSpec((gh, KV_DIM), c2), pl.BlockSpec((KV_DIM, gh), c2),
        ],
        out_specs=(pl.BlockSpec((None, ncp, KV_DIM), lambda b: (b, 0, 0)),
                   pl.BlockSpec((None, KV_DIM, ncp), lambda b: (b, 0, 0))),
        out_shape=(jax.ShapeDtypeStruct((B, ncp, KV_DIM), BF16), jax.ShapeDtypeStruct((B, KV_DIM, ncp), BF16)),
        compiler_params=pltpu.CompilerParams(dimension_semantics=("arbitrary",), vmem_limit_bytes=VMEM_LIMIT),
        name="compress",
    )(kch, vch, pek, pev, wk1, wv1, wk2, wv2t)


def _attn_kernel(qT_ref, qrT_ref, kc_ref, vcT_ref, ksel_ref, vselT_ref, kwin_ref, vwinT_ref, gT_ref, mT_ref,
                 o_ref, sel_scr):
    tq = qT_ref.shape[1]
    tk = ksel_ref.shape[1]
    nq = HEADS_PER_GROUP * tq
    ns = mT_ref.shape[0]
    g = pl.program_id(1)
    p = pl.program_id(2)
    qbase = p * tq

    def stack_q(ref):
        q4 = jnp.concatenate([ref[z * HEAD_DIM:(z + 1) * HEAD_DIM, :] for z in range(HEADS_PER_GROUP)], axis=1)
        q8 = jnp.concatenate([q4, q4], axis=0)
        rows = lax.broadcasted_iota(jnp.int32, q8.shape, 0)
        return jnp.where((rows // HEAD_DIM) == g, q8, jnp.zeros_like(q8))

    def lanes4(a):
        return jnp.concatenate([a] * HEADS_PER_GROUP, axis=1)

    q_pad = stack_q(qT_ref)
    s_c = _dot(kc_ref[...], q_pad)
    ncp = s_c.shape[0]
    n_idx = lax.broadcasted_iota(jnp.int32, (ncp, tq), 0)
    t_idx = qbase + lax.broadcasted_iota(jnp.int32, (ncp, tq), 1)
    ok = (n_idx * CMP_STRIDE + (CMP_BLOCK - 1)) <= t_idx
    bias_c = lanes4(jnp.where(ok, 0.0, NEG).astype(F32))
    keep_c = lanes4(ok.astype(F32))
    s_c = s_c + bias_c
    m_c = jnp.max(s_c, axis=0, keepdims=True)
    e_c = jnp.exp(s_c - m_c) * keep_c
    inv_c = 1.0 / jnp.maximum(jnp.sum(e_c, axis=0, keepdims=True), 1e-30)
    o_cmp = _dot(vcT_ref[...], e_c.astype(BF16)) * inv_c
    imp4 = jnp.dot(mT_ref[...], e_c, preferred_element_type=F32, precision=lax.Precision.HIGHEST) * inv_c
    imp = imp4[:, 0:tq]
    for z in range(1, HEADS_PER_GROUP):
        imp = imp + imp4[:, z * tq:(z + 1) * tq]

    j_idx = lax.broadcasted_iota(jnp.int32, (ns, tq), 0)
    cur = (qbase + lax.broadcasted_iota(jnp.int32, (ns, tq), 1)) // SEL_BLOCK
    forced = (j_idx == 0) | (j_idx == cur) | (j_idx == cur - 1)
    impv = jnp.where(j_idx <= cur, jnp.where(forced, FORCE, imp), NEG)
    cnt = jnp.zeros((ns, tq), F32)
    for jp in range(ns):
        row = impv[jp:jp + 1, :]
        beats = (row > impv) | ((row == impv) & (j_idx > jp))
        cnt = cnt + beats.astype(F32)
    sel = ((cnt < float(min(N_SEL, ns))) & (impv > 0.5 * NEG)).astype(F32)
    bpt = tk // SEL_BLOCK
    for kt in range(ns // bpt):
        sel_scr[kt, 0:bpt, :] = sel[kt * bpt:(kt + 1) * bpt, :]

    qr_pad = stack_q(qrT_ref)
    d_rc = lax.broadcasted_iota(jnp.int32, (tk, tq), 0) - lax.broadcasted_iota(jnp.int32, (tk, tq), 1)
    krow = lax.broadcasted_iota(jnp.int32, (tk, tq), 0)

    def flash(k_ref, vT_ref, lo, hi, mask_fn):
        def body(kt, carry):
            m, l, acc = carry
            ok = mask_fn(kt)
            s = _dot(k_ref[kt], qr_pad) + lanes4(jnp.where(ok, 0.0, NEG).astype(F32))
            m_new = jnp.maximum(m, jnp.max(s, axis=0, keepdims=True))
            a = jnp.exp(m - m_new)
            e = jnp.exp(s - m_new) * lanes4(ok.astype(F32))
            l = a * l + jnp.sum(e, axis=0, keepdims=True)
            acc = a * acc + _dot(vT_ref[kt], e.astype(BF16))
            return m_new, l, acc

        init = (jnp.full((1, nq), NEG, F32), jnp.zeros((1, nq), F32), jnp.zeros((HEAD_DIM, nq), F32))
        _, l, acc = lax.fori_loop(lo, hi, body, init)
        return acc * (1.0 / jnp.maximum(l, 1e-30))

    def sel_mask(kt):
        c = qbase - kt * tk
        blk = sel_scr[kt, 0:bpt, :]
        chosen = blk[0:1, :]
        for i in range(1, bpt):
            chosen = jnp.where(krow >= i * SEL_BLOCK, blk[i:i + 1, :], chosen)
        return (chosen > 0.5) & (d_rc <= c)

    def win_mask(kt):
        c = qbase - kt * tk
        return (d_rc <= c) & (d_rc > c - WINDOW)

    o_sel = flash(ksel_ref, vselT_ref, 0, (qbase + tq - 1) // tk + 1, sel_mask)
    win_lo = jnp.maximum(qbase - (WINDOW - 1), 0) // tk
    o_win = flash(kwin_ref, vwinT_ref, win_lo, (qbase + tq - 1) // tk + 1, win_mask)

    gt = gT_ref[...]

    def gate(br):
        return jnp.concatenate([gt[br * HEADS_PER_GROUP + z:br * HEADS_PER_GROUP + z + 1, :]
                                for z in range(HEADS_PER_GROUP)], axis=1)

    oT = gate(0) * o_cmp + gate(1) * o_sel + gate(2) * o_win
    for i in range(HEADS_PER_GROUP // 2):
        pair = jnp.concatenate([oT[:, (2 * i) * tq:(2 * i + 1) * tq], oT[:, (2 * i + 1) * tq:(2 * i + 2) * tq]], axis=0)
        o_ref[:, i * 2 * HEAD_DIM:(i + 1) * 2 * HEAD_DIM] = pair.T.astype(BF16)


def _attn(qT, qrT, kc, vcT, ksel, vselT, kwin, vwinT, gT, mT, B, S):
    tq, tk = ATT_TQ, ATT_TK
    P = S // tq
    ncp = kc.shape[1]
    ns = S // SEL_BLOCK
    gz = HEADS_PER_GROUP * HEAD_DIM
    qspec = pl.BlockSpec((None, gz, tq), lambda b, g, p: (b, g, p))
    kspec = pl.BlockSpec((None, S // tk, tk, KV_DIM), lambda b, g, p: (b, 0, 0, 0))
    vspec = pl.BlockSpec((None, S // tk, HEAD_DIM, tk), lambda b, g, p: (b, 0, g, 0))
    return pl.pallas_call(
        _attn_kernel,
        grid=(B, N_KV_GROUPS, P),
        in_specs=[
            qspec, qspec,
            pl.BlockSpec((None, ncp, KV_DIM), lambda b, g, p: (b, 0, 0)),
            pl.BlockSpec((None, HEAD_DIM, ncp), lambda b, g, p: (b, g, 0)),
            kspec, vspec, kspec, vspec,
            pl.BlockSpec((None, GATE_ROWS, tq), lambda b, g, p: (b, g, p)),
            pl.BlockSpec((ns, ncp), lambda b, g, p: (0, 0)),
        ],
        out_specs=pl.BlockSpec((tq, gz), lambda b, g, p: (b * P + p, g)),
        out_shape=jax.ShapeDtypeStruct((B * S, NSA_DIM), BF16),
        scratch_shapes=[pltpu.VMEM((S // tk, 8, tq), F32)],
        compiler_params=pltpu.CompilerParams(
            dimension_semantics=("arbitrary", "arbitrary", "arbitrary"), vmem_limit_bytes=VMEM_LIMIT),
        name="attn",
    )(qT, qrT, kc, vcT, ksel, vselT, kwin, vwinT, gT, mT)


def _layer_norm(z, g, b):
    mu = jnp.mean(z, axis=-1, keepdims=True)
    zc = z - mu
    var = jnp.mean(zc * zc, axis=-1, keepdims=True)
    return zc * lax.rsqrt(var + LN_EPS) * g + b


def _tail_kernel(alpha, x_ref, mix_ref, o_ref, wg_ref, wco_ref, wno_ref, wo_ref, g1_ref, b1_ref,
                 wup_ref, wdn_ref, g2_ref, b2_ref, out_ref):
    x = x_ref[...]
    xb = x.astype(BF16)
    y_conv = _dot(mix_ref[...], wco_ref[...])
    y_nsa = _dot(o_ref[...], wno_ref[...])
    mixed = (jax.nn.sigmoid(_dot(xb, wg_ref[:, 0:D_MODEL])) * y_conv
             + jax.nn.sigmoid(_dot(xb, wg_ref[:, D_MODEL:2 * D_MODEL])) * y_nsa)
    x1 = _layer_norm(alpha * x + _dot(mixed.astype(BF16), wo_ref[...]), g1_ref[...], b1_ref[...])
    x1b = x1.astype(BF16)
    ff = jnp.zeros_like(x1)
    for c in range(D_FF // FF_CHUNK):
        h = jnp.maximum(_dot(x1b, wup_ref[:, c * FF_CHUNK:(c + 1) * FF_CHUNK]), 0.0)
        ff = ff + _dot((h * h).astype(BF16), wdn_ref[c * FF_CHUNK:(c + 1) * FF_CHUNK, :])
    out_ref[...] = _layer_norm(alpha * x1 + ff, g2_ref[...], b2_ref[...])


def _tail(x2, mix, o, wg, wco, wno, wo, g1, b1, wup, wdn, g2, b2, alpha):
    n = x2.shape[0]
    tm = TAIL_TM
    row = lambda i: (i, 0)
    const = lambda i: (0, 0)
    resident = lambda shape: pl.BlockSpec(shape, const, pipeline_mode=pl.Buffered(1))
    return pl.pallas_call(
        functools.partial(_tail_kernel, alpha),
        grid=(n // tm,),
        in_specs=[
            pl.BlockSpec((tm, D_MODEL), row), pl.BlockSpec((tm, CONV_DIM), row), pl.BlockSpec((tm, NSA_DIM), row),
            resident((D_MODEL, 2 * D_MODEL)), resident((CONV_DIM, D_MODEL)), resident((NSA_DIM, D_MODEL)),
            resident((D_MODEL, D_MODEL)), resident((1, D_MODEL)), resident((1, D_MODEL)),
            resident((D_MODEL, D_FF)), resident((D_FF, D_MODEL)), resident((1, D_MODEL)), resident((1, D_MODEL)),
        ],
        out_specs=pl.BlockSpec((tm, D_MODEL), row),
        out_shape=jax.ShapeDtypeStruct((n, D_MODEL), F32),
        compiler_params=pltpu.CompilerParams(dimension_semantics=("arbitrary",), vmem_limit_bytes=VMEM_LIMIT),
        name="tail",
    )(x2, mix, o, wg, wco, wno, wo, g1, b1, wup, wdn, g2, b2)


def _rope_tables(S):
    inv = ROPE_THETA ** (-jnp.arange(0, ROPE_DIM, 2, dtype=F32) / ROPE_DIM)
    ang = jnp.arange(S, dtype=F32)[:, None] * inv[None, :]
    cos, sin = jnp.cos(ang), jnp.sin(ang)
    pad_c = jnp.ones((S, HEAD_DIM - ROPE_DIM), F32)
    pad_s = jnp.zeros((S, HEAD_DIM - ROPE_DIM), F32)
    ck = jnp.tile(jnp.concatenate([cos, cos, pad_c], axis=1), (1, N_KV_GROUPS))
    sk = jnp.tile(jnp.concatenate([-sin, sin, pad_s], axis=1), (1, N_KV_GROUPS))
    return ck, sk, cos.T, sin.T


def _overlap_matrix_t(ncp, ns):
    nc = ncp - 1
    i = np.arange(ncp)[None, :]
    j = np.arange(ns)[:, None]
    m = (i * CMP_STRIDE < (j + 1) * SEL_BLOCK) & (i * CMP_STRIDE + CMP_BLOCK > j * SEL_BLOCK) & (i < nc)
    return jnp.asarray(m.astype(np.float32))


def _gate_perm():
    idx = np.zeros((N_KV_GROUPS, GATE_ROWS), np.int64)
    valid = np.zeros((N_KV_GROUPS, GATE_ROWS), np.float32)
    for g in range(N_KV_GROUPS):
        for br in range(N_NSA_BRANCHES):
            for z in range(HEADS_PER_GROUP):
                idx[g, br * HEADS_PER_GROUP + z] = O_GATE + (g * HEADS_PER_GROUP + z) * N_NSA_BRANCHES + br
                valid[g, br * HEADS_PER_GROUP + z] = 1.0
    return idx.reshape(-1), valid.reshape(-1)


def _compress_weights(pe, w1, w2):
    eye = jnp.eye(N_KV_GROUPS, dtype=F32)
    w1h = w1.reshape(2, CMP_STRIDE, HEAD_DIM, CMP_HIDDEN)
    w1big = jnp.einsum('ab,srdh->sradbh', eye, w1h).reshape(
        2, CMP_STRIDE * KV_DIM, N_KV_GROUPS * CMP_HIDDEN)
    peh = pe.reshape(2, CMP_STRIDE, 1, HEAD_DIM)
    pebig = jnp.broadcast_to(peh, (2, CMP_STRIDE, N_KV_GROUPS, HEAD_DIM)).reshape(2, 1, CMP_STRIDE * KV_DIM)
    pebig = jnp.broadcast_to(pebig, (2, 8, CMP_STRIDE * KV_DIM))
    w2big = jnp.einsum('ab,hd->ahbd', eye, w2).reshape(N_KV_GROUPS * CMP_HIDDEN, KV_DIM)
    return pebig.astype(BF16), w1big.astype(BF16), w2big.astype(BF16)


def _layer(x2, B, S, w_in, conv_w, w_conv_out, pe_k, wk1, wk2, pe_v, wv1, wv2, w_nsa_out, w_o,
           ln1_g, ln1_b, w_up, w_down, ln2_g, ln2_b, alpha):
    ncp = S // CMP_STRIDE
    ns = S // SEL_BLOCK
    gidx, gvalid = _gate_perm()
    wstd = jnp.concatenate([w_in[:, O_H:O_Q], w_in[:, O_KCMP:O_KSEL], w_in[:, O_KSEL:O_VSEL],
                            w_in[:, O_KWIN:O_VWIN]], axis=1).astype(BF16)
    wtr = jnp.concatenate([w_in[:, O_Q:O_KCMP], w_in[:, O_VSEL:O_KWIN], w_in[:, O_VWIN:O_GATE],
                           w_in[:, gidx] * gvalid[None, :]], axis=1).T.astype(BF16)
    wg = w_in[:, O_GCONV:O_END].astype(BF16)
    ck, sk, cq, sq = _rope_tables(S)

    mix, kcmp, vcmp, ksel, kwin, qT, qrT, vselT, vwinT, gT = _proj(
        x2, wstd, wtr, conv_w.reshape(CONV_WIDTH, CONV_DIM), ck, sk, cq, sq, B, S)

    pek, wk1b, wk2b = _compress_weights(pe_k, wk1, wk2)
    pev, wv1b, wv2b = _compress_weights(pe_v, wv1, wv2)
    kc, vcT = _compress(kcmp.reshape(B, ncp, CMP_STRIDE * KV_DIM), vcmp.reshape(B, ncp, CMP_STRIDE * KV_DIM),
                        pek, pev, wk1b, wv1b, wk2b, wv2b.T, B, ncp)

    tk = ATT_TK
    o = _attn(qT, qrT, kc, vcT, ksel.reshape(B, S // tk, tk, KV_DIM), vselT,
              kwin.reshape(B, S // tk, tk, KV_DIM), vwinT, gT, _overlap_matrix_t(ncp, ns), B, S)

    row = lambda v: v.reshape(1, D_MODEL).astype(F32)
    return _tail(x2, mix, o, wg, w_conv_out.astype(BF16), w_nsa_out.astype(BF16), w_o.astype(BF16),
                 row(ln1_g), row(ln1_b), w_up.astype(BF16), w_down.astype(BF16), row(ln2_g), row(ln2_b), alpha)


def kernel(x, w_in, conv_w, w_conv_out, pe_k_cmp, w_k_cmp1, w_k_cmp2, pe_v_cmp, w_v_cmp1, w_v_cmp2,
           w_nsa_out, w_o, ln1_g, ln1_b, w_up, w_down, ln2_g, ln2_b):
    B, S, D = x.shape
    depth = w_in.shape[0]
    assert D == D_MODEL and w_in.shape[2] == O_END
    assert S % PROJ_TM == 0 and S % ATT_TQ == 0 and (B * S) % TAIL_TM == 0 and S >= WINDOW
    alpha = float((2 * depth) ** 0.25)
    x2 = x.reshape(B * S, D)
    for l in range(depth):
        x2 = _layer(x2, B, S, w_in[l], conv_w[l], w_conv_out[l], pe_k_cmp[l], w_k_cmp1[l], w_k_cmp2[l],
                    pe_v_cmp[l], w_v_cmp1[l], w_v_cmp2[l], w_nsa_out[l], w_o[l], ln1_g[l], ln1_b[l],
                    w_up[l], w_down[l], ln2_g[l], ln2_b[l], alpha)
    return x2.reshape(B, S, D)
```

```python
import functools

import numpy as np
import jax
import jax.numpy as jnp
from jax import lax
from jax.experimental import pallas as pl
from jax.experimental.pallas import tpu as pltpu

F32 = jnp.float32
BF16 = jnp.bfloat16

D_MODEL = 1024
CONV_DIM = D_MODEL // 2
CONV_WIDTH = 3
N_HEADS = 8
HEAD_DIM = 64
N_KV_GROUPS = 2
HEADS_PER_GROUP = N_HEADS // N_KV_GROUPS
NSA_DIM = N_HEADS * HEAD_DIM
KV_DIM = N_KV_GROUPS * HEAD_DIM
ROPE_DIM = HEAD_DIM // 4
ROPE_HALF = ROPE_DIM // 2
ROPE_THETA = 500000.0
CMP_BLOCK = 32
CMP_STRIDE = 16
CMP_HIDDEN = 2 * HEAD_DIM
SEL_BLOCK = 64
N_SEL = 16
WINDOW = 512
N_NSA_BRANCHES = 3
D_FF = 4 * D_MODEL
LN_EPS = 1e-5
NEG = -1e30
FORCE = 1e9
SCALE = HEAD_DIM ** -0.5

_SPLITS = (CONV_DIM, CONV_DIM, CONV_DIM, NSA_DIM, KV_DIM, KV_DIM, KV_DIM, KV_DIM, KV_DIM, KV_DIM,
           N_HEADS * N_NSA_BRANCHES, D_MODEL, D_MODEL)
_OFFS = np.concatenate([[0], np.cumsum(_SPLITS)]).tolist()
(O_H, O_B, O_C, O_Q, O_KCMP, O_VCMP, O_KSEL, O_VSEL, O_KWIN, O_VWIN, O_GATE, O_GCONV, O_GNSA, O_END) = _OFFS

GATE_ROWS = 16
N_STD = 3 * CONV_DIM + 4 * KV_DIM
N_TR = NSA_DIM + 2 * KV_DIM + N_KV_GROUPS * GATE_ROWS

PROJ_TM = 256
ATT_TQ = 256
ATT_TK = 256
TAIL_TM = 256
FF_CHUNK = 1024
VMEM_LIMIT = 56 * 1024 * 1024


def _dot(a, b):
    return jnp.dot(a, b, preferred_element_type=F32)


def _dot_nt(a, b):
    return lax.dot_general(a, b, (((1,), (1,)), ((), ())), preferred_element_type=F32)


def _proj_kernel(x_ref, wstd_ref, wtr_ref, convw_ref, cosk_ref, sink_ref, cosq_ref, sinq_ref,
                 mix_ref, kcmp_ref, vcmp_ref, ksel_ref, kwin_ref,
                 qT_ref, qrT_ref, vselT_ref, vwinT_ref, gT_ref, ubuf):
    tm = x_ref.shape[0]
    s = pl.program_id(1)
    xb = x_ref[...].astype(BF16)

    hbc = _dot(xb, wstd_ref[:, 0:3 * CONV_DIM])
    u = hbc[:, 2 * CONV_DIM:3 * CONV_DIM] * hbc[:, 0:CONV_DIM]

    @pl.when(s == 0)
    def _():
        ubuf[0:8, :] = jnp.zeros((8, CONV_DIM), F32)

    ubuf[8:8 + tm, :] = u
    u1 = ubuf[7:7 + tm, :]
    u2 = ubuf[6:6 + tm, :]
    cw = convw_ref[...]
    y = cw[0:1, :] * u2 + cw[1:2, :] * u1 + cw[2:3, :] * u
    mix_ref[...] = (hbc[:, CONV_DIM:2 * CONV_DIM] * y).astype(BF16)
    ubuf[0:8, :] = ubuf[tm:tm + 8, :]

    kk = _dot(xb, wstd_ref[:, 3 * CONV_DIM:N_STD])
    kcmp_ref[...] = kk[:, 0:KV_DIM].astype(BF16)
    vcmp_ref[...] = kk[:, KV_DIM:2 * KV_DIM].astype(BF16)
    lane = lax.broadcasted_iota(jnp.int32, (tm, KV_DIM), 1)
    first_half = (lane % ROPE_DIM) < ROPE_HALF
    ck = cosk_ref[...]
    sk = sink_ref[...]
    for j, out in ((2, ksel_ref), (3, kwin_ref)):
        k = kk[:, j * KV_DIM:(j + 1) * KV_DIM]
        partner = jnp.where(first_half, pltpu.roll(k, KV_DIM - ROPE_HALF, 1), pltpu.roll(k, ROPE_HALF, 1))
        out[...] = (k * ck + partner * sk).astype(BF16)

    yT = _dot_nt(wtr_ref[...], xb)
    q = yT[0:NSA_DIM, :] * SCALE
    qT_ref[...] = q.astype(BF16)
    qrT_ref[...] = q.astype(BF16)
    cq = cosq_ref[...]
    sq = sinq_ref[...]
    for h in range(N_HEADS):
        r0 = h * HEAD_DIM
        x1 = q[r0:r0 + ROPE_HALF, :]
        x2 = q[r0 + ROPE_HALF:r0 + ROPE_DIM, :]
        rot = jnp.concatenate([x1 * cq - x2 * sq, x2 * cq + x1 * sq], axis=0)
        qrT_ref[r0:r0 + ROPE_DIM, :] = rot.astype(BF16)
    tk = vselT_ref.shape[-1]
    for i in range(tm // tk):
        vselT_ref[i] = yT[NSA_DIM:NSA_DIM + KV_DIM, i * tk:(i + 1) * tk].astype(BF16)
        vwinT_ref[i] = yT[NSA_DIM + KV_DIM:NSA_DIM + 2 * KV_DIM, i * tk:(i + 1) * tk].astype(BF16)
    gT_ref[...] = jax.nn.sigmoid(yT[NSA_DIM + 2 * KV_DIM:N_TR, :])


def _proj(x2, wstd, wtr, convw, cosk, sink, cosq, sinq, B, S):
    tm, tk = PROJ_TM, ATT_TK
    ns = S // tm
    row = lambda b, s: (b * ns + s, 0)
    const = lambda b, s: (0, 0)
    tok = lambda c, dt: jax.ShapeDtypeStruct((B * S, c), dt)
    out_shape = (
        tok(CONV_DIM, BF16), tok(KV_DIM, BF16), tok(KV_DIM, BF16), tok(KV_DIM, BF16), tok(KV_DIM, BF16),
        jax.ShapeDtypeStruct((B, NSA_DIM, S), BF16), jax.ShapeDtypeStruct((B, NSA_DIM, S), BF16),
        jax.ShapeDtypeStruct((B, S // tk, KV_DIM, tk), BF16), jax.ShapeDtypeStruct((B, S // tk, KV_DIM, tk), BF16),
        jax.ShapeDtypeStruct((B, N_KV_GROUPS * GATE_ROWS, S), F32),
    )
    feat = lambda r: pl.BlockSpec((None, r, tm), lambda b, s: (b, 0, s))
    vt = pl.BlockSpec((None, tm // tk, KV_DIM, tk), lambda b, s: (b, s, 0, 0))
    return pl.pallas_call(
        _proj_kernel,
        grid=(B, ns),
        in_specs=[
            pl.BlockSpec((tm, D_MODEL), row),
            pl.BlockSpec((D_MODEL, N_STD), const),
            pl.BlockSpec((N_TR, D_MODEL), const),
            pl.BlockSpec((CONV_WIDTH, CONV_DIM), const),
            pl.BlockSpec((tm, KV_DIM), lambda b, s: (s, 0)),
            pl.BlockSpec((tm, KV_DIM), lambda b, s: (s, 0)),
            pl.BlockSpec((ROPE_HALF, tm), lambda b, s: (0, s)),
            pl.BlockSpec((ROPE_HALF, tm), lambda b, s: (0, s)),
        ],
        out_specs=(
            pl.BlockSpec((tm, CONV_DIM), row), pl.BlockSpec((tm, KV_DIM), row), pl.BlockSpec((tm, KV_DIM), row),
            pl.BlockSpec((tm, KV_DIM), row), pl.BlockSpec((tm, KV_DIM), row),
            feat(NSA_DIM), feat(NSA_DIM), vt, vt, feat(N_KV_GROUPS * GATE_ROWS),
        ),
        out_shape=out_shape,
        scratch_shapes=[pltpu.VMEM((tm + 8, CONV_DIM), F32)],
        compiler_params=pltpu.CompilerParams(
            dimension_semantics=("arbitrary", "arbitrary"), vmem_limit_bytes=VMEM_LIMIT),
        name="proj",
    )(x2, wstd, wtr, convw, cosk, sink, cosq, sinq)


def _gelu_tanh(x):
    return x * (0.5 * (1.0 + jnp.tanh(np.sqrt(2.0 / np.pi).astype(np.float32) * (x + 0.044715 * (x * x * x)))))


def _compress_kernel(kch_ref, vch_ref, pek_ref, pev_ref, wk1_ref, wv1_ref, wk2_ref, wv2t_ref, kc_ref, vcT_ref):
    def hidden(ch_ref, pe_ref, w1_ref):
        ch = ch_ref[...]
        a = _dot(ch, w1_ref[0])
        b = _dot(ch, w1_ref[1])
        pe = _dot(pe_ref[0], w1_ref[0]) + _dot(pe_ref[1], w1_ref[1])
        b_next = pltpu.roll(b, b.shape[0] - 1, 0)
        return _gelu_tanh(a + b_next + pe[0:1, :]).astype(BF16)

    kc_ref[...] = _dot(hidden(kch_ref, pek_ref, wk1_ref), wk2_ref[...]).astype(BF16)
    vcT_ref[...] = _dot_nt(wv2t_ref[...], hidden(vch_ref, pev_ref, wv1_ref)).astype(BF16)


def _compress(kch, vch, pek, pev, wk1, wv1, wk2, wv2t, B, ncp):
    cw = CMP_STRIDE * KV_DIM
    gh = N_KV_GROUPS * CMP_HIDDEN
    c3 = lambda b: (0, 0, 0)
    c2 = lambda b: (0, 0)
    return pl.pallas_call(
        _compress_kernel,
        grid=(B,),
        in_specs=[
            pl.BlockSpec((None, ncp, cw), lambda b: (b, 0, 0)),
            pl.BlockSpec((None, ncp, cw), lambda b: (b, 0, 0)),
            pl.BlockSpec((2, 8, cw), c3), pl.BlockSpec((2, 8, cw), c3),
            pl.BlockSpec((2, cw, gh), c3), pl.BlockSpec((2, cw, gh), c3),
            pl.BlockSpec((gh, KV_DIM), c2), pl.BlockSpec((KV_DIM, gh), c2),
        ],
        out_specs=(pl.BlockSpec((None, ncp, KV_DIM), lambda b: (b, 0, 0)),
                   pl.BlockSpec((None, KV_DIM, ncp), lambda b: (b, 0, 0))),
        out_shape=(jax.ShapeDtypeStruct((B, ncp, KV_DIM), BF16), jax.ShapeDtypeStruct((B, KV_DIM, ncp), BF16)),
        compiler_params=pltpu.CompilerParams(dimension_semantics=("arbitrary",), vmem_limit_bytes=VMEM_LIMIT),
        name="compress",
    )(kch, vch, pek, pev, wk1, wv1, wk2, wv2t)


def _attn_kernel(qT_ref, qrT_ref, kc_ref, vcT_ref, ksel_ref, vselT_ref, kwin_ref, vwinT_ref, gT_ref, mT_ref,
                 o_ref, sel_scr):
    tq = qT_ref.shape[1]
    tk = ksel_ref.shape[1]
    nq = HEADS_PER_GROUP * tq
    ns = mT_ref.shape[0]
    g = pl.program_id(1)
    p = pl.program_id(2)
    qbase = p * tq

    def stack_q(ref):
        q4 = jnp.concatenate([ref[z * HEAD_DIM:(z + 1) * HEAD_DIM, :] for z in range(HEADS_PER_GROUP)], axis=1)
        q8 = jnp.concatenate([q4, q4], axis=0)
        rows = lax.broadcasted_iota(jnp.int32, q8.shape, 0)
        return jnp.where((rows // HEAD_DIM) == g, q8, jnp.zeros_like(q8))

    def lanes4(a):
        return jnp.concatenate([a] * HEADS_PER_GROUP, axis=1)

    q_pad = stack_q(qT_ref)
    s_c = _dot(kc_ref[...], q_pad)
    ncp = s_c.shape[0]
    n_idx = lax.broadcasted_iota(jnp.int32, (ncp, tq), 0)
    t_idx = qbase + lax.broadcasted_iota(jnp.int32, (ncp, tq), 1)
    ok = (n_idx * CMP_STRIDE + (CMP_BLOCK - 1)) <= t_idx
    bias_c = lanes4(jnp.where(ok, 0.0, NEG).astype(F32))
    keep_c = lanes4(ok.astype(F32))
    s_c = s_c + bias_c
    m_c = jnp.max(s_c, axis=0, keepdims=True)
    e_c = jnp.exp(s_c - m_c) * keep_c
    inv_c = 1.0 / jnp.maximum(jnp.sum(e_c, axis=0, keepdims=True), 1e-30)
    o_cmp = _dot(vcT_ref[...], e_c.astype(BF16)) * inv_c
    imp4 = jnp.dot(mT_ref[...], e_c, preferred_element_type=F32, precision=lax.Precision.HIGHEST) * inv_c
    imp = imp4[:, 0:tq]
    for z in range(1, HEADS_PER_GROUP):
        imp = imp + imp4[:, z * tq:(z + 1) * tq]

    j_idx = lax.broadcasted_iota(jnp.int32, (ns, tq), 0)
    cur = (qbase + lax.broadcasted_iota(jnp.int32, (ns, tq), 1)) // SEL_BLOCK
    forced = (j_idx == 0) | (j_idx == cur) | (j_idx == cur - 1)
    impv = jnp.where(j_idx <= cur, jnp.where(forced, FORCE, imp), NEG)
    cnt = jnp.zeros((ns, tq), F32)
    for jp in range(ns):
        row = impv[jp:jp + 1, :]
        beats = (row > impv) | ((row == impv) & (j_idx > jp))
        cnt = cnt + beats.astype(F32)
    sel = ((cnt < float(min(N_SEL, ns))) & (impv > 0.5 * NEG)).astype(F32)
    bpt = tk // SEL_BLOCK
    for kt in range(ns // bpt):
        sel_scr[kt, 0:bpt, :] = sel[kt * bpt:(kt + 1) * bpt, :]

    qr_pad = stack_q(qrT_ref)
    d_rc = lax.broadcasted_iota(jnp.int32, (tk, tq), 0) - lax.broadcasted_iota(jnp.int32, (tk, tq), 1)
    krow = lax.broadcasted_iota(jnp.int32, (tk, tq), 0)

    def flash(k_ref, vT_ref, lo, hi, mask_fn):
        def body(kt, carry):
            m, l, acc = carry
            ok = mask_fn(kt)
            s = _dot(k_ref[kt], qr_pad) + lanes4(jnp.where(ok, 0.0, NEG).astype(F32))
            m_new = jnp.maximum(m, jnp.max(s, axis=0, keepdims=True))
            a = jnp.exp(m - m_new)
            e = jnp.exp(s - m_new) * lanes4(ok.astype(F32))
            l = a * l + jnp.sum(e, axis=0, keepdims=True)
            acc = a * acc + _dot(vT_ref[kt], e.astype(BF16))
            return m_new, l, acc

        init = (jnp.full((1, nq), NEG, F32), jnp.zeros((1, nq), F32), jnp.zeros((HEAD_DIM, nq), F32))
        _, l, acc = lax.fori_loop(lo, hi, body, init)
        return acc * (1.0 / jnp.maximum(l, 1e-30))

    def sel_mask(kt):
        c = qbase - kt * tk
        blk = sel_scr[kt, 0:bpt, :]
        chosen = blk[0:1, :]
        for i in range(1, bpt):
            chosen = jnp.where(krow >= i * SEL_BLOCK, blk[i:i + 1, :], chosen)
        return (chosen > 0.5) & (d_rc <= c)

    def win_mask(kt):
        c = qbase - kt * tk
        return (d_rc <= c) & (d_rc > c - WINDOW)

    o_sel = flash(ksel_ref, vselT_ref, 0, (qbase + tq - 1) // tk + 1, sel_mask)
    win_lo = jnp.maximum(qbase - (WINDOW - 1), 0) // tk
    o_win = flash(kwin_ref, vwinT_ref, win_lo, (qbase + tq - 1) // tk + 1, win_mask)

    gt = gT_ref[...]

    def gate(br):
        return jnp.concatenate([gt[br * HEADS_PER_GROUP + z:br * HEADS_PER_GROUP + z + 1, :]
                                for z in range(HEADS_PER_GROUP)], axis=1)

    oT = gate(0) * o_cmp + gate(1) * o_sel + gate(2) * o_win
    for i in range(HEADS_PER_GROUP // 2):
        pair = jnp.concatenate([oT[:, (2 * i) * tq:(2 * i + 1) * tq], oT[:, (2 * i + 1) * tq:(2 * i + 2) * tq]], axis=0)
        o_ref[:, i * 2 * HEAD_DIM:(i + 1) * 2 * HEAD_DIM] = pair.T.astype(BF16)


def _attn(qT, qrT, kc, vcT, ksel, vselT, kwin, vwinT, gT, mT, B, S):
    tq, tk = ATT_TQ, ATT_TK
    P = S // tq
    ncp = kc.shape[1]
    ns = S // SEL_BLOCK
    gz = HEADS_PER_GROUP * HEAD_DIM
    qspec = pl.BlockSpec((None, gz, tq), lambda b, g, p: (b, g, p))
    kspec = pl.BlockSpec((None, S // tk, tk, KV_DIM), lambda b, g, p: (b, 0, 0, 0))
    vspec = pl.BlockSpec((None, S // tk, HEAD_DIM, tk), lambda b, g, p: (b, 0, g, 0))
    return pl.pallas_call(
        _attn_kernel,
        grid=(B, N_KV_GROUPS, P),
        in_specs=[
            qspec, qspec,
            pl.BlockSpec((None, ncp, KV_DIM), lambda b, g, p: (b, 0, 0)),
            pl.BlockSpec((None, HEAD_DIM, ncp), lambda b, g, p: (b, g, 0)),
            kspec, vspec, kspec, vspec,
            pl.BlockSpec((None, GATE_ROWS, tq), lambda b, g, p: (b, g, p)),
            pl.BlockSpec((ns, ncp), lambda b, g, p: (0, 0)),
        ],
        out_specs=pl.BlockSpec((tq, gz), lambda b, g, p: (b * P + p, g)),
        out_shape=jax.ShapeDtypeStruct((B * S, NSA_DIM), BF16),
        scratch_shapes=[pltpu.VMEM((S // tk, 8, tq), F32)],
        compiler_params=pltpu.CompilerParams(
            dimension_semantics=("arbitrary", "arbitrary", "arbitrary"), vmem_limit_bytes=VMEM_LIMIT),
        name="attn",
    )(qT, qrT, kc, vcT, ksel, vselT, kwin, vwinT, gT, mT)


def _layer_norm(z, g, b):
    mu = jnp.mean(z, axis=-1, keepdims=True)
    zc = z - mu
    var = jnp.mean(zc * zc, axis=-1, keepdims=True)
    return zc * lax.rsqrt(var + LN_EPS) * g + b


def _tail_kernel(alpha, x_ref, mix_ref, o_ref, wg_ref, wco_ref, wno_ref, wo_ref, g1_ref, b1_ref,
                 wup_ref, wdn_ref, g2_ref, b2_ref, out_ref):
    x = x_ref[...]
    xb = x.astype(BF16)
    y_conv = _dot(mix_ref[...], wco_ref[...])
    y_nsa = _dot(o_ref[...], wno_ref[...])
    mixed = (jax.nn.sigmoid(_dot(xb, wg_ref[:, 0:D_MODEL])) * y_conv
             + jax.nn.sigmoid(_dot(xb, wg_ref[:, D_MODEL:2 * D_MODEL])) * y_nsa)
    x1 = _layer_norm(alpha * x + _dot(mixed.astype(BF16), wo_ref[...]), g1_ref[...], b1_ref[...])
    x1b = x1.astype(BF16)
    ff = jnp.zeros_like(x1)
    for c in range(D_FF // FF_CHUNK):
        h = jnp.maximum(_dot(x1b, wup_ref[:, c * FF_CHUNK:(c + 1) * FF_CHUNK]), 0.0)
        ff = ff + _dot((h * h).astype(BF16), wdn_ref[c * FF_CHUNK:(c + 1) * FF_CHUNK, :])
    out_ref[...] = _layer_norm(alpha * x1 + ff, g2_ref[...], b2_ref[...])


def _tail(x2, mix, o, wg, wco, wno, wo, g1, b1, wup, wdn, g2, b2, alpha):
    n = x2.shape[0]
    tm = TAIL_TM
    row = lambda i: (i, 0)
    const = lambda i: (0, 0)
    resident = lambda shape: pl.BlockSpec(shape, const, pipeline_mode=pl.Buffered(1))
    return pl.pallas_call(
        functools.partial(_tail_kernel, alpha),
        grid=(n // tm,),
        in_specs=[
            pl.BlockSpec((tm, D_MODEL), row), pl.BlockSpec((tm, CONV_DIM), row), pl.BlockSpec((tm, NSA_DIM), row),
            resident((D_MODEL, 2 * D_MODEL)), resident((CONV_DIM, D_MODEL)), resident((NSA_DIM, D_MODEL)),
            resident((D_MODEL, D_MODEL)), resident((1, D_MODEL)), resident((1, D_MODEL)),
            resident((D_MODEL, D_FF)), resident((D_FF, D_MODEL)), resident((1, D_MODEL)), resident((1, D_MODEL)),
        ],
        out_specs=pl.BlockSpec((tm, D_MODEL), row),
        out_shape=jax.ShapeDtypeStruct((n, D_MODEL), F32),
        compiler_params=pltpu.CompilerParams(dimension_semantics=("arbitrary",), vmem_limit_bytes=VMEM_LIMIT),
        name="tail",
    )(x2, mix, o, wg, wco, wno, wo, g1, b1, wup, wdn, g2, b2)


def _rope_tables(S):
    inv = ROPE_THETA ** (-jnp.arange(0, ROPE_DIM, 2, dtype=F32) / ROPE_DIM)
    ang = jnp.arange(S, dtype=F32)[:, None] * inv[None, :]
    cos, sin = jnp.cos(ang), jnp.sin(ang)
    pad_c = jnp.ones((S, HEAD_DIM - ROPE_DIM), F32)
    pad_s = jnp.zeros((S, HEAD_DIM - ROPE_DIM), F32)
    ck = jnp.tile(jnp.concatenate([cos, cos, pad_c], axis=1), (1, N_KV_GROUPS))
    sk = jnp.tile(jnp.concatenate([-sin, sin, pad_s], axis=1), (1, N_KV_GROUPS))
    return ck, sk, cos.T, sin.T


def _overlap_matrix_t(ncp, ns):
    nc = ncp - 1
    i = np.arange(ncp)[None, :]
    j = np.arange(ns)[:, None]
    m = (i * CMP_STRIDE < (j + 1) * SEL_BLOCK) & (i * CMP_STRIDE + CMP_BLOCK > j * SEL_BLOCK) & (i < nc)
    return jnp.asarray(m.astype(np.float32))


def _gate_perm():
    idx = np.zeros((N_KV_GROUPS, GATE_ROWS), np.int64)
    valid = np.zeros((N_KV_GROUPS, GATE_ROWS), np.float32)
    for g in range(N_KV_GROUPS):
        for br in range(N_NSA_BRANCHES):
            for z in range(HEADS_PER_GROUP):
                idx[g, br * HEADS_PER_GROUP + z] = O_GATE + (g * HEADS_PER_GROUP + z) * N_NSA_BRANCHES + br
                valid[g, br * HEADS_PER_GROUP + z] = 1.0
    return idx.reshape(-1), valid.reshape(-1)


def _compress_weights(pe, w1, w2):
    eye = jnp.eye(N_KV_GROUPS, dtype=F32)
    w1h = w1.reshape(2, CMP_STRIDE, HEAD_DIM, CMP_HIDDEN)
    w1big = jnp.einsum('ab,srdh->sradbh', eye, w1h).reshape(
        2, CMP_STRIDE * KV_DIM, N_KV_GROUPS * CMP_HIDDEN)
    peh = pe.reshape(2, CMP_STRIDE, 1, HEAD_DIM)
    pebig = jnp.broadcast_to(peh, (2, CMP_STRIDE, N_KV_GROUPS, HEAD_DIM)).reshape(2, 1, CMP_STRIDE * KV_DIM)
    pebig = jnp.broadcast_to(pebig, (2, 8, CMP_STRIDE * KV_DIM))
    w2big = jnp.einsum('ab,hd->ahbd', eye, w2).reshape(N_KV_GROUPS * CMP_HIDDEN, KV_DIM)
    return pebig.astype(BF16), w1big.astype(BF16), w2big.astype(BF16)


def _layer(x2, B, S, w_in, conv_w, w_conv_out, pe_k, wk1, wk2, pe_v, wv1, wv2, w_nsa_out, w_o,
           ln1_g, ln1_b, w_up, w_down, ln2_g, ln2_b, alpha):
    ncp = S // CMP_STRIDE
    ns = S // SEL_BLOCK
    gidx, gvalid = _gate_perm()
    wstd = jnp.concatenate([w_in[:, O_H:O_Q], w_in[:, O_KCMP:O_KSEL], w_in[:, O_KSEL:O_VSEL],
                            w_in[:, O_KWIN:O_VWIN]], axis=1).astype(BF16)
    wtr = jnp.concatenate([w_in[:, O_Q:O_KCMP], w_in[:, O_VSEL:O_KWIN], w_in[:, O_VWIN:O_GATE],
                           w_in[:, gidx] * gvalid[None, :]], axis=1).T.astype(BF16)
    wg = w_in[:, O_GCONV:O_END].astype(BF16)
    ck, sk, cq, sq = _rope_tables(S)

    mix, kcmp, vcmp, ksel, kwin, qT, qrT, vselT, vwinT, gT = _proj(
        x2, wstd, wtr, conv_w.reshape(CONV_WIDTH, CONV_DIM), ck, sk, cq, sq, B, S)

    pek, wk1b, wk2b = _compress_weights(pe_k, wk1, wk2)
    pev, wv1b, wv2b = _compress_weights(pe_v, wv1, wv2)
    kc, vcT = _compress(kcmp.reshape(B, ncp, CMP_STRIDE * KV_DIM), vcmp.reshape(B, ncp, CMP_STRIDE * KV_DIM),
                        pek, pev, wk1b, wv1b, wk2b, wv2b.T, B, ncp)

    tk = ATT_TK
    o = _attn(qT, qrT, kc, vcT, ksel.reshape(B, S // tk, tk, KV_DIM), vselT,
              kwin.reshape(B, S // tk, tk, KV_DIM), vwinT, gT, _overlap_matrix_t(ncp, ns), B, S)

    row = lambda v: v.reshape(1, D_MODEL).astype(F32)
    return _tail(x2, mix, o, wg, w_conv_out.astype(BF16), w_nsa_out.astype(BF16), w_o.astype(BF16),
                 row(ln1_g), row(ln1_b), w_up.astype(BF16), w_down.astype(BF16), row(ln2_g), row(ln2_b), alpha)


def kernel(x, w_in, conv_w, w_conv_out, pe_k_cmp, w_k_cmp1, w_k_cmp2, pe_v_cmp, w_v_cmp1, w_v_cmp2,
           w_nsa_out, w_o, ln1_g, ln1_b, w_up, w_down, ln2_g, ln2_b):
    B, S, D = x.shape
    depth = w_in.shape[0]
    assert D == D_MODEL and w_in.shape[2] == O_END
    assert S % PROJ_TM == 0 and S % ATT_TQ == 0 and (B * S) % TAIL_TM == 0 and S >= WINDOW
    alpha = float((2 * depth) ** 0.25)
    x2 = x.reshape(B * S, D)
    for l in range(depth):
        x2 = _layer(x2, B, S, w_in[l], conv_w[l], w_conv_out[l], pe_k_cmp[l], w_k_cmp1[l], w_k_cmp2[l],
                    pe_v_cmp[l], w_v_cmp1[l], w_v_cmp2[l], w_nsa_out[l], w_o[l], ln1_g[l], ln1_b[l],
                    w_up[l], w_down[l], ln2_g[l], ln2_b[l], alpha)
    return x2.reshape(B, S, D)
```

```python
import functools

import numpy as np
import jax
import jax.numpy as jnp
from jax import lax
from jax.experimental import pallas as pl
from jax.experimental.pallas import tpu as pltpu

F32 = jnp.float32
BF16 = jnp.bfloat16

D_MODEL = 1024
CONV_DIM = D_MODEL // 2
CONV_WIDTH = 3
N_HEADS = 8
HEAD_DIM = 64
N_KV_GROUPS = 2
HEADS_PER_GROUP = N_HEADS // N_KV_GROUPS
NSA_DIM = N_HEADS * HEAD_DIM
KV_DIM = N_KV_GROUPS * HEAD_DIM
ROPE_DIM = HEAD_DIM // 4
ROPE_HALF = ROPE_DIM // 2
ROPE_THETA = 500000.0
CMP_BLOCK = 32
CMP_STRIDE = 16
CMP_HIDDEN = 2 * HEAD_DIM
SEL_BLOCK = 64
N_SEL = 16
WINDOW = 512
N_NSA_BRANCHES = 3
D_FF = 4 * D_MODEL
LN_EPS = 1e-5
NEG = -1e30
FORCE = 1e9
SCALE = HEAD_DIM ** -0.5
LOG2E = 1.4426950408889634

SUBLANES = 8
LANES = 128
BF16_ROWS = 16
V_ROWS = HEAD_DIM + BF16_ROWS

_SPLITS = (CONV_DIM, CONV_DIM, CONV_DIM, NSA_DIM, KV_DIM, KV_DIM, KV_DIM, KV_DIM, KV_DIM, KV_DIM,
           N_HEADS * N_NSA_BRANCHES, D_MODEL, D_MODEL)
_OFFS = np.concatenate([[0], np.cumsum(_SPLITS)]).tolist()
(O_H, O_B, O_C, O_Q, O_KCMP, O_VCMP, O_KSEL, O_VSEL, O_KWIN, O_VWIN, O_GATE, O_GCONV, O_GNSA, O_END) = _OFFS

GATE_ROWS = 16
N_STD = 3 * CONV_DIM + 4 * KV_DIM
N_TR = NSA_DIM + 2 * KV_DIM + N_KV_GROUPS * GATE_ROWS

PROJ_TM = 256
ATT_TQ = 256
ATT_TK = 256
TAIL_TM = 256
FF_CHUNK = 1024
VMEM_LIMIT = 56 * 1024 * 1024


def _dot(a, b):
    return jnp.dot(a, b, preferred_element_type=F32)


def _dot_nt(a, b):
    return lax.dot_general(a, b, (((1,), (1,)), ((), ())), preferred_element_type=F32)


def _proj_kernel(x_ref, wstd_ref, wtr_ref, convw_ref, cosk_ref, sink_ref, cosq_ref, sinq_ref,
                 mix_ref, kcmp_ref, vcmp_ref, ksel_ref, kwin_ref,
                 qT_ref, qrT_ref, vselT_ref, vwinT_ref, gT_ref, ubuf):
    tm = x_ref.shape[0]
    s = pl.program_id(1)
    xb = x_ref[...].astype(BF16)

    hbc = _dot(xb, wstd_ref[:, 0:3 * CONV_DIM])
    u = hbc[:, 2 * CONV_DIM:3 * CONV_DIM] * hbc[:, 0:CONV_DIM]

    @pl.when(s == 0)
    def _():
        ubuf[0:SUBLANES, :] = jnp.zeros((SUBLANES, CONV_DIM), F32)

    ubuf[SUBLANES:SUBLANES + tm, :] = u
    u1 = ubuf[SUBLANES - 1:SUBLANES - 1 + tm, :]
    u2 = ubuf[SUBLANES - 2:SUBLANES - 2 + tm, :]
    cw = convw_ref[...]
    y = cw[0:1, :] * u2 + cw[1:2, :] * u1 + cw[2:3, :] * u
    mix_ref[...] = (hbc[:, CONV_DIM:2 * CONV_DIM] * y).astype(BF16)
    ubuf[0:SUBLANES, :] = ubuf[tm:tm + SUBLANES, :]

    kk = _dot(xb, wstd_ref[:, 3 * CONV_DIM:N_STD])
    kcmp_ref[...] = kk[:, 0:KV_DIM].astype(BF16)
    vcmp_ref[...] = kk[:, KV_DIM:2 * KV_DIM].astype(BF16)
    lane = lax.broadcasted_iota(jnp.int32, (tm, KV_DIM), 1)
    first_half = (lane % ROPE_DIM) < ROPE_HALF
    ck = cosk_ref[...]
    sk = sink_ref[...]
    for j, out in ((2, ksel_ref), (3, kwin_ref)):
        k = kk[:, j * KV_DIM:(j + 1) * KV_DIM]
        partner = jnp.where(first_half, pltpu.roll(k, KV_DIM - ROPE_HALF, 1), pltpu.roll(k, ROPE_HALF, 1))
        out[...] = (k * ck + partner * sk).astype(BF16)

    yT = _dot_nt(wtr_ref[...], xb)
    q = yT[0:NSA_DIM, :] * (SCALE * LOG2E)
    qT_ref[...] = q.astype(BF16)
    qrT_ref[...] = q.astype(BF16)
    cq = cosq_ref[...]
    sq = sinq_ref[...]
    for h in range(N_HEADS):
        r0 = h * HEAD_DIM
        x1 = q[r0:r0 + ROPE_HALF, :]
        x2 = q[r0 + ROPE_HALF:r0 + ROPE_DIM, :]
        rot = jnp.concatenate([x1 * cq - x2 * sq, x2 * cq + x1 * sq], axis=0)
        qrT_ref[r0:r0 + ROPE_DIM, :] = rot.astype(BF16)
    tk = vselT_ref.shape[-1]
    ones = jnp.ones((BF16_ROWS, tk), BF16)
    for j, out in enumerate((vselT_ref, vwinT_ref)):
        for i in range(tm // tk):
            for g in range(N_KV_GROUPS):
                r0 = NSA_DIM + j * KV_DIM + g * HEAD_DIM
                out[i, g * V_ROWS:g * V_ROWS + HEAD_DIM, :] = yT[r0:r0 + HEAD_DIM, i * tk:(i + 1) * tk].astype(BF16)
                out[i, g * V_ROWS + HEAD_DIM:(g + 1) * V_ROWS, :] = ones
    gT_ref[...] = jax.nn.sigmoid(yT[NSA_DIM + 2 * KV_DIM:N_TR, :])


def _proj(x2, wstd, wtr, convw, cosk, sink, cosq, sinq, B, S):
    tm, tk = PROJ_TM, ATT_TK
    ns = S // tm
    row = lambda b, s: (b * ns + s, 0)
    const = lambda b, s: (0, 0)
    tok = lambda c, dt: jax.ShapeDtypeStruct((B * S, c), dt)
    out_shape = (
        tok(CONV_DIM, BF16), tok(KV_DIM, BF16), tok(KV_DIM, BF16), tok(KV_DIM, BF16), tok(KV_DIM, BF16),
        jax.ShapeDtypeStruct((B, NSA_DIM, S), BF16), jax.ShapeDtypeStruct((B, NSA_DIM, S), BF16),
        jax.ShapeDtypeStruct((B, S // tk, N_KV_GROUPS * V_ROWS, tk), BF16),
        jax.ShapeDtypeStruct((B, S // tk, N_KV_GROUPS * V_ROWS, tk), BF16),
        jax.ShapeDtypeStruct((B, N_KV_GROUPS * GATE_ROWS, S), F32),
    )
    feat = lambda r: pl.BlockSpec((None, r, tm), lambda b, s: (b, 0, s))
    vt = pl.BlockSpec((None, tm // tk, N_KV_GROUPS * V_ROWS, tk), lambda b, s: (b, s, 0, 0))
    return pl.pallas_call(
        _proj_kernel,
        grid=(B, ns),
        in_specs=[
            pl.BlockSpec((tm, D_MODEL), row),
            pl.BlockSpec((D_MODEL, N_STD), const),
            pl.BlockSpec((N_TR, D_MODEL), const),
            pl.BlockSpec((CONV_WIDTH, CONV_DIM), const),
            pl.BlockSpec((tm, KV_DIM), lambda b, s: (s, 0)),
            pl.BlockSpec((tm, KV_DIM), lambda b, s: (s, 0)),
            pl.BlockSpec((ROPE_HALF, tm), lambda b, s: (0, s)),
            pl.BlockSpec((ROPE_HALF, tm), lambda b, s: (0, s)),
        ],
        out_specs=(
            pl.BlockSpec((tm, CONV_DIM), row), pl.BlockSpec((tm, KV_DIM), row), pl.BlockSpec((tm, KV_DIM), row),
            pl.BlockSpec((tm, KV_DIM), row), pl.BlockSpec((tm, KV_DIM), row),
            feat(NSA_DIM), feat(NSA_DIM), vt, vt, feat(N_KV_GROUPS * GATE_ROWS),
        ),
        out_shape=out_shape,
        scratch_shapes=[pltpu.VMEM((tm + SUBLANES, CONV_DIM), F32)],
        compiler_params=pltpu.CompilerParams(
            dimension_semantics=("arbitrary", "arbitrary"), vmem_limit_bytes=VMEM_LIMIT),
        name="proj",
    )(x2, wstd, wtr, convw, cosk, sink, cosq, sinq)


def _gelu_tanh(x):
    return x * (0.5 * (1.0 + jnp.tanh(np.sqrt(2.0 / np.pi).astype(np.float32) * (x + 0.044715 * (x * x * x)))))


def _compress_kernel(kch_ref, vch_ref, pek_ref, pev_ref, wk1_ref, wv1_ref, wk2_ref, wv2t_ref, kc_ref, vcT_ref):
    def hidden(ch_ref, pe_ref, w1_ref):
        ch = ch_ref[...]
        a = _dot(ch, w1_ref[0])
        b = _dot(ch, w1_ref[1])
        pe = _dot(pe_ref[0], w1_ref[0]) + _dot(pe_ref[1], w1_ref[1])
        b_next = pltpu.roll(b, b.shape[0] - 1, 0)
        return _gelu_tanh(a + b_next + pe[0:1, :]).astype(BF16)

    kc_ref[...] = _dot(hidden(kch_ref, pek_ref, wk1_ref), wk2_ref[...]).astype(BF16)
    vcT_ref[...] = _dot_nt(wv2t_ref[...], hidden(vch_ref, pev_ref, wv1_ref)).astype(BF16)


def _compress(kch, vch, pek, pev, wk1, wv1, wk2, wv2t, B, ncp):
    cw = CMP_STRIDE * KV_DIM
    gh = N_KV_GROUPS * CMP_HIDDEN
    c3 = lambda b: (0, 0, 0)
    c2 = lambda b: (0, 0)
    return pl.pallas_call(
        _compress_kernel,
        grid=(B,),
        in_specs=[
            pl.BlockSpec((None, ncp, cw), lambda b: (b, 0, 0)),
            pl.BlockSpec((None, ncp, cw), lambda b: (b, 0, 0)),
            pl.BlockSpec((2, 8, cw), c3), pl.BlockSpec((2, 8, cw), c3),
            pl.BlockSpec((2, cw, gh), c3), pl.BlockSpec((2, cw, gh), c3),
            pl.BlockSpec((gh, KV_DIM), c2), pl.BlockSpec((KV_DIM, gh), c2),
        ],
        out_specs=(pl.BlockSpec((None, ncp, KV_DIM), lambda b: (b, 0, 0)),
                   pl.BlockSpec((None, KV_DIM, ncp), lambda b: (b, 0, 0))),
        out_shape=(jax.ShapeDtypeStruct((B, ncp, KV_DIM), BF16), jax.ShapeDtypeStruct((B, KV_DIM, ncp), BF16)),
        compiler_params=pltpu.CompilerParams(dimension_semantics=("arbitrary",), vmem_limit_bytes=VMEM_LIMIT),
        name="compress",
    )(kch, vch, pek, pev, wk1, wv1, wk2, wv2t)


def _attn_kernel(qT_ref, qrT_ref, kc_ref, vcT_ref, ksel_ref, vselT_ref, kwin_ref, vwinT_ref, gT_ref, mT_ref,
                 tri_ref, o_ref, selb_scr, q_scr, m_scr, acc_scr, oT_scr):
    tq = qT_ref.shape[1]
    tk = ksel_ref.shape[1]
    nsub = tq // LANES
    nchunk = N_HEADS * nsub
    gchunk = HEADS_PER_GROUP * nsub
    gl = HEADS_PER_GROUP * tq
    ns = mT_ref.shape[0]
    bpt = tk // SEL_BLOCK
    p = pl.program_id(1)
    qbase = p * tq

    def chunk(c):
        return slice(c * LANES, (c + 1) * LANES)

    def sub(tsub):
        return slice(tsub * LANES, (tsub + 1) * LANES)

    def stage_q(ref):
        zeros = jnp.zeros((HEAD_DIM, tq), BF16)
        for h in range(N_HEADS):
            qh = ref[h * HEAD_DIM:(h + 1) * HEAD_DIM, :]
            halves = [qh, zeros] if h // HEADS_PER_GROUP == 0 else [zeros, qh]
            q_scr[:, h * tq:(h + 1) * tq] = jnp.concatenate(halves, axis=0)

    gt = gT_ref[...]

    def gate(br, c):
        h, tsub = divmod(c, nsub)
        g, z = divmod(h, HEADS_PER_GROUP)
        r = g * GATE_ROWS + br * HEADS_PER_GROUP + z
        return gt[r:r + 1, sub(tsub)]

    stage_q(qT_ref)
    kc = kc_ref[...]
    mT = mT_ref[...]
    ncp = kc.shape[0]
    s_cmp = _dot(kc, q_scr[...])
    imp = []
    for g in range(N_KV_GROUPS):
        vcT = vcT_ref[g * HEAD_DIM:(g + 1) * HEAD_DIM, :]
        imp_parts = []
        for tsub in range(nsub):
            n_idx = lax.broadcasted_iota(jnp.int32, (ncp, LANES), 0)
            t_idx = qbase + tsub * LANES + lax.broadcasted_iota(jnp.int32, (ncp, LANES), 1)
            ok = (n_idx * CMP_STRIDE + (CMP_BLOCK - 1)) <= t_idx
            bias = jnp.where(ok, 0.0, NEG).astype(F32)
            keep = ok.astype(F32)
            imp_t = None
            for z in range(HEADS_PER_GROUP):
                c = (g * HEADS_PER_GROUP + z) * nsub + tsub
                s = s_cmp[:, chunk(c)] + bias
                e = jnp.exp2(s - jnp.max(s, axis=0, keepdims=True)) * keep
                inv = 1.0 / jnp.maximum(jnp.sum(e, axis=0, keepdims=True), 1e-30)
                oT_scr[:, chunk(c)] = _dot(vcT, e.astype(BF16)) * (inv * gate(0, c))
                part = jnp.dot(mT, e, preferred_element_type=F32, precision=lax.Precision.HIGHEST) * inv
                imp_t = part if imp_t is None else imp_t + part
            imp_parts.append(imp_t)
        imp.append(jnp.concatenate(imp_parts, axis=1))

    j_idx = lax.broadcasted_iota(jnp.int32, (ns, tq), 0)
    cur = (qbase + lax.broadcasted_iota(jnp.int32, (ns, tq), 1)) // SEL_BLOCK
    forced = (j_idx == 0) | (j_idx == cur) | (j_idx == cur - 1)
    for g in range(N_KV_GROUPS):
        impv = jnp.where(j_idx <= cur, jnp.where(forced, FORCE, imp[g]), NEG)
        cnt = jnp.zeros((ns, tq), F32)
        for jp in range(ns):
            row = impv[jp:jp + 1, :]
            beats = (row > impv) | ((row == impv) & (j_idx > jp))
            cnt = cnt + beats.astype(F32)
        sel = (cnt < float(min(N_SEL, ns))) & (impv > 0.5 * NEG)
        selb = jnp.where(sel, 0.0, NEG).astype(F32)
        for kt in range(ns // bpt):
            selb_scr[g, kt, 0:bpt, :] = selb[kt * bpt:(kt + 1) * bpt, :]

    stage_q(qrT_ref)

    def reset():
        m_scr[...] = jnp.full(m_scr.shape, NEG, F32)
        acc_scr[...] = jnp.zeros(acc_scr.shape, F32)

    def tile_update(k_t, vT_t, block_bias, tri):
        s_all = _dot(k_t, q_scr[...])
        for g in range(N_KV_GROUPS):
            es, alphas = [], []
            for cg in range(gchunk):
                c = g * gchunk + cg
                tsub = c % nsub
                s = s_all[:, chunk(c)]
                if block_bias is not None:
                    rb = block_bias[g][:, sub(tsub)]
                    s = s + jnp.concatenate(
                        [jnp.broadcast_to(rb[i:i + 1, :], (SEL_BLOCK, LANES)) for i in range(bpt)], axis=0)
                if tri is not None:
                    s = s + tri_ref[tri, :, sub(tsub)]
                m_old = m_scr[:, chunk(c)]
                m_new = jnp.maximum(m_old, jnp.max(s, axis=0, keepdims=True))
                es.append(jnp.exp2(s - m_new).astype(BF16))
                alphas.append(jnp.exp2(m_old - m_new))
                m_scr[:, chunk(c)] = m_new
            lanes = slice(g * gl, (g + 1) * gl)
            pv = _dot(vT_t[g * V_ROWS:(g + 1) * V_ROWS, :], jnp.concatenate(es, axis=1))
            acc_scr[:, lanes] = jnp.concatenate(alphas, axis=1) * acc_scr[:, lanes] + pv

    def finish(br):
        for c in range(nchunk):
            acc = acc_scr[:, chunk(c)]
            inv = 1.0 / jnp.maximum(acc[HEAD_DIM:HEAD_DIM + 1, :], 1e-30)
            oT_scr[:, chunk(c)] += acc[0:HEAD_DIM, :] * (inv * gate(br, c))

    def block_bias(kt):
        return [selb_scr[g, kt, 0:bpt, :] for g in range(N_KV_GROUPS)]

    reset()

    def sel_body(kt, carry):
        tile_update(ksel_ref[kt], vselT_ref[kt], block_bias(kt), None)
        return carry

    lax.fori_loop(0, p, sel_body, 0)
    tile_update(ksel_ref[p], vselT_ref[p], block_bias(p), TRI_CAUSAL)
    finish(1)

    reset()
    tile_update(kwin_ref[p], vwinT_ref[p], None, TRI_CAUSAL)
    n_back = WINDOW // tk
    for j in range(1, n_back + 1):
        @pl.when(p >= j)
        def _():
            tile_update(kwin_ref[p - j], vwinT_ref[p - j], None, TRI_EDGE if j == n_back else None)
    finish(2)

    for tsub in range(nsub):
        for i in range(N_HEADS // 2):
            pair = jnp.concatenate([oT_scr[:, chunk((2 * i) * nsub + tsub)],
                                    oT_scr[:, chunk((2 * i + 1) * nsub + tsub)]], axis=0)
            o_ref[sub(tsub), i * 2 * HEAD_DIM:(i + 1) * 2 * HEAD_DIM] = pair.T.astype(BF16)


TRI_CAUSAL = 0
TRI_EDGE = 1


def _tri_bias(t):
    r = np.arange(t)[:, None]
    c = np.arange(t)[None, :]
    return jnp.asarray(np.stack([np.where(r <= c, 0.0, NEG), np.where(r > c, 0.0, NEG)]).astype(np.float32))


def _attn(qT, qrT, kc, vcT, ksel, vselT, kwin, vwinT, gT, mT, B, S):
    tq, tk = ATT_TQ, ATT_TK
    assert tq == tk and WINDOW % tk == 0 and tk % SEL_BLOCK == 0 and tq % LANES == 0
    assert tk // SEL_BLOCK <= SUBLANES
    P = S // tq
    ncp = kc.shape[1]
    ns = S // SEL_BLOCK
    nq = N_HEADS * tq
    qspec = pl.BlockSpec((None, NSA_DIM, tq), lambda b, p: (b, 0, p))
    kspec = pl.BlockSpec((None, S // tk, tk, KV_DIM), lambda b, p: (b, 0, 0, 0))
    vspec = pl.BlockSpec((None, S // tk, N_KV_GROUPS * V_ROWS, tk), lambda b, p: (b, 0, 0, 0))
    return pl.pallas_call(
        _attn_kernel,
        grid=(B, P),
        in_specs=[
            qspec, qspec,
            pl.BlockSpec((None, ncp, KV_DIM), lambda b, p: (b, 0, 0)),
            pl.BlockSpec((None, KV_DIM, ncp), lambda b, p: (b, 0, 0)),
            kspec, vspec, kspec, vspec,
            pl.BlockSpec((None, N_KV_GROUPS * GATE_ROWS, tq), lambda b, p: (b, 0, p)),
            pl.BlockSpec((ns, ncp), lambda b, p: (0, 0)),
            pl.BlockSpec((2, tk, tq), lambda b, p: (0, 0, 0)),
        ],
        out_specs=pl.BlockSpec((tq, NSA_DIM), lambda b, p: (b * P + p, 0)),
        out_shape=jax.ShapeDtypeStruct((B * S, NSA_DIM), BF16),
        scratch_shapes=[
            pltpu.VMEM((N_KV_GROUPS, S // tk, SUBLANES, tq), F32),
            pltpu.VMEM((KV_DIM, nq), BF16),
            pltpu.VMEM((1, nq), F32),
            pltpu.VMEM((V_ROWS, nq), F32),
            pltpu.VMEM((HEAD_DIM, nq), F32),
        ],
        compiler_params=pltpu.CompilerParams(
            dimension_semantics=("arbitrary", "arbitrary"), vmem_limit_bytes=VMEM_LIMIT),
        name="attn",
    )(qT, qrT, kc, vcT, ksel, vselT, kwin, vwinT, gT, mT, _tri_bias(tk))


def _layer_norm(z, g, b):
    mu = jnp.mean(z, axis=-1, keepdims=True)
    zc = z - mu
    var = jnp.mean(zc * zc, axis=-1, keepdims=True)
    return zc * lax.rsqrt(var + LN_EPS) * g + b


def _tail_kernel(alpha, x_ref, mix_ref, o_ref, wg_ref, wco_ref, wno_ref, wo_ref, g1_ref, b1_ref,
                 wup_ref, wdn_ref, g2_ref, b2_ref, out_ref):
    x = x_ref[...]
    xb = x.astype(BF16)
    y_conv = _dot(mix_ref[...], wco_ref[...])
    y_nsa = _dot(o_ref[...], wno_ref[...])
    mixed = (jax.nn.sigmoid(_dot(xb, wg_ref[:, 0:D_MODEL])) * y_conv
             + jax.nn.sigmoid(_dot(xb, wg_ref[:, D_MODEL:2 * D_MODEL])) * y_nsa)
    x1 = _layer_norm(alpha * x + _dot(mixed.astype(BF16), wo_ref[...]), g1_ref[...], b1_ref[...])
    x1b = x1.astype(BF16)
    ff = jnp.zeros_like(x1)
    for c in range(D_FF // FF_CHUNK):
        h = jnp.maximum(_dot(x1b, wup_ref[:, c * FF_CHUNK:(c + 1) * FF_CHUNK]), 0.0)
        ff = ff + _dot((h * h).astype(BF16), wdn_ref[c * FF_CHUNK:(c + 1) * FF_CHUNK, :])
    out_ref[...] = _layer_norm(alpha * x1 + ff, g2_ref[...], b2_ref[...])


def _tail(x2, mix, o, wg, wco, wno, wo, g1, b1, wup, wdn, g2, b2, alpha):
    n = x2.shape[0]
    tm = TAIL_TM
    row = lambda i: (i, 0)
    const = lambda i: (0, 0)
    resident = lambda shape: pl.BlockSpec(shape, const, pipeline_mode=pl.Buffered(1))
    return pl.pallas_call(
        functools.partial(_tail_kernel, alpha),
        grid=(n // tm,),
        in_specs=[
            pl.BlockSpec((tm, D_MODEL), row), pl.BlockSpec((tm, CONV_DIM), row), pl.BlockSpec((tm, NSA_DIM), row),
            resident((D_MODEL, 2 * D_MODEL)), resident((CONV_DIM, D_MODEL)), resident((NSA_DIM, D_MODEL)),
            resident((D_MODEL, D_MODEL)), resident((1, D_MODEL)), resident((1, D_MODEL)),
            resident((D_MODEL, D_FF)), resident((D_FF, D_MODEL)), resident((1, D_MODEL)), resident((1, D_MODEL)),
        ],
        out_specs=pl.BlockSpec((tm, D_MODEL), row),
        out_shape=jax.ShapeDtypeStruct((n, D_MODEL), F32),
        compiler_params=pltpu.CompilerParams(dimension_semantics=("arbitrary",), vmem_limit_bytes=VMEM_LIMIT),
        name="tail",
    )(x2, mix, o, wg, wco, wno, wo, g1, b1, wup, wdn, g2, b2)


def _rope_tables(S):
    inv = ROPE_THETA ** (-jnp.arange(0, ROPE_DIM, 2, dtype=F32) / ROPE_DIM)
    ang = jnp.arange(S, dtype=F32)[:, None] * inv[None, :]
    cos, sin = jnp.cos(ang), jnp.sin(ang)
    pad_c = jnp.ones((S, HEAD_DIM - ROPE_DIM), F32)
    pad_s = jnp.zeros((S, HEAD_DIM - ROPE_DIM), F32)
    ck = jnp.tile(jnp.concatenate([cos, cos, pad_c], axis=1), (1, N_KV_GROUPS))
    sk = jnp.tile(jnp.concatenate([-sin, sin, pad_s], axis=1), (1, N_KV_GROUPS))
    return ck, sk, cos.T, sin.T


def _overlap_matrix_t(ncp, ns):
    nc = ncp - 1
    i = np.arange(ncp)[None, :]
    j = np.arange(ns)[:, None]
    m = (i * CMP_STRIDE < (j + 1) * SEL_BLOCK) & (i * CMP_STRIDE + CMP_BLOCK > j * SEL_BLOCK) & (i < nc)
    return jnp.asarray(m.astype(np.float32))


def _gate_perm():
    idx = np.zeros((N_KV_GROUPS, GATE_ROWS), np.int64)
    valid = np.zeros((N_KV_GROUPS, GATE_ROWS), np.float32)
    for g in range(N_KV_GROUPS):
        for br in range(N_NSA_BRANCHES):
            for z in range(HEADS_PER_GROUP):
                idx[g, br * HEADS_PER_GROUP + z] = O_GATE + (g * HEADS_PER_GROUP + z) * N_NSA_BRANCHES + br
                valid[g, br * HEADS_PER_GROUP + z] = 1.0
    return idx.reshape(-1), valid.reshape(-1)


def _compress_weights(pe, w1, w2):
    eye = jnp.eye(N_KV_GROUPS, dtype=F32)
    w1h = w1.reshape(2, CMP_STRIDE, HEAD_DIM, CMP_HIDDEN)
    w1big = jnp.einsum('ab,srdh->sradbh', eye, w1h).reshape(
        2, CMP_STRIDE * KV_DIM, N_KV_GROUPS * CMP_HIDDEN)
    peh = pe.reshape(2, CMP_STRIDE, 1, HEAD_DIM)
    pebig = jnp.broadcast_to(peh, (2, CMP_STRIDE, N_KV_GROUPS, HEAD_DIM)).reshape(2, 1, CMP_STRIDE * KV_DIM)
    pebig = jnp.broadcast_to(pebig, (2, 8, CMP_STRIDE * KV_DIM))
    w2big = jnp.einsum('ab,hd->ahbd', eye, w2).reshape(N_KV_GROUPS * CMP_HIDDEN, KV_DIM)
    return pebig.astype(BF16), w1big.astype(BF16), w2big.astype(BF16)


def _layer(x2, B, S, w_in, conv_w, w_conv_out, pe_k, wk1, wk2, pe_v, wv1, wv2, w_nsa_out, w_o,
           ln1_g, ln1_b, w_up, w_down, ln2_g, ln2_b, alpha):
    ncp = S // CMP_STRIDE
    ns = S // SEL_BLOCK
    gidx, gvalid = _gate_perm()
    wstd = jnp.concatenate([w_in[:, O_H:O_Q], w_in[:, O_KCMP:O_KSEL], w_in[:, O_KSEL:O_VSEL],
                            w_in[:, O_KWIN:O_VWIN]], axis=1).astype(BF16)
    wtr = jnp.concatenate([w_in[:, O_Q:O_KCMP], w_in[:, O_VSEL:O_KWIN], w_in[:, O_VWIN:O_GATE],
                           w_in[:, gidx] * gvalid[None, :]], axis=1).T.astype(BF16)
    wg = w_in[:, O_GCONV:O_END].astype(BF16)
    ck, sk, cq, sq = _rope_tables(S)

    mix, kcmp, vcmp, ksel, kwin, qT, qrT, vselT, vwinT, gT = _proj(
        x2, wstd, wtr, conv_w.reshape(CONV_WIDTH, CONV_DIM), ck, sk, cq, sq, B, S)

    pek, wk1b, wk2b = _compress_weights(pe_k, wk1, wk2)
    pev, wv1b, wv2b = _compress_weights(pe_v, wv1, wv2)
    kc, vcT = _compress(kcmp.reshape(B, ncp, CMP_STRIDE * KV_DIM), vcmp.reshape(B, ncp, CMP_STRIDE * KV_DIM),
                        pek, pev, wk1b, wv1b, wk2b, wv2b.T, B, ncp)

    tk = ATT_TK
    o = _attn(qT, qrT, kc, vcT, ksel.reshape(B, S // tk, tk, KV_DIM), vselT,
              kwin.reshape(B, S // tk, tk, KV_DIM), vwinT, gT, _overlap_matrix_t(ncp, ns), B, S)

    row = lambda v: v.reshape(1, D_MODEL).astype(F32)
    return _tail(x2, mix, o, wg, w_conv_out.astype(BF16), w_nsa_out.astype(BF16), w_o.astype(BF16),
                 row(ln1_g), row(ln1_b), w_up.astype(BF16), w_down.astype(BF16), row(ln2_g), row(ln2_b), alpha)


def kernel(x, w_in, conv_w, w_conv_out, pe_k_cmp, w_k_cmp1, w_k_cmp2, pe_v_cmp, w_v_cmp1, w_v_cmp2,
           w_nsa_out, w_o, ln1_g, ln1_b, w_up, w_down, ln2_g, ln2_b):
    B, S, D = x.shape
    depth = w_in.shape[0]
    assert D == D_MODEL and w_in.shape[2] == O_END
    assert S % PROJ_TM == 0 and S % ATT_TQ == 0 and (B * S) % TAIL_TM == 0 and S >= WINDOW
    alpha = float((2 * depth) ** 0.25)
    x2 = x.reshape(B * S, D)
    for l in range(depth):
        x2 = _layer(x2, B, S, w_in[l], conv_w[l], w_conv_out[l], pe_k_cmp[l], w_k_cmp1[l], w_k_cmp2[l],
                    pe_v_cmp[l], w_v_cmp1[l], w_v_cmp2[l], w_nsa_out[l], w_o[l], ln1_g[l], ln1_b[l],
                    w_up[l], w_down[l], ln2_g[l], ln2_b[l], alpha)
    return x2.reshape(B, S, D)
```

```python
import functools

import numpy as np
import jax
import jax.numpy as jnp
from jax import lax
from jax.experimental import pallas as pl
from jax.experimental.pallas import tpu as pltpu

F32 = jnp.float32
BF16 = jnp.bfloat16

D_MODEL = 1024
CONV_DIM = D_MODEL // 2
CONV_WIDTH = 3
N_HEADS = 8
HEAD_DIM = 64
N_KV_GROUPS = 2
HEADS_PER_GROUP = N_HEADS // N_KV_GROUPS
NSA_DIM = N_HEADS * HEAD_DIM
KV_DIM = N_KV_GROUPS * HEAD_DIM
ROPE_DIM = HEAD_DIM // 4
ROPE_HALF = ROPE_DIM // 2
ROPE_THETA = 500000.0
CMP_BLOCK = 32
CMP_STRIDE = 16
CMP_HIDDEN = 2 * HEAD_DIM
SEL_BLOCK = 64
N_SEL = 16
WINDOW = 512
N_NSA_BRANCHES = 3
D_FF = 4 * D_MODEL
LN_EPS = 1e-5
NEG = -1e30
FORCE = 1e9
SCALE = HEAD_DIM ** -0.5
LOG2E = 1.4426950408889634

SUBLANES = 8
LANES = 128
BF16_ROWS = 16
V_ROWS = HEAD_DIM + BF16_ROWS

_SPLITS = (CONV_DIM, CONV_DIM, CONV_DIM, NSA_DIM, KV_DIM, KV_DIM, KV_DIM, KV_DIM, KV_DIM, KV_DIM,
           N_HEADS * N_NSA_BRANCHES, D_MODEL, D_MODEL)
_OFFS = np.concatenate([[0], np.cumsum(_SPLITS)]).tolist()
(O_H, O_B, O_C, O_Q, O_KCMP, O_VCMP, O_KSEL, O_VSEL, O_KWIN, O_VWIN, O_GATE, O_GCONV, O_GNSA, O_END) = _OFFS

GATE_ROWS = 16
N_STD = 3 * CONV_DIM + 4 * KV_DIM
N_TR = NSA_DIM + 2 * KV_DIM + N_KV_GROUPS * GATE_ROWS

PROJ_TM = 512
ATT_TQ = 256
ATT_TK = 256
TAIL_TM = 256
FF_CHUNK = 1024
VMEM_LIMIT = 56 * 1024 * 1024


def _dot(a, b):
    return jnp.dot(a, b, preferred_element_type=F32)


def _dot_nt(a, b):
    return lax.dot_general(a, b, (((1,), (1,)), ((), ())), preferred_element_type=F32)


def _proj_kernel(x_ref, wstd_ref, wtr_ref, convw_ref, cosk_ref, sink_ref, cosq_ref, sinq_ref,
                 mix_ref, kcmp_ref, vcmp_ref, ksel_ref, kwin_ref,
                 qT_ref, qrT_ref, vselT_ref, vwinT_ref, gT_ref, ubuf):
    tm = x_ref.shape[0]
    s = pl.program_id(1)
    xb = x_ref[...].astype(BF16)

    hbc = _dot(xb, wstd_ref[:, 0:3 * CONV_DIM])
    u = hbc[:, 2 * CONV_DIM:3 * CONV_DIM] * hbc[:, 0:CONV_DIM]

    @pl.when(s == 0)
    def _():
        ubuf[0:SUBLANES, :] = jnp.zeros((SUBLANES, CONV_DIM), F32)

    ubuf[SUBLANES:SUBLANES + tm, :] = u
    u1 = ubuf[SUBLANES - 1:SUBLANES - 1 + tm, :]
    u2 = ubuf[SUBLANES - 2:SUBLANES - 2 + tm, :]
    cw = convw_ref[...]
    y = cw[0:1, :] * u2 + cw[1:2, :] * u1 + cw[2:3, :] * u
    mix_ref[...] = (hbc[:, CONV_DIM:2 * CONV_DIM] * y).astype(BF16)
    ubuf[0:SUBLANES, :] = ubuf[tm:tm + SUBLANES, :]

    kk = _dot(xb, wstd_ref[:, 3 * CONV_DIM:N_STD])
    kcmp_ref[...] = kk[:, 0:KV_DIM].astype(BF16)
    vcmp_ref[...] = kk[:, KV_DIM:2 * KV_DIM].astype(BF16)
    lane = lax.broadcasted_iota(jnp.int32, (tm, KV_DIM), 1)
    first_half = (lane % ROPE_DIM) < ROPE_HALF
    ck = cosk_ref[...]
    sk = sink_ref[...]
    for j, out in ((2, ksel_ref), (3, kwin_ref)):
        k = kk[:, j * KV_DIM:(j + 1) * KV_DIM]
        partner = jnp.where(first_half, pltpu.roll(k, KV_DIM - ROPE_HALF, 1), pltpu.roll(k, ROPE_HALF, 1))
        out[...] = (k * ck + partner * sk).astype(BF16)

    yT = _dot_nt(wtr_ref[...], xb)
    q = yT[0:NSA_DIM, :] * (SCALE * LOG2E)
    qT_ref[...] = q.astype(BF16)
    qrT_ref[...] = q.astype(BF16)
    cq = cosq_ref[...]
    sq = sinq_ref[...]
    for h in range(N_HEADS):
        r0 = h * HEAD_DIM
        x1 = q[r0:r0 + ROPE_HALF, :]
        x2 = q[r0 + ROPE_HALF:r0 + ROPE_DIM, :]
        rot = jnp.concatenate([x1 * cq - x2 * sq, x2 * cq + x1 * sq], axis=0)
        qrT_ref[r0:r0 + ROPE_DIM, :] = rot.astype(BF16)
    tk = vselT_ref.shape[-1]
    ones = jnp.ones((BF16_ROWS, tk), BF16)
    for j, out in enumerate((vselT_ref, vwinT_ref)):
        for i in range(tm // tk):
            for g in range(N_KV_GROUPS):
                r0 = NSA_DIM + j * KV_DIM + g * HEAD_DIM
                out[i, g * V_ROWS:g * V_ROWS + HEAD_DIM, :] = yT[r0:r0 + HEAD_DIM, i * tk:(i + 1) * tk].astype(BF16)
                out[i, g * V_ROWS + HEAD_DIM:(g + 1) * V_ROWS, :] = ones
    gT_ref[...] = jax.nn.sigmoid(yT[NSA_DIM + 2 * KV_DIM:N_TR, :])


def _proj(x2, wstd, wtr, convw, cosk, sink, cosq, sinq, B, S):
    tm, tk = PROJ_TM, ATT_TK
    ns = S // tm
    row = lambda b, s: (b * ns + s, 0)
    const = lambda b, s: (0, 0)
    tok = lambda c, dt: jax.ShapeDtypeStruct((B * S, c), dt)
    out_shape = (
        tok(CONV_DIM, BF16), tok(KV_DIM, BF16), tok(KV_DIM, BF16), tok(KV_DIM, BF16), tok(KV_DIM, BF16),
        jax.ShapeDtypeStruct((B, NSA_DIM, S), BF16), jax.ShapeDtypeStruct((B, NSA_DIM, S), BF16),
        jax.ShapeDtypeStruct((B, S // tk, N_KV_GROUPS * V_ROWS, tk), BF16),
        jax.ShapeDtypeStruct((B, S // tk, N_KV_GROUPS * V_ROWS, tk), BF16),
        jax.ShapeDtypeStruct((B, N_KV_GROUPS * GATE_ROWS, S), F32),
    )
    feat = lambda r: pl.BlockSpec((None, r, tm), lambda b, s: (b, 0, s))
    vt = pl.BlockSpec((None, tm // tk, N_KV_GROUPS * V_ROWS, tk), lambda b, s: (b, s, 0, 0))
    return pl.pallas_call(
        _proj_kernel,
        grid=(B, ns),
        in_specs=[
            pl.BlockSpec((tm, D_MODEL), row),
            pl.BlockSpec((D_MODEL, N_STD), const),
            pl.BlockSpec((N_TR, D_MODEL), const),
            pl.BlockSpec((CONV_WIDTH, CONV_DIM), const),
            pl.BlockSpec((tm, KV_DIM), lambda b, s: (s, 0)),
            pl.BlockSpec((tm, KV_DIM), lambda b, s: (s, 0)),
            pl.BlockSpec((ROPE_HALF, tm), lambda b, s: (0, s)),
            pl.BlockSpec((ROPE_HALF, tm), lambda b, s: (0, s)),
        ],
        out_specs=(
            pl.BlockSpec((tm, CONV_DIM), row), pl.BlockSpec((tm, KV_DIM), row), pl.BlockSpec((tm, KV_DIM), row),
            pl.BlockSpec((tm, KV_DIM), row), pl.BlockSpec((tm, KV_DIM), row),
            feat(NSA_DIM), feat(NSA_DIM), vt, vt, feat(N_KV_GROUPS * GATE_ROWS),
        ),
        out_shape=out_shape,
        scratch_shapes=[pltpu.VMEM((tm + SUBLANES, CONV_DIM), F32)],
        compiler_params=pltpu.CompilerParams(
            dimension_semantics=("arbitrary", "arbitrary"), vmem_limit_bytes=VMEM_LIMIT),
        name="proj",
    )(x2, wstd, wtr, convw, cosk, sink, cosq, sinq)


def _gelu_tanh(x):
    return x * (0.5 * (1.0 + jnp.tanh(np.sqrt(2.0 / np.pi).astype(np.float32) * (x + 0.044715 * (x * x * x)))))


def _compress_kernel(kch_ref, vch_ref, pek_ref, pev_ref, wk1_ref, wv1_ref, wk2_ref, wv2t_ref, kc_ref, vcT_ref):
    def hidden(ch_ref, pe_ref, w1_ref):
        ch = ch_ref[...]
        a = _dot(ch, w1_ref[0])
        b = _dot(ch, w1_ref[1])
        pe = _dot(pe_ref[0], w1_ref[0]) + _dot(pe_ref[1], w1_ref[1])
        b_next = pltpu.roll(b, b.shape[0] - 1, 0)
        return _gelu_tanh(a + b_next + pe[0:1, :]).astype(BF16)

    kc_ref[...] = _dot(hidden(kch_ref, pek_ref, wk1_ref), wk2_ref[...]).astype(BF16)
    vcT_ref[...] = _dot_nt(wv2t_ref[...], hidden(vch_ref, pev_ref, wv1_ref)).astype(BF16)


def _compress(kch, vch, pek, pev, wk1, wv1, wk2, wv2t, B, ncp):
    cw = CMP_STRIDE * KV_DIM
    gh = N_KV_GROUPS * CMP_HIDDEN
    c3 = lambda b: (0, 0, 0)
    c2 = lambda b: (0, 0)
    return pl.pallas_call(
        _compress_kernel,
        grid=(B,),
        in_specs=[
            pl.BlockSpec((None, ncp, cw), lambda b: (b, 0, 0)),
            pl.BlockSpec((None, ncp, cw), lambda b: (b, 0, 0)),
            pl.BlockSpec((2, 8, cw), c3), pl.BlockSpec((2, 8, cw), c3),
            pl.BlockSpec((2, cw, gh), c3), pl.BlockSpec((2, cw, gh), c3),
            pl.BlockSpec((gh, KV_DIM), c2), pl.BlockSpec((KV_DIM, gh), c2),
        ],
        out_specs=(pl.BlockSpec((None, ncp, KV_DIM), lambda b: (b, 0, 0)),
                   pl.BlockSpec((None, KV_DIM, ncp), lambda b: (b, 0, 0))),
        out_shape=(jax.ShapeDtypeStruct((B, ncp, KV_DIM), BF16), jax.ShapeDtypeStruct((B, KV_DIM, ncp), BF16)),
        compiler_params=pltpu.CompilerParams(dimension_semantics=("arbitrary",), vmem_limit_bytes=VMEM_LIMIT),
        name="compress",
    )(kch, vch, pek, pev, wk1, wv1, wk2, wv2t)


def _attn_kernel(qT_ref, qrT_ref, kc_ref, vcT_ref, ksel_ref, vselT_ref, kwin_ref, vwinT_ref, gT_ref, mT_ref,
                 wbias_ref, o_ref, selb_scr, q_scr, qr_scr, s_scr, swin_scr, m_scr, acc_scr, oT_scr):
    tq = qT_ref.shape[1]
    tk = ksel_ref.shape[1]
    nsub = tq // LANES
    nchunk = N_HEADS * nsub
    gchunk = HEADS_PER_GROUP * nsub
    gl = HEADS_PER_GROUP * tq
    ns = mT_ref.shape[0]
    bpt = tk // SEL_BLOCK
    p = pl.program_id(1)
    qbase = p * tq

    def chunk(c):
        return slice(c * LANES, (c + 1) * LANES)

    def sub(tsub):
        return slice(tsub * LANES, (tsub + 1) * LANES)

    def stage_q(ref, scr):
        zeros = jnp.zeros((HEAD_DIM, tq), BF16)
        for h in range(N_HEADS):
            qh = ref[h * HEAD_DIM:(h + 1) * HEAD_DIM, :]
            halves = [qh, zeros] if h // HEADS_PER_GROUP == 0 else [zeros, qh]
            scr[:, h * tq:(h + 1) * tq] = jnp.concatenate(halves, axis=0)

    gt = gT_ref[...]

    def gate(br, c):
        h, tsub = divmod(c, nsub)
        g, z = divmod(h, HEADS_PER_GROUP)
        r = g * GATE_ROWS + br * HEADS_PER_GROUP + z
        return gt[r:r + 1, sub(tsub)]

    stage_q(qT_ref, q_scr)
    stage_q(qrT_ref, qr_scr)
    start = p % 2
    n_back = WINDOW // tk
    case = jnp.minimum(p, n_back)
    lo = p - case

    def qk_sel(kt, slot):
        s_scr[slot] = _dot(ksel_ref[kt], qr_scr[...])

    kc = kc_ref[...]
    mT = mT_ref[...]
    ncp = kc.shape[0]
    s_cmp = _dot(kc, q_scr[...])
    k_win = kwin_ref[pl.ds(lo, n_back + 1)].reshape((n_back + 1) * tk, KV_DIM)
    swin_scr[...] = _dot(k_win, qr_scr[...])
    qk_sel(start, 0)

    imp = []
    for g in range(N_KV_GROUPS):
        vcT = vcT_ref[g * HEAD_DIM:(g + 1) * HEAD_DIM, :]
        imp_parts = []
        for tsub in range(nsub):
            n_idx = lax.broadcasted_iota(jnp.int32, (ncp, LANES), 0)
            t_idx = qbase + tsub * LANES + lax.broadcasted_iota(jnp.int32, (ncp, LANES), 1)
            ok = (n_idx * CMP_STRIDE + (CMP_BLOCK - 1)) <= t_idx
            bias = jnp.where(ok, 0.0, NEG).astype(F32)
            keep = ok.astype(F32)
            p_sum = None
            for z in range(HEADS_PER_GROUP):
                c = (g * HEADS_PER_GROUP + z) * nsub + tsub
                s = s_cmp[:, chunk(c)] + bias
                e = jnp.exp2(s - jnp.max(s, axis=0, keepdims=True)) * keep
                inv = 1.0 / jnp.maximum(jnp.sum(e, axis=0, keepdims=True), 1e-30)
                oT_scr[:, chunk(c)] = _dot(vcT, e.astype(BF16)) * (inv * gate(0, c))
                p_sum = e * inv if p_sum is None else p_sum + e * inv
            imp_parts.append(jnp.dot(mT, p_sum, preferred_element_type=F32, precision=lax.Precision.HIGHEST))
        imp.append(jnp.concatenate(imp_parts, axis=1))

    j_idx = lax.broadcasted_iota(jnp.int32, (ns, tq), 0)
    cur = (qbase + lax.broadcasted_iota(jnp.int32, (ns, tq), 1)) // SEL_BLOCK
    forced = (j_idx == 0) | (j_idx == cur) | (j_idx == cur - 1)
    for g in range(N_KV_GROUPS):
        impv = jnp.where(j_idx <= cur, jnp.where(forced, FORCE, imp[g]), NEG)
        cnt = jnp.zeros((ns, tq), F32)
        for jp in range(ns):
            row = impv[jp:jp + 1, :]
            beats = (row > impv) | ((row == impv) & (j_idx > jp))
            cnt = cnt + beats.astype(F32)
        sel = (cnt < float(min(N_SEL, ns))) & (impv > 0.5 * NEG)
        selb = jnp.where(sel, 0.0, NEG).astype(F32)
        for kt in range(ns // bpt):
            selb_scr[g, kt, 0:bpt, :] = selb[kt * bpt:(kt + 1) * bpt, :]

    SEL, WIN = 0, 1
    m_scr[...] = jnp.full(m_scr.shape, NEG, F32)
    acc_scr[...] = jnp.zeros(acc_scr.shape, F32)

    def softmax_pv(st, load_s, vT_t, block_bias, tile_bias):
        for g in range(N_KV_GROUPS):
            es, alphas = [], []
            for cg in range(gchunk):
                c = g * gchunk + cg
                tsub = c % nsub
                s = load_s(c)
                if block_bias is not None:
                    rb = block_bias[g][:, sub(tsub)]
                    s = s + jnp.concatenate(
                        [jnp.broadcast_to(rb[i:i + 1, :], (SEL_BLOCK, LANES)) for i in range(bpt)], axis=0)
                if tile_bias is not None:
                    s = s + tile_bias(tsub)
                m_old = m_scr[st, :, chunk(c)]
                m_new = jnp.maximum(m_old, jnp.max(s, axis=0, keepdims=True))
                es.append(jnp.exp2(s - m_new).astype(BF16))
                alphas.append(jnp.exp2(m_old - m_new))
                m_scr[st, :, chunk(c)] = m_new
            lanes = slice(g * gl, (g + 1) * gl)
            pv = _dot(vT_t[g * V_ROWS:(g + 1) * V_ROWS, :], jnp.concatenate(es, axis=1))
            acc_scr[st, :, lanes] = jnp.concatenate(alphas, axis=1) * acc_scr[st, :, lanes] + pv

    def finish(st, br):
        for c in range(nchunk):
            acc = acc_scr[st, :, chunk(c)]
            inv = 1.0 / jnp.maximum(acc[HEAD_DIM:HEAD_DIM + 1, :], 1e-30)
            oT_scr[:, chunk(c)] += acc[0:HEAD_DIM, :] * (inv * gate(br, c))

    for i in range(n_back, -1, -1):
        rows = slice(i * tk, (i + 1) * tk)
        softmax_pv(WIN, lambda c: swin_scr[rows, chunk(c)], vwinT_ref[lo + i], None,
                   lambda tsub: wbias_ref[case, rows, sub(tsub)])
    finish(WIN, 2)

    def sp_sel(kt, slot, tile_bias):
        softmax_pv(SEL, lambda c: s_scr[slot, :, chunk(c)], vselT_ref[kt],
                   [selb_scr[g, kt, 0:bpt, :] for g in range(N_KV_GROUPS)], tile_bias)

    @pl.when(start == 1)
    def _():
        qk_sel(0, 1)
        sp_sel(0, 1, None)

    def pair_body(j, carry):
        a = start + 2 * j
        qk_sel(a + 1, 1)
        sp_sel(a, 0, None)
        qk_sel(a + 2, 0)
        sp_sel(a + 1, 1, None)
        return carry

    lax.fori_loop(0, p // 2, pair_body, 0)
    sp_sel(p, 0, lambda tsub: wbias_ref[n_back, n_back * tk:(n_back + 1) * tk, sub(tsub)])
    finish(SEL, 1)

    for tsub in range(nsub):
        for i in range(N_HEADS // 2):
            pair = jnp.concatenate([oT_scr[:, chunk((2 * i) * nsub + tsub)],
                                    oT_scr[:, chunk((2 * i + 1) * nsub + tsub)]], axis=0)
            o_ref[sub(tsub), i * 2 * HEAD_DIM:(i + 1) * 2 * HEAD_DIM] = pair.T.astype(BF16)


def _window_bias(t, n_back):
    r = np.arange(t)[:, None]
    c = np.arange(t)[None, :]
    causal = np.where(r <= c, 0.0, NEG)
    edge = np.where(r > c, 0.0, NEG)
    out = np.zeros((n_back + 1, (n_back + 1) * t, t), np.float32)
    for case in range(n_back + 1):
        for i in range(n_back + 1):
            back = case - i
            tile = NEG if back < 0 else causal if back == 0 else edge if back == n_back else 0.0
            out[case, i * t:(i + 1) * t, :] = tile
    return jnp.asarray(out)


def _attn(qT, qrT, kc, vcT, ksel, vselT, kwin, vwinT, gT, mT, B, S):
    tq, tk = ATT_TQ, ATT_TK
    assert tq == tk and WINDOW % tk == 0 and tk % SEL_BLOCK == 0 and tq % LANES == 0
    assert tk // SEL_BLOCK <= SUBLANES
    n_back = WINDOW // tk
    assert S // tk > n_back
    P = S // tq
    ncp = kc.shape[1]
    ns = S // SEL_BLOCK
    nq = N_HEADS * tq
    qspec = pl.BlockSpec((None, NSA_DIM, tq), lambda b, p: (b, 0, p))
    kspec = pl.BlockSpec((None, S // tk, tk, KV_DIM), lambda b, p: (b, 0, 0, 0))
    vspec = pl.BlockSpec((None, S // tk, N_KV_GROUPS * V_ROWS, tk), lambda b, p: (b, 0, 0, 0))
    return pl.pallas_call(
        _attn_kernel,
        grid=(B, P),
        in_specs=[
            qspec, qspec,
            pl.BlockSpec((None, ncp, KV_DIM), lambda b, p: (b, 0, 0)),
            pl.BlockSpec((None, KV_DIM, ncp), lambda b, p: (b, 0, 0)),
            kspec, vspec, kspec, vspec,
            pl.BlockSpec((None, N_KV_GROUPS * GATE_ROWS, tq), lambda b, p: (b, 0, p)),
            pl.BlockSpec((ns, ncp), lambda b, p: (0, 0)),
            pl.BlockSpec((n_back + 1, (n_back + 1) * tk, tq), lambda b, p: (0, 0, 0)),
        ],
        out_specs=pl.BlockSpec((tq, NSA_DIM), lambda b, p: (b * P + p, 0)),
        out_shape=jax.ShapeDtypeStruct((B * S, NSA_DIM), BF16),
        scratch_shapes=[
            pltpu.VMEM((N_KV_GROUPS, S // tk, SUBLANES, tq), F32),
            pltpu.VMEM((KV_DIM, nq), BF16),
            pltpu.VMEM((KV_DIM, nq), BF16),
            pltpu.VMEM((2, tk, nq), F32),
            pltpu.VMEM(((n_back + 1) * tk, nq), F32),
            pltpu.VMEM((2, 1, nq), F32),
            pltpu.VMEM((2, V_ROWS, nq), F32),
            pltpu.VMEM((HEAD_DIM, nq), F32),
        ],
        compiler_params=pltpu.CompilerParams(
            dimension_semantics=("arbitrary", "arbitrary"), vmem_limit_bytes=VMEM_LIMIT),
        name="attn",
    )(qT, qrT, kc, vcT, ksel, vselT, kwin, vwinT, gT, mT, _window_bias(tk, n_back))


def _layer_norm(z, g, b):
    mu = jnp.mean(z, axis=-1, keepdims=True)
    zc = z - mu
    var = jnp.mean(zc * zc, axis=-1, keepdims=True)
    return zc * lax.rsqrt(var + LN_EPS) * g + b


def _tail_kernel(alpha, x_ref, mix_ref, o_ref, wg_ref, wco_ref, wno_ref, wo_ref, g1_ref, b1_ref,
                 wup_ref, wdn_ref, g2_ref, b2_ref, out_ref):
    x = x_ref[...]
    xb = x.astype(BF16)
    y_conv = _dot(mix_ref[...], wco_ref[...])
    y_nsa = _dot(o_ref[...], wno_ref[...])
    mixed = (jax.nn.sigmoid(_dot(xb, wg_ref[:, 0:D_MODEL])) * y_conv
             + jax.nn.sigmoid(_dot(xb, wg_ref[:, D_MODEL:2 * D_MODEL])) * y_nsa)
    x1 = _layer_norm(alpha * x + _dot(mixed.astype(BF16), wo_ref[...]), g1_ref[...], b1_ref[...])
    x1b = x1.astype(BF16)
    ff = jnp.zeros_like(x1)
    for c in range(D_FF // FF_CHUNK):
        h = jnp.maximum(_dot(x1b, wup_ref[:, c * FF_CHUNK:(c + 1) * FF_CHUNK]), 0.0)
        ff = ff + _dot((h * h).astype(BF16), wdn_ref[c * FF_CHUNK:(c + 1) * FF_CHUNK, :])
    out_ref[...] = _layer_norm(alpha * x1 + ff, g2_ref[...], b2_ref[...])


def _tail(x2, mix, o, wg, wco, wno, wo, g1, b1, wup, wdn, g2, b2, alpha):
    n = x2.shape[0]
    tm = TAIL_TM
    row = lambda i: (i, 0)
    const = lambda i: (0, 0)
    resident = lambda shape: pl.BlockSpec(shape, const, pipeline_mode=pl.Buffered(1))
    return pl.pallas_call(
        functools.partial(_tail_kernel, alpha),
        grid=(n // tm,),
        in_specs=[
            pl.BlockSpec((tm, D_MODEL), row), pl.BlockSpec((tm, CONV_DIM), row), pl.BlockSpec((tm, NSA_DIM), row),
            resident((D_MODEL, 2 * D_MODEL)), resident((CONV_DIM, D_MODEL)), resident((NSA_DIM, D_MODEL)),
            resident((D_MODEL, D_MODEL)), resident((1, D_MODEL)), resident((1, D_MODEL)),
            resident((D_MODEL, D_FF)), resident((D_FF, D_MODEL)), resident((1, D_MODEL)), resident((1, D_MODEL)),
        ],
        out_specs=pl.BlockSpec((tm, D_MODEL), row),
        out_shape=jax.ShapeDtypeStruct((n, D_MODEL), F32),
        compiler_params=pltpu.CompilerParams(dimension_semantics=("arbitrary",), vmem_limit_bytes=VMEM_LIMIT),
        name="tail",
    )(x2, mix, o, wg, wco, wno, wo, g1, b1, wup, wdn, g2, b2)


def _rope_tables(S):
    inv = ROPE_THETA ** (-jnp.arange(0, ROPE_DIM, 2, dtype=F32) / ROPE_DIM)
    ang = jnp.arange(S, dtype=F32)[:, None] * inv[None, :]
    cos, sin = jnp.cos(ang), jnp.sin(ang)
    pad_c = jnp.ones((S, HEAD_DIM - ROPE_DIM), F32)
    pad_s = jnp.zeros((S, HEAD_DIM - ROPE_DIM), F32)
    ck = jnp.tile(jnp.concatenate([cos, cos, pad_c], axis=1), (1, N_KV_GROUPS))
    sk = jnp.tile(jnp.concatenate([-sin, sin, pad_s], axis=1), (1, N_KV_GROUPS))
    return ck, sk, cos.T, sin.T


def _overlap_matrix_t(ncp, ns):
    nc = ncp - 1
    i = np.arange(ncp)[None, :]
    j = np.arange(ns)[:, None]
    m = (i * CMP_STRIDE < (j + 1) * SEL_BLOCK) & (i * CMP_STRIDE + CMP_BLOCK > j * SEL_BLOCK) & (i < nc)
    return jnp.asarray(m.astype(np.float32))


def _gate_perm():
    idx = np.zeros((N_KV_GROUPS, GATE_ROWS), np.int64)
    valid = np.zeros((N_KV_GROUPS, GATE_ROWS), np.float32)
    for g in range(N_KV_GROUPS):
        for br in range(N_NSA_BRANCHES):
            for z in range(HEADS_PER_GROUP):
                idx[g, br * HEADS_PER_GROUP + z] = O_GATE + (g * HEADS_PER_GROUP + z) * N_NSA_BRANCHES + br
                valid[g, br * HEADS_PER_GROUP + z] = 1.0
    return idx.reshape(-1), valid.reshape(-1)


def _compress_weights(pe, w1, w2):
    eye = jnp.eye(N_KV_GROUPS, dtype=F32)
    w1h = w1.reshape(2, CMP_STRIDE, HEAD_DIM, CMP_HIDDEN)
    w1big = jnp.einsum('ab,srdh->sradbh', eye, w1h).reshape(
        2, CMP_STRIDE * KV_DIM, N_KV_GROUPS * CMP_HIDDEN)
    peh = pe.reshape(2, CMP_STRIDE, 1, HEAD_DIM)
    pebig = jnp.broadcast_to(peh, (2, CMP_STRIDE, N_KV_GROUPS, HEAD_DIM)).reshape(2, 1, CMP_STRIDE * KV_DIM)
    pebig = jnp.broadcast_to(pebig, (2, 8, CMP_STRIDE * KV_DIM))
    w2big = jnp.einsum('ab,hd->ahbd', eye, w2).reshape(N_KV_GROUPS * CMP_HIDDEN, KV_DIM)
    return pebig.astype(BF16), w1big.astype(BF16), w2big.astype(BF16)


def _layer(x2, B, S, w_in, conv_w, w_conv_out, pe_k, wk1, wk2, pe_v, wv1, wv2, w_nsa_out, w_o,
           ln1_g, ln1_b, w_up, w_down, ln2_g, ln2_b, alpha):
    ncp = S // CMP_STRIDE
    ns = S // SEL_BLOCK
    gidx, gvalid = _gate_perm()
    wstd = jnp.concatenate([w_in[:, O_H:O_Q], w_in[:, O_KCMP:O_KSEL], w_in[:, O_KSEL:O_VSEL],
                            w_in[:, O_KWIN:O_VWIN]], axis=1).astype(BF16)
    wtr = jnp.concatenate([w_in[:, O_Q:O_KCMP], w_in[:, O_VSEL:O_KWIN], w_in[:, O_VWIN:O_GATE],
                           w_in[:, gidx] * gvalid[None, :]], axis=1).T.astype(BF16)
    wg = w_in[:, O_GCONV:O_END].astype(BF16)
    ck, sk, cq, sq = _rope_tables(S)

    mix, kcmp, vcmp, ksel, kwin, qT, qrT, vselT, vwinT, gT = _proj(
        x2, wstd, wtr, conv_w.reshape(CONV_WIDTH, CONV_DIM), ck, sk, cq, sq, B, S)

    pek, wk1b, wk2b = _compress_weights(pe_k, wk1, wk2)
    pev, wv1b, wv2b = _compress_weights(pe_v, wv1, wv2)
    kc, vcT = _compress(kcmp.reshape(B, ncp, CMP_STRIDE * KV_DIM), vcmp.reshape(B, ncp, CMP_STRIDE * KV_DIM),
                        pek, pev, wk1b, wv1b, wk2b, wv2b.T, B, ncp)

    tk = ATT_TK
    o = _attn(qT, qrT, kc, vcT, ksel.reshape(B, S // tk, tk, KV_DIM), vselT,
              kwin.reshape(B, S // tk, tk, KV_DIM), vwinT, gT, _overlap_matrix_t(ncp, ns), B, S)

    row = lambda v: v.reshape(1, D_MODEL).astype(F32)
    return _tail(x2, mix, o, wg, w_conv_out.astype(BF16), w_nsa_out.astype(BF16), w_o.astype(BF16),
                 row(ln1_g), row(ln1_b), w_up.astype(BF16), w_down.astype(BF16), row(ln2_g), row(ln2_b), alpha)


def kernel(x, w_in, conv_w, w_conv_out, pe_k_cmp, w_k_cmp1, w_k_cmp2, pe_v_cmp, w_v_cmp1, w_v_cmp2,
           w_nsa_out, w_o, ln1_g, ln1_b, w_up, w_down, ln2_g, ln2_b):
    B, S, D = x.shape
    depth = w_in.shape[0]
    assert D == D_MODEL and w_in.shape[2] == O_END
    assert S % PROJ_TM == 0 and S % ATT_TQ == 0 and (B * S) % TAIL_TM == 0 and S >= WINDOW
    alpha = float((2 * depth) ** 0.25)
    x2 = x.reshape(B * S, D)
    for l in range(depth):
        x2 = _layer(x2, B, S, w_in[l], conv_w[l], w_conv_out[l], pe_k_cmp[l], w_k_cmp1[l], w_k_cmp2[l],
                    pe_v_cmp[l], w_v_cmp1[l], w_v_cmp2[l], w_nsa_out[l], w_o[l], ln1_g[l], ln1_b[l],
                    w_up[l], w_down[l], ln2_g[l], ln2_b[l], alpha)
    return x2.reshape(B, S, D)
```

```python
import functools

import numpy as np
import jax
import jax.numpy as jnp
from jax import lax
from jax.experimental import pallas as pl
from jax.experimental.pallas import tpu as pltpu

F32 = jnp.float32
BF16 = jnp.bfloat16

D_MODEL = 1024
CONV_DIM = D_MODEL // 2
CONV_WIDTH = 3
N_HEADS = 8
HEAD_DIM = 64
N_KV_GROUPS = 2
HEADS_PER_GROUP = N_HEADS // N_KV_GROUPS
NSA_DIM = N_HEADS * HEAD_DIM
KV_DIM = N_KV_GROUPS * HEAD_DIM
ROPE_DIM = HEAD_DIM // 4
ROPE_HALF = ROPE_DIM // 2
ROPE_THETA = 500000.0
CMP_BLOCK = 32
CMP_STRIDE = 16
CMP_HIDDEN = 2 * HEAD_DIM
SEL_BLOCK = 64
N_SEL = 16
WINDOW = 512
N_NSA_BRANCHES = 3
D_FF = 4 * D_MODEL
LN_EPS = 1e-5
NEG = -1e30
FORCE = 1e9
SCALE = HEAD_DIM ** -0.5
LOG2E = 1.4426950408889634

SUBLANES = 8
LANES = 128
BF16_ROWS = 16
V_ROWS = HEAD_DIM + BF16_ROWS

_SPLITS = (CONV_DIM, CONV_DIM, CONV_DIM, NSA_DIM, KV_DIM, KV_DIM, KV_DIM, KV_DIM, KV_DIM, KV_DIM,
           N_HEADS * N_NSA_BRANCHES, D_MODEL, D_MODEL)
_OFFS = np.concatenate([[0], np.cumsum(_SPLITS)]).tolist()
(O_H, O_B, O_C, O_Q, O_KCMP, O_VCMP, O_KSEL, O_VSEL, O_KWIN, O_VWIN, O_GATE, O_GCONV, O_GNSA, O_END) = _OFFS

GATE_ROWS = 16
N_STD = 3 * CONV_DIM + 4 * KV_DIM
N_TR = NSA_DIM + 2 * KV_DIM + N_KV_GROUPS * GATE_ROWS

PROJ_TM = 512
ATT_TQ = 256
ATT_TK = 256
TAIL_TM = 512
FF_CHUNK = 1024
VMEM_LIMIT = 56 * 1024 * 1024


def _dot(a, b):
    return jnp.dot(a, b, preferred_element_type=F32)


def _dot_nt(a, b):
    return lax.dot_general(a, b, (((1,), (1,)), ((), ())), preferred_element_type=F32)


def _proj_kernel(x_ref, wstd_ref, wtr_ref, convw_ref, cosk_ref, sink_ref, cosq_ref, sinq_ref,
                 mix_ref, kch_ref, vch_ref, ksel_ref, kwin_ref,
                 qT_ref, qrT_ref, vselT_ref, vwinT_ref, gT_ref, ubuf, kvbuf):
    tm = x_ref.shape[0]
    s = pl.program_id(1)
    xb = x_ref[...].astype(BF16)

    hbc = _dot(xb, wstd_ref[:, 0:3 * CONV_DIM])
    u = hbc[:, 2 * CONV_DIM:3 * CONV_DIM] * hbc[:, 0:CONV_DIM]

    @pl.when(s == 0)
    def _():
        ubuf[0:SUBLANES, :] = jnp.zeros((SUBLANES, CONV_DIM), F32)

    ubuf[SUBLANES:SUBLANES + tm, :] = u
    u1 = ubuf[SUBLANES - 1:SUBLANES - 1 + tm, :]
    u2 = ubuf[SUBLANES - 2:SUBLANES - 2 + tm, :]
    cw = convw_ref[...]
    y = cw[0:1, :] * u2 + cw[1:2, :] * u1 + cw[2:3, :] * u
    mix_ref[...] = (hbc[:, CONV_DIM:2 * CONV_DIM] * y).astype(BF16)
    ubuf[0:SUBLANES, :] = ubuf[tm:tm + SUBLANES, :]

    kk = _dot(xb, wstd_ref[:, 3 * CONV_DIM:N_STD])
    for j, out in enumerate((kch_ref, vch_ref)):
        kvbuf[j] = kk[:, j * KV_DIM:(j + 1) * KV_DIM]
        for r in range(CMP_STRIDE):
            rows = kvbuf[j, pl.ds(r, tm // CMP_STRIDE, stride=CMP_STRIDE), :]
            out[:, r * KV_DIM:(r + 1) * KV_DIM] = rows.astype(BF16)
    lane = lax.broadcasted_iota(jnp.int32, (tm, KV_DIM), 1)
    first_half = (lane % ROPE_DIM) < ROPE_HALF
    ck = cosk_ref[...]
    sk = sink_ref[...]
    for j, out in ((2, ksel_ref), (3, kwin_ref)):
        k = kk[:, j * KV_DIM:(j + 1) * KV_DIM]
        partner = jnp.where(first_half, pltpu.roll(k, KV_DIM - ROPE_HALF, 1), pltpu.roll(k, ROPE_HALF, 1))
        out[...] = (k * ck + partner * sk).astype(BF16)

    yT = _dot_nt(wtr_ref[...], xb)
    q = yT[0:NSA_DIM, :] * (SCALE * LOG2E)
    qT_ref[...] = q.astype(BF16)
    qrT_ref[...] = q.astype(BF16)
    cq = cosq_ref[...]
    sq = sinq_ref[...]
    for h in range(N_HEADS):
        r0 = h * HEAD_DIM
        x1 = q[r0:r0 + ROPE_HALF, :]
        x2 = q[r0 + ROPE_HALF:r0 + ROPE_DIM, :]
        rot = jnp.concatenate([x1 * cq - x2 * sq, x2 * cq + x1 * sq], axis=0)
        qrT_ref[r0:r0 + ROPE_DIM, :] = rot.astype(BF16)
    tk = vselT_ref.shape[-1]
    ones = jnp.ones((BF16_ROWS, tk), BF16)
    for j, out in enumerate((vselT_ref, vwinT_ref)):
        for i in range(tm // tk):
            for g in range(N_KV_GROUPS):
                r0 = NSA_DIM + j * KV_DIM + g * HEAD_DIM
                out[i, g * V_ROWS:g * V_ROWS + HEAD_DIM, :] = yT[r0:r0 + HEAD_DIM, i * tk:(i + 1) * tk].astype(BF16)
                out[i, g * V_ROWS + HEAD_DIM:(g + 1) * V_ROWS, :] = ones
    gT_ref[...] = jax.nn.sigmoid(yT[NSA_DIM + 2 * KV_DIM:N_TR, :])


def _proj(x2, wstd, wtr, convw, cosk, sink, cosq, sinq, B, S):
    tm, tk = PROJ_TM, ATT_TK
    ns = S // tm
    row = lambda b, s: (b * ns + s, 0)
    const = lambda b, s: (0, 0)
    tok = lambda c, dt: jax.ShapeDtypeStruct((B * S, c), dt)
    cw = CMP_STRIDE * KV_DIM
    chunked = jax.ShapeDtypeStruct((B, S // CMP_STRIDE, cw), BF16)
    chunk_spec = pl.BlockSpec((None, tm // CMP_STRIDE, cw), lambda b, s: (b, s, 0))
    out_shape = (
        tok(CONV_DIM, BF16), chunked, chunked, tok(KV_DIM, BF16), tok(KV_DIM, BF16),
        jax.ShapeDtypeStruct((B, NSA_DIM, S), BF16), jax.ShapeDtypeStruct((B, NSA_DIM, S), BF16),
        jax.ShapeDtypeStruct((B, S // tk, N_KV_GROUPS * V_ROWS, tk), BF16),
        jax.ShapeDtypeStruct((B, S // tk, N_KV_GROUPS * V_ROWS, tk), BF16),
        jax.ShapeDtypeStruct((B, N_KV_GROUPS * GATE_ROWS, S), F32),
    )
    feat = lambda r: pl.BlockSpec((None, r, tm), lambda b, s: (b, 0, s))
    vt = pl.BlockSpec((None, tm // tk, N_KV_GROUPS * V_ROWS, tk), lambda b, s: (b, s, 0, 0))
    return pl.pallas_call(
        _proj_kernel,
        grid=(B, ns),
        in_specs=[
            pl.BlockSpec((tm, D_MODEL), row),
            pl.BlockSpec((D_MODEL, N_STD), const),
            pl.BlockSpec((N_TR, D_MODEL), const),
            pl.BlockSpec((CONV_WIDTH, CONV_DIM), const),
            pl.BlockSpec((tm, KV_DIM), lambda b, s: (s, 0)),
            pl.BlockSpec((tm, KV_DIM), lambda b, s: (s, 0)),
            pl.BlockSpec((ROPE_HALF, tm), lambda b, s: (0, s)),
            pl.BlockSpec((ROPE_HALF, tm), lambda b, s: (0, s)),
        ],
        out_specs=(
            pl.BlockSpec((tm, CONV_DIM), row), chunk_spec, chunk_spec,
            pl.BlockSpec((tm, KV_DIM), row), pl.BlockSpec((tm, KV_DIM), row),
            feat(NSA_DIM), feat(NSA_DIM), vt, vt, feat(N_KV_GROUPS * GATE_ROWS),
        ),
        out_shape=out_shape,
        scratch_shapes=[pltpu.VMEM((tm + SUBLANES, CONV_DIM), F32), pltpu.VMEM((2, tm, KV_DIM), F32)],
        compiler_params=pltpu.CompilerParams(
            dimension_semantics=("arbitrary", "arbitrary"), vmem_limit_bytes=VMEM_LIMIT),
        name="proj",
    )(x2, wstd, wtr, convw, cosk, sink, cosq, sinq)


def _gelu_tanh(x):
    return x * (0.5 * (1.0 + jnp.tanh(np.sqrt(2.0 / np.pi).astype(np.float32) * (x + 0.044715 * (x * x * x)))))


def _compress_kernel(kch_ref, vch_ref, pek_ref, pev_ref, wk1_ref, wv1_ref, wk2_ref, wv2t_ref, kc_ref, vcT_ref,
                     wk1_big, wv1_big):
    @pl.when(pl.program_id(0) == 0)
    def _():
        for w_ref, big in ((wk1_ref, wk1_big), (wv1_ref, wv1_big)):
            big[...] = jnp.zeros(big.shape, BF16)
            for half in range(2):
                for r in range(CMP_STRIDE):
                    for g in range(N_KV_GROUPS):
                        r0 = r * KV_DIM + g * HEAD_DIM
                        big[half, r0:r0 + HEAD_DIM, g * CMP_HIDDEN:(g + 1) * CMP_HIDDEN] = (
                            w_ref[half * CMP_STRIDE + r])

    def hidden(ch_ref, pe_ref, big):
        ch = ch_ref[...]
        a = _dot(ch, big[0])
        b = _dot(ch, big[1])
        pe = _dot(pe_ref[0], big[0]) + _dot(pe_ref[1], big[1])
        b_next = pltpu.roll(b, b.shape[0] - 1, 0)
        return _gelu_tanh(a + b_next + pe[0:1, :]).astype(BF16)

    kc_ref[...] = _dot(hidden(kch_ref, pek_ref, wk1_big), wk2_ref[...]).astype(BF16)
    vcT_ref[...] = _dot_nt(wv2t_ref[...], hidden(vch_ref, pev_ref, wv1_big)).astype(BF16)


def _compress(kch, vch, pek, pev, wk1, wv1, wk2, wv2t, B, ncp):
    cw = CMP_STRIDE * KV_DIM
    gh = N_KV_GROUPS * CMP_HIDDEN
    c3 = lambda b: (0, 0, 0)
    c2 = lambda b: (0, 0)
    return pl.pallas_call(
        _compress_kernel,
        grid=(B,),
        in_specs=[
            pl.BlockSpec((None, ncp, cw), lambda b: (b, 0, 0)),
            pl.BlockSpec((None, ncp, cw), lambda b: (b, 0, 0)),
            pl.BlockSpec((2, SUBLANES, cw), c3), pl.BlockSpec((2, SUBLANES, cw), c3),
            pl.BlockSpec((CMP_BLOCK, HEAD_DIM, CMP_HIDDEN), c3), pl.BlockSpec((CMP_BLOCK, HEAD_DIM, CMP_HIDDEN), c3),
            pl.BlockSpec((gh, KV_DIM), c2), pl.BlockSpec((KV_DIM, gh), c2),
        ],
        out_specs=(pl.BlockSpec((None, ncp, KV_DIM), lambda b: (b, 0, 0)),
                   pl.BlockSpec((None, KV_DIM, ncp), lambda b: (b, 0, 0))),
        out_shape=(jax.ShapeDtypeStruct((B, ncp, KV_DIM), BF16), jax.ShapeDtypeStruct((B, KV_DIM, ncp), BF16)),
        scratch_shapes=[pltpu.VMEM((2, cw, gh), BF16), pltpu.VMEM((2, cw, gh), BF16)],
        compiler_params=pltpu.CompilerParams(dimension_semantics=("arbitrary",), vmem_limit_bytes=VMEM_LIMIT),
        name="compress",
    )(kch, vch, pek, pev, wk1, wv1, wk2, wv2t)


def _attn_kernel(qT_ref, qrT_ref, kc_ref, vcT_ref, ksel_ref, vselT_ref, kwin_ref, vwinT_ref, gT_ref, mT_ref,
                 wbias_ref, o_ref, selb_scr, q_scr, qr_scr, s_scr, swin_scr, m_scr, acc_scr, oT_scr):
    tq = qT_ref.shape[1]
    tk = ksel_ref.shape[1]
    nsub = tq // LANES
    nchunk = N_HEADS * nsub
    gchunk = HEADS_PER_GROUP * nsub
    gl = HEADS_PER_GROUP * tq
    ns = mT_ref.shape[0]
    bpt = tk // SEL_BLOCK
    p = pl.program_id(1)
    qbase = p * tq

    def chunk(c):
        return slice(c * LANES, (c + 1) * LANES)

    def sub(tsub):
        return slice(tsub * LANES, (tsub + 1) * LANES)

    def stage_q(ref, scr):
        zeros = jnp.zeros((HEAD_DIM, tq), BF16)
        for h in range(N_HEADS):
            qh = ref[h * HEAD_DIM:(h + 1) * HEAD_DIM, :]
            halves = [qh, zeros] if h // HEADS_PER_GROUP == 0 else [zeros, qh]
            scr[:, h * tq:(h + 1) * tq] = jnp.concatenate(halves, axis=0)

    gt = gT_ref[...]

    def gate(br, c):
        h, tsub = divmod(c, nsub)
        g, z = divmod(h, HEADS_PER_GROUP)
        r = g * GATE_ROWS + br * HEADS_PER_GROUP + z
        return gt[r:r + 1, sub(tsub)]

    stage_q(qT_ref, q_scr)
    stage_q(qrT_ref, qr_scr)
    start = p % 2
    n_back = WINDOW // tk
    case = jnp.minimum(p, n_back)
    lo = p - case

    def qk_sel(kt, slot):
        s_scr[slot] = _dot(ksel_ref[kt], qr_scr[...])

    kc = kc_ref[...]
    mT = mT_ref[...]
    ncp = kc.shape[0]
    s_cmp = _dot(kc, q_scr[...])
    k_win = kwin_ref[pl.ds(lo, n_back + 1)].reshape((n_back + 1) * tk, KV_DIM)
    swin_scr[...] = _dot(k_win, qr_scr[...])
    qk_sel(start, 0)

    imp = []
    for g in range(N_KV_GROUPS):
        vcT = vcT_ref[g * HEAD_DIM:(g + 1) * HEAD_DIM, :]
        imp_parts = []
        for tsub in range(nsub):
            n_idx = lax.broadcasted_iota(jnp.int32, (ncp, LANES), 0)
            t_idx = qbase + tsub * LANES + lax.broadcasted_iota(jnp.int32, (ncp, LANES), 1)
            ok = (n_idx * CMP_STRIDE + (CMP_BLOCK - 1)) <= t_idx
            bias = jnp.where(ok, 0.0, NEG).astype(F32)
            keep = ok.astype(F32)
            p_sum = None
            for z in range(HEADS_PER_GROUP):
                c = (g * HEADS_PER_GROUP + z) * nsub + tsub
                s = s_cmp[:, chunk(c)] + bias
                e = jnp.exp2(s - jnp.max(s, axis=0, keepdims=True)) * keep
                inv = 1.0 / jnp.maximum(jnp.sum(e, axis=0, keepdims=True), 1e-30)
                oT_scr[:, chunk(c)] = _dot(vcT, e.astype(BF16)) * (inv * gate(0, c))
                p_sum = e * inv if p_sum is None else p_sum + e * inv
            imp_parts.append(jnp.dot(mT, p_sum, preferred_element_type=F32, precision=lax.Precision.HIGHEST))
        imp.append(jnp.concatenate(imp_parts, axis=1))

    j_idx = lax.broadcasted_iota(jnp.int32, (ns, tq), 0)
    cur = (qbase + lax.broadcasted_iota(jnp.int32, (ns, tq), 1)) // SEL_BLOCK
    forced = (j_idx == 0) | (j_idx == cur) | (j_idx == cur - 1)
    for g in range(N_KV_GROUPS):
        impv = jnp.where(j_idx <= cur, jnp.where(forced, FORCE, imp[g]), NEG)
        cnt = jnp.zeros((ns, tq), F32)
        for jp in range(ns):
            row = impv[jp:jp + 1, :]
            beats = (row > impv) | ((row == impv) & (j_idx > jp))
            cnt = cnt + beats.astype(F32)
        sel = (cnt < float(min(N_SEL, ns))) & (impv > 0.5 * NEG)
        selb = jnp.where(sel, 0.0, NEG).astype(F32)
        for kt in range(ns // bpt):
            selb_scr[g, kt, 0:bpt, :] = selb[kt * bpt:(kt + 1) * bpt, :]

    SEL, WIN = 0, 1
    m_scr[...] = jnp.full(m_scr.shape, NEG, F32)
    acc_scr[...] = jnp.zeros(acc_scr.shape, F32)

    def softmax_pv(st, load_s, vT_t, block_bias, tile_bias):
        for g in range(N_KV_GROUPS):
            es, alphas = [], []
            for cg in range(gchunk):
                c = g * gchunk + cg
                tsub = c % nsub
                s = load_s(c)
                if block_bias is not None:
                    rb = block_bias[g][:, sub(tsub)]
                    s = s + jnp.concatenate(
                        [jnp.broadcast_to(rb[i:i + 1, :], (SEL_BLOCK, LANES)) for i in range(bpt)], axis=0)
                if tile_bias is not None:
                    s = s + tile_bias(tsub)
                m_old = m_scr[st, :, chunk(c)]
                m_new = jnp.maximum(m_old, jnp.max(s, axis=0, keepdims=True))
                es.append(jnp.exp2(s - m_new).astype(BF16))
                alphas.append(jnp.exp2(m_old - m_new))
                m_scr[st, :, chunk(c)] = m_new
            lanes = slice(g * gl, (g + 1) * gl)
            pv = _dot(vT_t[g * V_ROWS:(g + 1) * V_ROWS, :], jnp.concatenate(es, axis=1))
            acc_scr[st, :, lanes] = jnp.concatenate(alphas, axis=1) * acc_scr[st, :, lanes] + pv

    def finish(st, br):
        for c in range(nchunk):
            acc = acc_scr[st, :, chunk(c)]
            inv = 1.0 / jnp.maximum(acc[HEAD_DIM:HEAD_DIM + 1, :], 1e-30)
            oT_scr[:, chunk(c)] += acc[0:HEAD_DIM, :] * (inv * gate(br, c))

    for i in range(n_back, -1, -1):
        rows = slice(i * tk, (i + 1) * tk)
        softmax_pv(WIN, lambda c: swin_scr[rows, chunk(c)], vwinT_ref[lo + i], None,
                   lambda tsub: wbias_ref[case, rows, sub(tsub)])
    finish(WIN, 2)

    def sp_sel(kt, slot, tile_bias):
        softmax_pv(SEL, lambda c: s_scr[slot, :, chunk(c)], vselT_ref[kt],
                   [selb_scr[g, kt, 0:bpt, :] for g in range(N_KV_GROUPS)], tile_bias)

    @pl.when(start == 1)
    def _():
        qk_sel(0, 1)
        sp_sel(0, 1, None)

    def pair_body(j, carry):
        a = start + 2 * j
        qk_sel(a + 1, 1)
        sp_sel(a, 0, None)
        qk_sel(a + 2, 0)
        sp_sel(a + 1, 1, None)
        return carry

    lax.fori_loop(0, p // 2, pair_body, 0)
    sp_sel(p, 0, lambda tsub: wbias_ref[n_back, n_back * tk:(n_back + 1) * tk, sub(tsub)])
    finish(SEL, 1)

    for tsub in range(nsub):
        for i in range(N_HEADS // 2):
            pair = jnp.concatenate([oT_scr[:, chunk((2 * i) * nsub + tsub)],
                                    oT_scr[:, chunk((2 * i + 1) * nsub + tsub)]], axis=0)
            o_ref[sub(tsub), i * 2 * HEAD_DIM:(i + 1) * 2 * HEAD_DIM] = pair.T.astype(BF16)


def _window_bias(t, n_back):
    r = np.arange(t)[:, None]
    c = np.arange(t)[None, :]
    causal = np.where(r <= c, 0.0, NEG)
    edge = np.where(r > c, 0.0, NEG)
    out = np.zeros((n_back + 1, (n_back + 1) * t, t), np.float32)
    for case in range(n_back + 1):
        for i in range(n_back + 1):
            back = case - i
            tile = NEG if back < 0 else causal if back == 0 else edge if back == n_back else 0.0
            out[case, i * t:(i + 1) * t, :] = tile
    return jnp.asarray(out)


def _attn(qT, qrT, kc, vcT, ksel, vselT, kwin, vwinT, gT, mT, B, S):
    tq, tk = ATT_TQ, ATT_TK
    assert tq == tk and WINDOW % tk == 0 and tk % SEL_BLOCK == 0 and tq % LANES == 0
    assert tk // SEL_BLOCK <= SUBLANES
    n_back = WINDOW // tk
    assert S // tk > n_back
    P = S // tq
    ncp = kc.shape[1]
    ns = S // SEL_BLOCK
    nq = N_HEADS * tq
    qspec = pl.BlockSpec((None, NSA_DIM, tq), lambda b, p: (b, 0, p))
    kspec = pl.BlockSpec((None, S // tk, tk, KV_DIM), lambda b, p: (b, 0, 0, 0))
    vspec = pl.BlockSpec((None, S // tk, N_KV_GROUPS * V_ROWS, tk), lambda b, p: (b, 0, 0, 0))
    return pl.pallas_call(
        _attn_kernel,
        grid=(B, P),
        in_specs=[
            qspec, qspec,
            pl.BlockSpec((None, ncp, KV_DIM), lambda b, p: (b, 0, 0)),
            pl.BlockSpec((None, KV_DIM, ncp), lambda b, p: (b, 0, 0)),
            kspec, vspec, kspec, vspec,
            pl.BlockSpec((None, N_KV_GROUPS * GATE_ROWS, tq), lambda b, p: (b, 0, p)),
            pl.BlockSpec((ns, ncp), lambda b, p: (0, 0)),
            pl.BlockSpec((n_back + 1, (n_back + 1) * tk, tq), lambda b, p: (0, 0, 0)),
        ],
        out_specs=pl.BlockSpec((tq, NSA_DIM), lambda b, p: (b * P + p, 0)),
        out_shape=jax.ShapeDtypeStruct((B * S, NSA_DIM), BF16),
        scratch_shapes=[
            pltpu.VMEM((N_KV_GROUPS, S // tk, SUBLANES, tq), F32),
            pltpu.VMEM((KV_DIM, nq), BF16),
            pltpu.VMEM((KV_DIM, nq), BF16),
            pltpu.VMEM((2, tk, nq), F32),
            pltpu.VMEM(((n_back + 1) * tk, nq), F32),
            pltpu.VMEM((2, 1, nq), F32),
            pltpu.VMEM((2, V_ROWS, nq), F32),
            pltpu.VMEM((HEAD_DIM, nq), F32),
        ],
        compiler_params=pltpu.CompilerParams(
            dimension_semantics=("arbitrary", "arbitrary"), vmem_limit_bytes=VMEM_LIMIT),
        name="attn",
    )(qT, qrT, kc, vcT, ksel, vselT, kwin, vwinT, gT, mT, _window_bias(tk, n_back))


def _layer_norm(z, g, b):
    mu = jnp.mean(z, axis=-1, keepdims=True)
    zc = z - mu
    var = jnp.mean(zc * zc, axis=-1, keepdims=True)
    return zc * lax.rsqrt(var + LN_EPS) * g + b


def _tail_kernel(alpha, x_ref, mix_ref, o_ref, wg_ref, wco_ref, wno_ref, wo_ref, g1_ref, b1_ref,
                 wup_ref, wdn_ref, g2_ref, b2_ref, out_ref):
    x = x_ref[...]
    xb = x.astype(BF16)
    y_conv = _dot(mix_ref[...], wco_ref[...])
    y_nsa = _dot(o_ref[...], wno_ref[...])
    mixed = (jax.nn.sigmoid(_dot(xb, wg_ref[:, 0:D_MODEL])) * y_conv
             + jax.nn.sigmoid(_dot(xb, wg_ref[:, D_MODEL:2 * D_MODEL])) * y_nsa)
    x1 = _layer_norm(alpha * x + _dot(mixed.astype(BF16), wo_ref[...]), g1_ref[...], b1_ref[...])
    x1b = x1.astype(BF16)
    ff = jnp.zeros_like(x1)
    for c in range(D_FF // FF_CHUNK):
        h = jnp.maximum(_dot(x1b, wup_ref[:, c * FF_CHUNK:(c + 1) * FF_CHUNK]), 0.0)
        ff = ff + _dot((h * h).astype(BF16), wdn_ref[c * FF_CHUNK:(c + 1) * FF_CHUNK, :])
    out_ref[...] = _layer_norm(alpha * x1 + ff, g2_ref[...], b2_ref[...])


def _tail(x2, mix, o, wg, wco, wno, wo, g1, b1, wup, wdn, g2, b2, alpha):
    n = x2.shape[0]
    tm = TAIL_TM
    row = lambda i: (i, 0)
    const = lambda i: (0, 0)
    resident = lambda shape: pl.BlockSpec(shape, const, pipeline_mode=pl.Buffered(1))
    return pl.pallas_call(
        functools.partial(_tail_kernel, alpha),
        grid=(n // tm,),
        in_specs=[
            pl.BlockSpec((tm, D_MODEL), row), pl.BlockSpec((tm, CONV_DIM), row), pl.BlockSpec((tm, NSA_DIM), row),
            resident((D_MODEL, 2 * D_MODEL)), resident((CONV_DIM, D_MODEL)), resident((NSA_DIM, D_MODEL)),
            resident((D_MODEL, D_MODEL)), resident((1, D_MODEL)), resident((1, D_MODEL)),
            resident((D_MODEL, D_FF)), resident((D_FF, D_MODEL)), resident((1, D_MODEL)), resident((1, D_MODEL)),
        ],
        out_specs=pl.BlockSpec((tm, D_MODEL), row),
        out_shape=jax.ShapeDtypeStruct((n, D_MODEL), F32),
        compiler_params=pltpu.CompilerParams(dimension_semantics=("arbitrary",), vmem_limit_bytes=VMEM_LIMIT),
        name="tail",
    )(x2, mix, o, wg, wco, wno, wo, g1, b1, wup, wdn, g2, b2)


def _rope_tables(S):
    inv = ROPE_THETA ** (-jnp.arange(0, ROPE_DIM, 2, dtype=F32) / ROPE_DIM)
    ang = jnp.arange(S, dtype=F32)[:, None] * inv[None, :]
    cos, sin = jnp.cos(ang), jnp.sin(ang)
    pad_c = jnp.ones((S, HEAD_DIM - ROPE_DIM), F32)
    pad_s = jnp.zeros((S, HEAD_DIM - ROPE_DIM), F32)
    ck = jnp.tile(jnp.concatenate([cos, cos, pad_c], axis=1), (1, N_KV_GROUPS))
    sk = jnp.tile(jnp.concatenate([-sin, sin, pad_s], axis=1), (1, N_KV_GROUPS))
    return ck, sk, cos.T, sin.T


def _overlap_matrix_t(ncp, ns):
    nc = ncp - 1
    i = np.arange(ncp)[None, :]
    j = np.arange(ns)[:, None]
    m = (i * CMP_STRIDE < (j + 1) * SEL_BLOCK) & (i * CMP_STRIDE + CMP_BLOCK > j * SEL_BLOCK) & (i < nc)
    return jnp.asarray(m.astype(np.float32))


def _gate_perm():
    idx = np.zeros((N_KV_GROUPS, GATE_ROWS), np.int64)
    valid = np.zeros((N_KV_GROUPS, GATE_ROWS), np.float32)
    for g in range(N_KV_GROUPS):
        for br in range(N_NSA_BRANCHES):
            for z in range(HEADS_PER_GROUP):
                idx[g, br * HEADS_PER_GROUP + z] = O_GATE + (g * HEADS_PER_GROUP + z) * N_NSA_BRANCHES + br
                valid[g, br * HEADS_PER_GROUP + z] = 1.0
    return idx.reshape(-1), valid.reshape(-1)


def _compress_weights(pe, w1, w2):
    eye = jnp.eye(N_KV_GROUPS, dtype=F32)
    peh = pe.reshape(2, CMP_STRIDE, 1, HEAD_DIM)
    pebig = jnp.broadcast_to(peh, (2, CMP_STRIDE, N_KV_GROUPS, HEAD_DIM)).reshape(2, 1, CMP_STRIDE * KV_DIM)
    pebig = jnp.broadcast_to(pebig, (2, SUBLANES, CMP_STRIDE * KV_DIM))
    w2big = jnp.einsum('ab,hd->ahbd', eye, w2).reshape(N_KV_GROUPS * CMP_HIDDEN, KV_DIM)
    return pebig.astype(BF16), w1.astype(BF16), w2big.astype(BF16)


def _layer(x2, B, S, w_in, conv_w, w_conv_out, pe_k, wk1, wk2, pe_v, wv1, wv2, w_nsa_out, w_o,
           ln1_g, ln1_b, w_up, w_down, ln2_g, ln2_b, alpha):
    ncp = S // CMP_STRIDE
    ns = S // SEL_BLOCK
    gidx, gvalid = _gate_perm()
    wstd = jnp.concatenate([w_in[:, O_H:O_Q], w_in[:, O_KCMP:O_KSEL], w_in[:, O_KSEL:O_VSEL],
                            w_in[:, O_KWIN:O_VWIN]], axis=1).astype(BF16)
    wtr = jnp.concatenate([w_in[:, O_Q:O_KCMP], w_in[:, O_VSEL:O_KWIN], w_in[:, O_VWIN:O_GATE],
                           w_in[:, gidx] * gvalid[None, :]], axis=1).T.astype(BF16)
    wg = w_in[:, O_GCONV:O_END].astype(BF16)
    ck, sk, cq, sq = _rope_tables(S)

    mix, kch, vch, ksel, kwin, qT, qrT, vselT, vwinT, gT = _proj(
        x2, wstd, wtr, conv_w.reshape(CONV_WIDTH, CONV_DIM), ck, sk, cq, sq, B, S)

    pek, wk1b, wk2b = _compress_weights(pe_k, wk1, wk2)
    pev, wv1b, wv2b = _compress_weights(pe_v, wv1, wv2)
    kc, vcT = _compress(kch, vch, pek, pev, wk1b, wv1b, wk2b, wv2b.T, B, ncp)

    tk = ATT_TK
    o = _attn(qT, qrT, kc, vcT, ksel.reshape(B, S // tk, tk, KV_DIM), vselT,
              kwin.reshape(B, S // tk, tk, KV_DIM), vwinT, gT, _overlap_matrix_t(ncp, ns), B, S)

    row = lambda v: v.reshape(1, D_MODEL).astype(F32)
    return _tail(x2, mix, o, wg, w_conv_out.astype(BF16), w_nsa_out.astype(BF16), w_o.astype(BF16),
                 row(ln1_g), row(ln1_b), w_up.astype(BF16), w_down.astype(BF16), row(ln2_g), row(ln2_b), alpha)


def kernel(x, w_in, conv_w, w_conv_out, pe_k_cmp, w_k_cmp1, w_k_cmp2, pe_v_cmp, w_v_cmp1, w_v_cmp2,
           w_nsa_out, w_o, ln1_g, ln1_b, w_up, w_down, ln2_g, ln2_b):
    B, S, D = x.shape
    depth = w_in.shape[0]
    assert D == D_MODEL and w_in.shape[2] == O_END
    assert S % PROJ_TM == 0 and S % ATT_TQ == 0 and (B * S) % TAIL_TM == 0 and S >= WINDOW
    alpha = float((2 * depth) ** 0.25)
    x2 = x.reshape(B * S, D)
    for l in range(depth):
        x2 = _layer(x2, B, S, w_in[l], conv_w[l], w_conv_out[l], pe_k_cmp[l], w_k_cmp1[l], w_k_cmp2[l],
                    pe_v_cmp[l], w_v_cmp1[l], w_v_cmp2[l], w_nsa_out[l], w_o[l], ln1_g[l], ln1_b[l],
                    w_up[l], w_down[l], ln2_g[l], ln2_b[l], alpha)
    return x2.reshape(B, S, D)
```

```python
import functools

import numpy as np
import jax
import jax.numpy as jnp
from jax import lax
from jax.experimental import pallas as pl
from jax.experimental.pallas import tpu as pltpu

F32 = jnp.float32
BF16 = jnp.bfloat16

D_MODEL = 1024
CONV_DIM = D_MODEL // 2
CONV_WIDTH = 3
N_HEADS = 8
HEAD_DIM = 64
N_KV_GROUPS = 2
HEADS_PER_GROUP = N_HEADS // N_KV_GROUPS
NSA_DIM = N_HEADS * HEAD_DIM
KV_DIM = N_KV_GROUPS * HEAD_DIM
ROPE_DIM = HEAD_DIM // 4
ROPE_HALF = ROPE_DIM // 2
ROPE_THETA = 500000.0
CMP_BLOCK = 32
CMP_STRIDE = 16
CMP_HIDDEN = 2 * HEAD_DIM
SEL_BLOCK = 64
N_SEL = 16
WINDOW = 512
N_NSA_BRANCHES = 3
D_FF = 4 * D_MODEL
LN_EPS = 1e-5
NEG = -1e30
FORCE = 1e9
SCALE = HEAD_DIM ** -0.5
LOG2E = 1.4426950408889634

SUBLANES = 8
LANES = 128
BF16_ROWS = 16
V_ROWS = HEAD_DIM + BF16_ROWS

_SPLITS = (CONV_DIM, CONV_DIM, CONV_DIM, NSA_DIM, KV_DIM, KV_DIM, KV_DIM, KV_DIM, KV_DIM, KV_DIM,
           N_HEADS * N_NSA_BRANCHES, D_MODEL, D_MODEL)
_OFFS = np.concatenate([[0], np.cumsum(_SPLITS)]).tolist()
(O_H, O_B, O_C, O_Q, O_KCMP, O_VCMP, O_KSEL, O_VSEL, O_KWIN, O_VWIN, O_GATE, O_GCONV, O_GNSA, O_END) = _OFFS

GATE_ROWS = 16
N_STD = 3 * CONV_DIM + 4 * KV_DIM
N_TR = NSA_DIM + 2 * KV_DIM + N_KV_GROUPS * GATE_ROWS

PROJ_TM = 512
ATT_TQ = 256
ATT_TK = 256
TAIL_TM = 512
FF_CHUNK = 1024
VMEM_LIMIT = 56 * 1024 * 1024


def _dot(a, b):
    return jnp.dot(a, b, preferred_element_type=F32)


def _dot_nt(a, b):
    return lax.dot_general(a, b, (((1,), (1,)), ((), ())), preferred_element_type=F32)


def _proj_kernel(x_ref, wstd_ref, wtr_ref, convw_ref, cosk_ref, sink_ref, cosq_ref, sinq_ref,
                 mix_ref, kch_ref, vch_ref, ksel_ref, kwin_ref,
                 qT_ref, qrT_ref, vselT_ref, vwinT_ref, gT_ref, ubuf, kvbuf):
    tm = x_ref.shape[0]
    s = pl.program_id(1)
    xb = x_ref[...].astype(BF16)

    hbc = _dot(xb, wstd_ref[:, 0:3 * CONV_DIM])
    u = hbc[:, 2 * CONV_DIM:3 * CONV_DIM] * hbc[:, 0:CONV_DIM]

    @pl.when(s == 0)
    def _():
        ubuf[0:SUBLANES, :] = jnp.zeros((SUBLANES, CONV_DIM), F32)

    ubuf[SUBLANES:SUBLANES + tm, :] = u
    u1 = ubuf[SUBLANES - 1:SUBLANES - 1 + tm, :]
    u2 = ubuf[SUBLANES - 2:SUBLANES - 2 + tm, :]
    cw = convw_ref[...]
    y = cw[0:1, :] * u2 + cw[1:2, :] * u1 + cw[2:3, :] * u
    mix_ref[...] = (hbc[:, CONV_DIM:2 * CONV_DIM] * y).astype(BF16)
    ubuf[0:SUBLANES, :] = ubuf[tm:tm + SUBLANES, :]

    kk = _dot(xb, wstd_ref[:, 3 * CONV_DIM:N_STD])
    for j, out in enumerate((kch_ref, vch_ref)):
        kvbuf[j] = kk[:, j * KV_DIM:(j + 1) * KV_DIM]
        for r in range(CMP_STRIDE):
            rows = kvbuf[j, pl.ds(r, tm // CMP_STRIDE, stride=CMP_STRIDE), :]
            out[:, r * KV_DIM:(r + 1) * KV_DIM] = rows.astype(BF16)
    lane = lax.broadcasted_iota(jnp.int32, (tm, KV_DIM), 1)
    first_half = (lane % ROPE_DIM) < ROPE_HALF
    ck = cosk_ref[...]
    sk = sink_ref[...]
    for j, out in ((2, ksel_ref), (3, kwin_ref)):
        k = kk[:, j * KV_DIM:(j + 1) * KV_DIM]
        partner = jnp.where(first_half, pltpu.roll(k, KV_DIM - ROPE_HALF, 1), pltpu.roll(k, ROPE_HALF, 1))
        out[...] = (k * ck + partner * sk).astype(BF16)

    yT = _dot_nt(wtr_ref[...], xb)
    q = yT[0:NSA_DIM, :] * (SCALE * LOG2E)
    qT_ref[...] = q.astype(BF16)
    qrT_ref[...] = q.astype(BF16)
    cq = cosq_ref[...]
    sq = sinq_ref[...]
    for h in range(N_HEADS):
        r0 = h * HEAD_DIM
        x1 = q[r0:r0 + ROPE_HALF, :]
        x2 = q[r0 + ROPE_HALF:r0 + ROPE_DIM, :]
        rot = jnp.concatenate([x1 * cq - x2 * sq, x2 * cq + x1 * sq], axis=0)
        qrT_ref[r0:r0 + ROPE_DIM, :] = rot.astype(BF16)
    tk = vselT_ref.shape[-1]
    ones = jnp.ones((BF16_ROWS, tk), BF16)
    for j, out in enumerate((vselT_ref, vwinT_ref)):
        for i in range(tm // tk):
            for g in range(N_KV_GROUPS):
                r0 = NSA_DIM + j * KV_DIM + g * HEAD_DIM
                out[i, g * V_ROWS:g * V_ROWS + HEAD_DIM, :] = yT[r0:r0 + HEAD_DIM, i * tk:(i + 1) * tk].astype(BF16)
                out[i, g * V_ROWS + HEAD_DIM:(g + 1) * V_ROWS, :] = ones
    gT_ref[...] = jax.nn.sigmoid(yT[NSA_DIM + 2 * KV_DIM:N_TR, :])


def _proj(x2, wstd, wtr, convw, cosk, sink, cosq, sinq, B, S):
    tm, tk = PROJ_TM, ATT_TK
    ns = S // tm
    row = lambda b, s: (b * ns + s, 0)
    const = lambda b, s: (0, 0)
    tok = lambda c, dt: jax.ShapeDtypeStruct((B * S, c), dt)
    cw = CMP_STRIDE * KV_DIM
    chunked = jax.ShapeDtypeStruct((B, S // CMP_STRIDE, cw), BF16)
    chunk_spec = pl.BlockSpec((None, tm // CMP_STRIDE, cw), lambda b, s: (b, s, 0))
    out_shape = (
        tok(CONV_DIM, BF16), chunked, chunked, tok(KV_DIM, BF16), tok(KV_DIM, BF16),
        jax.ShapeDtypeStruct((B, NSA_DIM, S), BF16), jax.ShapeDtypeStruct((B, NSA_DIM, S), BF16),
        jax.ShapeDtypeStruct((B, S // tk, N_KV_GROUPS * V_ROWS, tk), BF16),
        jax.ShapeDtypeStruct((B, S // tk, N_KV_GROUPS * V_ROWS, tk), BF16),
        jax.ShapeDtypeStruct((B, N_KV_GROUPS * GATE_ROWS, S), F32),
    )
    feat = lambda r: pl.BlockSpec((None, r, tm), lambda b, s: (b, 0, s))
    vt = pl.BlockSpec((None, tm // tk, N_KV_GROUPS * V_ROWS, tk), lambda b, s: (b, s, 0, 0))
    return pl.pallas_call(
        _proj_kernel,
        grid=(B, ns),
        in_specs=[
            pl.BlockSpec((tm, D_MODEL), row),
            pl.BlockSpec((D_MODEL, N_STD), const),
            pl.BlockSpec((N_TR, D_MODEL), const),
            pl.BlockSpec((CONV_WIDTH, CONV_DIM), const),
            pl.BlockSpec((tm, KV_DIM), lambda b, s: (s, 0)),
            pl.BlockSpec((tm, KV_DIM), lambda b, s: (s, 0)),
            pl.BlockSpec((ROPE_HALF, tm), lambda b, s: (0, s)),
            pl.BlockSpec((ROPE_HALF, tm), lambda b, s: (0, s)),
        ],
        out_specs=(
            pl.BlockSpec((tm, CONV_DIM), row), chunk_spec, chunk_spec,
            pl.BlockSpec((tm, KV_DIM), row), pl.BlockSpec((tm, KV_DIM), row),
            feat(NSA_DIM), feat(NSA_DIM), vt, vt, feat(N_KV_GROUPS * GATE_ROWS),
        ),
        out_shape=out_shape,
        scratch_shapes=[pltpu.VMEM((tm + SUBLANES, CONV_DIM), F32), pltpu.VMEM((2, tm, KV_DIM), F32)],
        compiler_params=pltpu.CompilerParams(
            dimension_semantics=("arbitrary", "arbitrary"), vmem_limit_bytes=VMEM_LIMIT),
        name="proj",
    )(x2, wstd, wtr, convw, cosk, sink, cosq, sinq)


def _gelu_tanh(x):
    return x * (0.5 * (1.0 + jnp.tanh(np.sqrt(2.0 / np.pi).astype(np.float32) * (x + 0.044715 * (x * x * x)))))


def _compress_kernel(kch_ref, vch_ref, pek_ref, pev_ref, wk1_ref, wv1_ref, wk2_ref, wv2t_ref, kc_ref, vcT_ref,
                     wk1_big, wv1_big):
    @pl.when(pl.program_id(0) == 0)
    def _():
        for w_ref, big in ((wk1_ref, wk1_big), (wv1_ref, wv1_big)):
            big[...] = jnp.zeros(big.shape, BF16)
            for half in range(2):
                for r in range(CMP_STRIDE):
                    for g in range(N_KV_GROUPS):
                        r0 = r * KV_DIM + g * HEAD_DIM
                        big[half, r0:r0 + HEAD_DIM, g * CMP_HIDDEN:(g + 1) * CMP_HIDDEN] = (
                            w_ref[half * CMP_STRIDE + r])

    def hidden(ch_ref, pe_ref, big):
        ch = ch_ref[...]
        a = _dot(ch, big[0])
        b = _dot(ch, big[1])
        pe = _dot(pe_ref[0], big[0]) + _dot(pe_ref[1], big[1])
        b_next = pltpu.roll(b, b.shape[0] - 1, 0)
        return _gelu_tanh(a + b_next + pe[0:1, :]).astype(BF16)

    kc_ref[...] = _dot(hidden(kch_ref, pek_ref, wk1_big), wk2_ref[...]).astype(BF16)
    vcT_ref[...] = _dot_nt(wv2t_ref[...], hidden(vch_ref, pev_ref, wv1_big)).astype(BF16)


def _compress(kch, vch, pek, pev, wk1, wv1, wk2, wv2t, B, ncp):
    cw = CMP_STRIDE * KV_DIM
    gh = N_KV_GROUPS * CMP_HIDDEN
    c3 = lambda b: (0, 0, 0)
    c2 = lambda b: (0, 0)
    return pl.pallas_call(
        _compress_kernel,
        grid=(B,),
        in_specs=[
            pl.BlockSpec((None, ncp, cw), lambda b: (b, 0, 0)),
            pl.BlockSpec((None, ncp, cw), lambda b: (b, 0, 0)),
            pl.BlockSpec((2, SUBLANES, cw), c3), pl.BlockSpec((2, SUBLANES, cw), c3),
            pl.BlockSpec((CMP_BLOCK, HEAD_DIM, CMP_HIDDEN), c3), pl.BlockSpec((CMP_BLOCK, HEAD_DIM, CMP_HIDDEN), c3),
            pl.BlockSpec((gh, KV_DIM), c2), pl.BlockSpec((KV_DIM, gh), c2),
        ],
        out_specs=(pl.BlockSpec((None, ncp, KV_DIM), lambda b: (b, 0, 0)),
                   pl.BlockSpec((None, KV_DIM, ncp), lambda b: (b, 0, 0))),
        out_shape=(jax.ShapeDtypeStruct((B, ncp, KV_DIM), BF16), jax.ShapeDtypeStruct((B, KV_DIM, ncp), BF16)),
        scratch_shapes=[pltpu.VMEM((2, cw, gh), BF16), pltpu.VMEM((2, cw, gh), BF16)],
        compiler_params=pltpu.CompilerParams(dimension_semantics=("arbitrary",), vmem_limit_bytes=VMEM_LIMIT),
        name="compress",
    )(kch, vch, pek, pev, wk1, wv1, wk2, wv2t)


def _attn_kernel(qT_ref, qrT_ref, kc_ref, vcT_ref, ksel_ref, vselT_ref, kwin_ref, vwinT_ref, gT_ref, mT_ref,
                 wbias_ref, o_ref, selb_scr, q_scr, qr_scr, s_scr, swin_scr, m_scr, acc_scr, oT_scr):
    tq = qT_ref.shape[1]
    tk = ksel_ref.shape[1]
    nsub = tq // LANES
    nchunk = N_HEADS * nsub
    gchunk = HEADS_PER_GROUP * nsub
    gl = HEADS_PER_GROUP * tq
    ns = mT_ref.shape[0]
    bpt = tk // SEL_BLOCK
    p = pl.program_id(1)
    qbase = p * tq

    def chunk(c):
        return slice(c * LANES, (c + 1) * LANES)

    def sub(tsub):
        return slice(tsub * LANES, (tsub + 1) * LANES)

    def stage_q(ref, scr):
        zeros = jnp.zeros((HEAD_DIM, tq), BF16)
        for h in range(N_HEADS):
            qh = ref[h * HEAD_DIM:(h + 1) * HEAD_DIM, :]
            halves = [qh, zeros] if h // HEADS_PER_GROUP == 0 else [zeros, qh]
            scr[:, h * tq:(h + 1) * tq] = jnp.concatenate(halves, axis=0)

    gt = gT_ref[...]

    def gate(br, c):
        h, tsub = divmod(c, nsub)
        g, z = divmod(h, HEADS_PER_GROUP)
        r = g * GATE_ROWS + br * HEADS_PER_GROUP + z
        return gt[r:r + 1, sub(tsub)]

    stage_q(qT_ref, q_scr)
    stage_q(qrT_ref, qr_scr)
    start = p % 2
    n_back = WINDOW // tk
    case = jnp.minimum(p, n_back)
    lo = p - case

    def qk_sel(kt, slot):
        s_scr[slot] = _dot(ksel_ref[kt], qr_scr[...]).astype(BF16)

    kc = kc_ref[...]
    mT = mT_ref[...]
    ncp = kc.shape[0]
    s_cmp = _dot(kc, q_scr[...])
    k_win = kwin_ref[pl.ds(lo, n_back + 1)].reshape((n_back + 1) * tk, KV_DIM)
    swin_scr[...] = _dot(k_win, qr_scr[...]).astype(BF16)
    qk_sel(start, 0)

    imp = []
    for g in range(N_KV_GROUPS):
        vcT = vcT_ref[g * HEAD_DIM:(g + 1) * HEAD_DIM, :]
        imp_parts = []
        for tsub in range(nsub):
            n_idx = lax.broadcasted_iota(jnp.int32, (ncp, LANES), 0)
            t_idx = qbase + tsub * LANES + lax.broadcasted_iota(jnp.int32, (ncp, LANES), 1)
            ok = (n_idx * CMP_STRIDE + (CMP_BLOCK - 1)) <= t_idx
            bias = jnp.where(ok, 0.0, NEG).astype(F32)
            keep = ok.astype(F32)
            p_sum = None
            for z in range(HEADS_PER_GROUP):
                c = (g * HEADS_PER_GROUP + z) * nsub + tsub
                s = s_cmp[:, chunk(c)] + bias
                e = jnp.exp2(s - jnp.max(s, axis=0, keepdims=True)) * keep
                inv = 1.0 / jnp.maximum(jnp.sum(e, axis=0, keepdims=True), 1e-30)
                oT_scr[:, chunk(c)] = _dot(vcT, e.astype(BF16)) * (inv * gate(0, c))
                p_sum = e * inv if p_sum is None else p_sum + e * inv
            imp_parts.append(jnp.dot(mT, p_sum, preferred_element_type=F32, precision=lax.Precision.HIGHEST))
        imp.append(jnp.concatenate(imp_parts, axis=1))

    j_idx = lax.broadcasted_iota(jnp.int32, (ns, tq), 0)
    cur = (qbase + lax.broadcasted_iota(jnp.int32, (ns, tq), 1)) // SEL_BLOCK
    forced = (j_idx == 0) | (j_idx == cur) | (j_idx == cur - 1)
    for g in range(N_KV_GROUPS):
        impv = jnp.where(j_idx <= cur, jnp.where(forced, FORCE, imp[g]), NEG)
        cnt = jnp.zeros((ns, tq), F32)
        for jp in range(ns):
            row = impv[jp:jp + 1, :]
            beats = (row > impv) | ((row == impv) & (j_idx > jp))
            cnt = cnt + beats.astype(F32)
        sel = (cnt < float(min(N_SEL, ns))) & (impv > 0.5 * NEG)
        selb = jnp.where(sel, 0.0, NEG).astype(F32)
        for kt in range(ns // bpt):
            selb_scr[g, kt, 0:bpt, :] = selb[kt * bpt:(kt + 1) * bpt, :]

    SEL, WIN = 0, 1
    m_scr[...] = jnp.full(m_scr.shape, NEG, F32)
    acc_scr[...] = jnp.zeros(acc_scr.shape, F32)

    def softmax_pv(st, load_s, vT_t, block_bias, tile_bias):
        for g in range(N_KV_GROUPS):
            es, alphas = [], []
            for cg in range(gchunk):
                c = g * gchunk + cg
                tsub = c % nsub
                s = load_s(c)
                if block_bias is not None:
                    rb = block_bias[g][:, sub(tsub)].astype(BF16)
                    s = s + jnp.concatenate(
                        [jnp.broadcast_to(rb[i:i + 1, :], (SEL_BLOCK, LANES)) for i in range(bpt)], axis=0)
                if tile_bias is not None:
                    s = s + tile_bias(tsub)
                m_old = m_scr[st, :, chunk(c)]
                m_new = jnp.maximum(m_old, jnp.max(s, axis=0, keepdims=True).astype(F32))
                es.append(jnp.exp2(s - m_new.astype(BF16)))
                alphas.append(jnp.exp2(m_old - m_new))
                m_scr[st, :, chunk(c)] = m_new
            lanes = slice(g * gl, (g + 1) * gl)
            pv = _dot(vT_t[g * V_ROWS:(g + 1) * V_ROWS, :], jnp.concatenate(es, axis=1))
            acc_scr[st, :, lanes] = jnp.concatenate(alphas, axis=1) * acc_scr[st, :, lanes] + pv

    def finish(st, br):
        for c in range(nchunk):
            acc = acc_scr[st, :, chunk(c)]
            inv = 1.0 / jnp.maximum(acc[HEAD_DIM:HEAD_DIM + 1, :], 1e-30)
            oT_scr[:, chunk(c)] += acc[0:HEAD_DIM, :] * (inv * gate(br, c))

    for i in range(n_back, -1, -1):
        rows = slice(i * tk, (i + 1) * tk)
        softmax_pv(WIN, lambda c: swin_scr[rows, chunk(c)], vwinT_ref[lo + i], None,
                   lambda tsub: wbias_ref[case, rows, sub(tsub)])
    finish(WIN, 2)

    def sp_sel(kt, slot, tile_bias):
        softmax_pv(SEL, lambda c: s_scr[slot, :, chunk(c)], vselT_ref[kt],
                   [selb_scr[g, kt, 0:bpt, :] for g in range(N_KV_GROUPS)], tile_bias)

    @pl.when(start == 1)
    def _():
        qk_sel(0, 1)
        sp_sel(0, 1, None)

    def pair_body(j, carry):
        a = start + 2 * j
        qk_sel(a + 1, 1)
        sp_sel(a, 0, None)
        qk_sel(a + 2, 0)
        sp_sel(a + 1, 1, None)
        return carry

    lax.fori_loop(0, p // 2, pair_body, 0)
    sp_sel(p, 0, lambda tsub: wbias_ref[n_back, n_back * tk:(n_back + 1) * tk, sub(tsub)])
    finish(SEL, 1)

    for tsub in range(nsub):
        for i in range(N_HEADS // 2):
            pair = jnp.concatenate([oT_scr[:, chunk((2 * i) * nsub + tsub)],
                                    oT_scr[:, chunk((2 * i + 1) * nsub + tsub)]], axis=0)
            o_ref[sub(tsub), i * 2 * HEAD_DIM:(i + 1) * 2 * HEAD_DIM] = pair.T.astype(BF16)


def _window_bias(t, n_back):
    r = np.arange(t)[:, None]
    c = np.arange(t)[None, :]
    causal = np.where(r <= c, 0.0, NEG)
    edge = np.where(r > c, 0.0, NEG)
    out = np.zeros((n_back + 1, (n_back + 1) * t, t), np.float32)
    for case in range(n_back + 1):
        for i in range(n_back + 1):
            back = case - i
            tile = NEG if back < 0 else causal if back == 0 else edge if back == n_back else 0.0
            out[case, i * t:(i + 1) * t, :] = tile
    return jnp.asarray(out).astype(BF16)


def _attn(qT, qrT, kc, vcT, ksel, vselT, kwin, vwinT, gT, mT, B, S):
    tq, tk = ATT_TQ, ATT_TK
    assert tq == tk and WINDOW % tk == 0 and tk % SEL_BLOCK == 0 and tq % LANES == 0
    assert tk // SEL_BLOCK <= SUBLANES
    n_back = WINDOW // tk
    assert S // tk > n_back
    P = S // tq
    ncp = kc.shape[1]
    ns = S // SEL_BLOCK
    nq = N_HEADS * tq
    qspec = pl.BlockSpec((None, NSA_DIM, tq), lambda b, p: (b, 0, p))
    kspec = pl.BlockSpec((None, S // tk, tk, KV_DIM), lambda b, p: (b, 0, 0, 0))
    vspec = pl.BlockSpec((None, S // tk, N_KV_GROUPS * V_ROWS, tk), lambda b, p: (b, 0, 0, 0))
    return pl.pallas_call(
        _attn_kernel,
        grid=(B, P),
        in_specs=[
            qspec, qspec,
            pl.BlockSpec((None, ncp, KV_DIM), lambda b, p: (b, 0, 0)),
            pl.BlockSpec((None, KV_DIM, ncp), lambda b, p: (b, 0, 0)),
            kspec, vspec, kspec, vspec,
            pl.BlockSpec((None, N_KV_GROUPS * GATE_ROWS, tq), lambda b, p: (b, 0, p)),
            pl.BlockSpec((ns, ncp), lambda b, p: (0, 0)),
            pl.BlockSpec((n_back + 1, (n_back + 1) * tk, tq), lambda b, p: (0, 0, 0)),
        ],
        out_specs=pl.BlockSpec((tq, NSA_DIM), lambda b, p: (b * P + p, 0)),
        out_shape=jax.ShapeDtypeStruct((B * S, NSA_DIM), BF16),
        scratch_shapes=[
            pltpu.VMEM((N_KV_GROUPS, S // tk, SUBLANES, tq), F32),
            pltpu.VMEM((KV_DIM, nq), BF16),
            pltpu.VMEM((KV_DIM, nq), BF16),
            pltpu.VMEM((2, tk, nq), BF16),
            pltpu.VMEM(((n_back + 1) * tk, nq), BF16),
            pltpu.VMEM((2, 1, nq), F32),
            pltpu.VMEM((2, V_ROWS, nq), F32),
            pltpu.VMEM((HEAD_DIM, nq), F32),
        ],
        compiler_params=pltpu.CompilerParams(
            dimension_semantics=("arbitrary", "arbitrary"), vmem_limit_bytes=VMEM_LIMIT),
        name="attn",
    )(qT, qrT, kc, vcT, ksel, vselT, kwin, vwinT, gT, mT, _window_bias(tk, n_back))


def _layer_norm(z, g, b):
    mu = jnp.mean(z, axis=-1, keepdims=True)
    zc = z - mu
    var = jnp.mean(zc * zc, axis=-1, keepdims=True)
    return zc * lax.rsqrt(var + LN_EPS) * g + b


def _tail_kernel(alpha, x_ref, mix_ref, o_ref, wg_ref, wco_ref, wno_ref, wo_ref, g1_ref, b1_ref,
                 wup_ref, wdn_ref, g2_ref, b2_ref, out_ref):
    x = x_ref[...]
    xb = x.astype(BF16)
    y_conv = _dot(mix_ref[...], wco_ref[...])
    y_nsa = _dot(o_ref[...], wno_ref[...])
    mixed = (jax.nn.sigmoid(_dot(xb, wg_ref[:, 0:D_MODEL])) * y_conv
             + jax.nn.sigmoid(_dot(xb, wg_ref[:, D_MODEL:2 * D_MODEL])) * y_nsa)
    x1 = _layer_norm(alpha * x + _dot(mixed.astype(BF16), wo_ref[...]), g1_ref[...], b1_ref[...])
    x1b = x1.astype(BF16)
    ff = jnp.zeros_like(x1)
    for c in range(D_FF // FF_CHUNK):
        h = jnp.maximum(_dot(x1b, wup_ref[:, c * FF_CHUNK:(c + 1) * FF_CHUNK]), 0.0)
        ff = ff + _dot((h * h).astype(BF16), wdn_ref[c * FF_CHUNK:(c + 1) * FF_CHUNK, :])
    out_ref[...] = _layer_norm(alpha * x1 + ff, g2_ref[...], b2_ref[...])


def _tail(x2, mix, o, wg, wco, wno, wo, g1, b1, wup, wdn, g2, b2, alpha):
    n = x2.shape[0]
    tm = TAIL_TM
    row = lambda i: (i, 0)
    const = lambda i: (0, 0)
    resident = lambda shape: pl.BlockSpec(shape, const, pipeline_mode=pl.Buffered(1))
    return pl.pallas_call(
        functools.partial(_tail_kernel, alpha),
        grid=(n // tm,),
        in_specs=[
            pl.BlockSpec((tm, D_MODEL), row), pl.BlockSpec((tm, CONV_DIM), row), pl.BlockSpec((tm, NSA_DIM), row),
            resident((D_MODEL, 2 * D_MODEL)), resident((CONV_DIM, D_MODEL)), resident((NSA_DIM, D_MODEL)),
            resident((D_MODEL, D_MODEL)), resident((1, D_MODEL)), resident((1, D_MODEL)),
            resident((D_MODEL, D_FF)), resident((D_FF, D_MODEL)), resident((1, D_MODEL)), resident((1, D_MODEL)),
        ],
        out_specs=pl.BlockSpec((tm, D_MODEL), row),
        out_shape=jax.ShapeDtypeStruct((n, D_MODEL), F32),
        compiler_params=pltpu.CompilerParams(dimension_semantics=("arbitrary",), vmem_limit_bytes=VMEM_LIMIT),
        name="tail",
    )(x2, mix, o, wg, wco, wno, wo, g1, b1, wup, wdn, g2, b2)


def _rope_tables(S):
    inv = ROPE_THETA ** (-jnp.arange(0, ROPE_DIM, 2, dtype=F32) / ROPE_DIM)
    ang = jnp.arange(S, dtype=F32)[:, None] * inv[None, :]
    cos, sin = jnp.cos(ang), jnp.sin(ang)
    pad_c = jnp.ones((S, HEAD_DIM - ROPE_DIM), F32)
    pad_s = jnp.zeros((S, HEAD_DIM - ROPE_DIM), F32)
    ck = jnp.tile(jnp.concatenate([cos, cos, pad_c], axis=1), (1, N_KV_GROUPS))
    sk = jnp.tile(jnp.concatenate([-sin, sin, pad_s], axis=1), (1, N_KV_GROUPS))
    return ck, sk, cos.T, sin.T


def _overlap_matrix_t(ncp, ns):
    nc = ncp - 1
    i = np.arange(ncp)[None, :]
    j = np.arange(ns)[:, None]
    m = (i * CMP_STRIDE < (j + 1) * SEL_BLOCK) & (i * CMP_STRIDE + CMP_BLOCK > j * SEL_BLOCK) & (i < nc)
    return jnp.asarray(m.astype(np.float32))


def _gate_columns(w_in):
    w = w_in[:, O_GATE:O_GCONV].reshape(-1, N_KV_GROUPS, HEADS_PER_GROUP, N_NSA_BRANCHES)
    w = jnp.swapaxes(w, 2, 3).reshape(-1, N_KV_GROUPS, N_NSA_BRANCHES * HEADS_PER_GROUP)
    w = jnp.pad(w, ((0, 0), (0, 0), (0, GATE_ROWS - N_NSA_BRANCHES * HEADS_PER_GROUP)))
    return w.reshape(-1, N_KV_GROUPS * GATE_ROWS)


def _compress_weights(pe, w1, w2):
    eye = jnp.eye(N_KV_GROUPS, dtype=F32)
    peh = pe.reshape(2, CMP_STRIDE, 1, HEAD_DIM)
    pebig = jnp.broadcast_to(peh, (2, CMP_STRIDE, N_KV_GROUPS, HEAD_DIM)).reshape(2, 1, CMP_STRIDE * KV_DIM)
    pebig = jnp.broadcast_to(pebig, (2, SUBLANES, CMP_STRIDE * KV_DIM))
    w2big = jnp.einsum('ab,hd->ahbd', eye, w2).reshape(N_KV_GROUPS * CMP_HIDDEN, KV_DIM)
    return pebig.astype(BF16), w1.astype(BF16), w2big.astype(BF16)


def _layer(x2, B, S, w_in, conv_w, w_conv_out, pe_k, wk1, wk2, pe_v, wv1, wv2, w_nsa_out, w_o,
           ln1_g, ln1_b, w_up, w_down, ln2_g, ln2_b, alpha):
    ncp = S // CMP_STRIDE
    ns = S // SEL_BLOCK
    wstd = jnp.concatenate([w_in[:, O_H:O_Q], w_in[:, O_KCMP:O_KSEL], w_in[:, O_KSEL:O_VSEL],
                            w_in[:, O_KWIN:O_VWIN]], axis=1).astype(BF16)
    wtr = jnp.concatenate([w_in[:, O_Q:O_KCMP], w_in[:, O_VSEL:O_KWIN], w_in[:, O_VWIN:O_GATE],
                           _gate_columns(w_in)], axis=1).T.astype(BF16)
    wg = w_in[:, O_GCONV:O_END].astype(BF16)
    ck, sk, cq, sq = _rope_tables(S)

    mix, kch, vch, ksel, kwin, qT, qrT, vselT, vwinT, gT = _proj(
        x2, wstd, wtr, conv_w.reshape(CONV_WIDTH, CONV_DIM), ck, sk, cq, sq, B, S)

    pek, wk1b, wk2b = _compress_weights(pe_k, wk1, wk2)
    pev, wv1b, wv2b = _compress_weights(pe_v, wv1, wv2)
    kc, vcT = _compress(kch, vch, pek, pev, wk1b, wv1b, wk2b, wv2b.T, B, ncp)

    tk = ATT_TK
    o = _attn(qT, qrT, kc, vcT, ksel.reshape(B, S // tk, tk, KV_DIM), vselT,
              kwin.reshape(B, S // tk, tk, KV_DIM), vwinT, gT, _overlap_matrix_t(ncp, ns), B, S)

    row = lambda v: v.reshape(1, D_MODEL).astype(F32)
    return _tail(x2, mix, o, wg, w_conv_out.astype(BF16), w_nsa_out.astype(BF16), w_o.astype(BF16),
                 row(ln1_g), row(ln1_b), w_up.astype(BF16), w_down.astype(BF16), row(ln2_g), row(ln2_b), alpha)


def kernel(x, w_in, conv_w, w_conv_out, pe_k_cmp, w_k_cmp1, w_k_cmp2, pe_v_cmp, w_v_cmp1, w_v_cmp2,
           w_nsa_out, w_o, ln1_g, ln1_b, w_up, w_down, ln2_g, ln2_b):
    B, S, D = x.shape
    depth = w_in.shape[0]
    assert D == D_MODEL and w_in.shape[2] == O_END
    assert S % PROJ_TM == 0 and S % ATT_TQ == 0 and (B * S) % TAIL_TM == 0 and S >= WINDOW
    alpha = float((2 * depth) ** 0.25)
    x2 = x.reshape(B * S, D)
    for l in range(depth):
        x2 = _layer(x2, B, S, w_in[l], conv_w[l], w_conv_out[l], pe_k_cmp[l], w_k_cmp1[l], w_k_cmp2[l],
                    pe_v_cmp[l], w_v_cmp1[l], w_v_cmp2[l], w_nsa_out[l], w_o[l], ln1_g[l], ln1_b[l],
                    w_up[l], w_down[l], ln2_g[l], ln2_b[l], alpha)
    return x2.reshape(B, S, D)
```

```python
import functools

import numpy as np
import jax
import jax.numpy as jnp
from jax import lax
from jax.experimental import pallas as pl
from jax.experimental.pallas import tpu as pltpu

F32 = jnp.float32
BF16 = jnp.bfloat16

D_MODEL = 1024
CONV_DIM = D_MODEL // 2
CONV_WIDTH = 3
N_HEADS = 8
HEAD_DIM = 64
N_KV_GROUPS = 2
HEADS_PER_GROUP = N_HEADS // N_KV_GROUPS
NSA_DIM = N_HEADS * HEAD_DIM
KV_DIM = N_KV_GROUPS * HEAD_DIM
ROPE_DIM = HEAD_DIM // 4
ROPE_HALF = ROPE_DIM // 2
ROPE_THETA = 500000.0
CMP_BLOCK = 32
CMP_STRIDE = 16
CMP_HIDDEN = 2 * HEAD_DIM
SEL_BLOCK = 64
N_SEL = 16
WINDOW = 512
N_NSA_BRANCHES = 3
D_FF = 4 * D_MODEL
LN_EPS = 1e-5
NEG = -1e30
FORCE = 1e9
SCALE = HEAD_DIM ** -0.5
LOG2E = 1.4426950408889634

SUBLANES = 8
LANES = 128
BF16_ROWS = 16
V_ROWS = HEAD_DIM + BF16_ROWS

_SPLITS = (CONV_DIM, CONV_DIM, CONV_DIM, NSA_DIM, KV_DIM, KV_DIM, KV_DIM, KV_DIM, KV_DIM, KV_DIM,
           N_HEADS * N_NSA_BRANCHES, D_MODEL, D_MODEL)
_OFFS = np.concatenate([[0], np.cumsum(_SPLITS)]).tolist()
(O_H, O_B, O_C, O_Q, O_KCMP, O_VCMP, O_KSEL, O_VSEL, O_KWIN, O_VWIN, O_GATE, O_GCONV, O_GNSA, O_END) = _OFFS

GATE_ROWS = 16
N_STD = 3 * CONV_DIM + 4 * KV_DIM
N_TR = NSA_DIM + 2 * KV_DIM + N_KV_GROUPS * GATE_ROWS

PROJ_TM = 512
ATT_TQ = 256
ATT_TK = 256
TAIL_TM = 512
FF_CHUNK = 1024
VMEM_LIMIT = 56 * 1024 * 1024


def _dot(a, b):
    return jnp.dot(a, b, preferred_element_type=F32)


def _dot_nt(a, b):
    return lax.dot_general(a, b, (((1,), (1,)), ((), ())), preferred_element_type=F32)


def _wprep_kernel(w_ref, wstd_ref, wg_ref, wtr_ref):
    c = 0
    for a, b in ((O_H, O_Q), (O_KCMP, O_KSEL), (O_KSEL, O_VSEL), (O_KWIN, O_VWIN)):
        wstd_ref[:, c:c + b - a] = w_ref[:, a:b].astype(BF16)
        c += b - a
    rest = w_ref[:, O_GATE:O_END]
    wg_ref[...] = rest[:, O_GCONV - O_GATE:].astype(BF16)
    r = 0
    for a, n in ((O_Q, NSA_DIM), (O_VSEL, KV_DIM), (O_VWIN, KV_DIM)):
        for j in range(n // LANES):
            wtr_ref[r:r + LANES, :] = w_ref[:, a + j * LANES:a + (j + 1) * LANES].T.astype(BF16)
            r += LANES
    gates_t = rest[:, 0:LANES].T
    rows = []
    for g in range(N_KV_GROUPS):
        for br in range(N_NSA_BRANCHES):
            for z in range(HEADS_PER_GROUP):
                src = (g * HEADS_PER_GROUP + z) * N_NSA_BRANCHES + br
                rows.append(gates_t[src:src + 1, :])
        rows.append(jnp.zeros((GATE_ROWS - N_NSA_BRANCHES * HEADS_PER_GROUP, gates_t.shape[1]), F32))
    wtr_ref[r:r + N_KV_GROUPS * GATE_ROWS, :] = jnp.concatenate(rows, axis=0).astype(BF16)


def _wprep(w_in):
    d = w_in.shape[0]
    rb = LANES
    assert O_GATE % LANES == 0 and d % rb == 0
    return pl.pallas_call(
        _wprep_kernel,
        grid=(d // rb,),
        in_specs=[pl.BlockSpec((rb, O_END), lambda i: (i, 0))],
        out_specs=(pl.BlockSpec((rb, N_STD), lambda i: (i, 0)),
                   pl.BlockSpec((rb, 2 * D_MODEL), lambda i: (i, 0)),
                   pl.BlockSpec((N_TR, rb), lambda i: (0, i))),
        out_shape=(jax.ShapeDtypeStruct((d, N_STD), BF16), jax.ShapeDtypeStruct((d, 2 * D_MODEL), BF16),
                   jax.ShapeDtypeStruct((N_TR, d), BF16)),
        compiler_params=pltpu.CompilerParams(dimension_semantics=("arbitrary",), vmem_limit_bytes=VMEM_LIMIT),
        name="wprep",
    )(w_in)


def _proj_kernel(x_ref, wstd_ref, wtr_ref, convw_ref, cosk_ref, sink_ref, cosq_ref, sinq_ref,
                 mix_ref, kch_ref, vch_ref, ksel_ref, kwin_ref,
                 qT_ref, qrT_ref, vselT_ref, vwinT_ref, gT_ref, ubuf, kvbuf):
    tm = x_ref.shape[0]

    @pl.when(pl.program_id(1) == 0)
    def _():
        ubuf[...] = jnp.zeros(ubuf.shape, F32)

    xb = x_ref[...].astype(BF16)
    hbc = _dot(xb, wstd_ref[:, 0:3 * CONV_DIM])
    kk = _dot(xb, wstd_ref[:, 3 * CONV_DIM:N_STD])
    yT = _dot_nt(wtr_ref[...], xb)

    u = hbc[:, 2 * CONV_DIM:3 * CONV_DIM] * hbc[:, 0:CONV_DIM]
    prev = ubuf[...]
    ubuf[...] = u[tm - SUBLANES:tm, :]
    head_rows = lax.broadcasted_iota(jnp.int32, (SUBLANES, CONV_DIM), 0)

    def shifted(k):
        r = pltpu.roll(u, k, 0)
        first = jnp.where(head_rows < k, pltpu.roll(prev, k, 0), r[0:SUBLANES, :])
        return jnp.concatenate([first, r[SUBLANES:, :]], axis=0)

    cw = convw_ref[...]
    y = cw[0:1, :] * shifted(2) + cw[1:2, :] * shifted(1) + cw[2:3, :] * u
    mix_ref[...] = (hbc[:, CONV_DIM:2 * CONV_DIM] * y).astype(BF16)

    for j, out in enumerate((kch_ref, vch_ref)):
        kvbuf[j] = kk[:, j * KV_DIM:(j + 1) * KV_DIM]
        for r in range(CMP_STRIDE):
            rows = kvbuf[j, pl.ds(r, tm // CMP_STRIDE, stride=CMP_STRIDE), :]
            out[:, r * KV_DIM:(r + 1) * KV_DIM] = rows.astype(BF16)
    lane = lax.broadcasted_iota(jnp.int32, (tm, KV_DIM), 1)
    first_half = (lane % ROPE_DIM) < ROPE_HALF
    ck = cosk_ref[...]
    sk = sink_ref[...]
    for j, out in ((2, ksel_ref), (3, kwin_ref)):
        k = kk[:, j * KV_DIM:(j + 1) * KV_DIM]
        partner = jnp.where(first_half, pltpu.roll(k, KV_DIM - ROPE_HALF, 1), pltpu.roll(k, ROPE_HALF, 1))
        out[...] = (k * ck + partner * sk).astype(BF16)

    q = yT[0:NSA_DIM, :] * (SCALE * LOG2E)
    qT_ref[...] = q.astype(BF16)
    qrT_ref[...] = q.astype(BF16)
    cq = cosq_ref[...]
    sq = sinq_ref[...]
    for h in range(N_HEADS):
        r0 = h * HEAD_DIM
        x1 = q[r0:r0 + ROPE_HALF, :]
        x2 = q[r0 + ROPE_HALF:r0 + ROPE_DIM, :]
        rot = jnp.concatenate([x1 * cq - x2 * sq, x2 * cq + x1 * sq], axis=0)
        qrT_ref[r0:r0 + ROPE_DIM, :] = rot.astype(BF16)
    tk = vselT_ref.shape[-1]
    ones = jnp.ones((BF16_ROWS, tk), BF16)
    for j, out in enumerate((vselT_ref, vwinT_ref)):
        for i in range(tm // tk):
            for g in range(N_KV_GROUPS):
                r0 = NSA_DIM + j * KV_DIM + g * HEAD_DIM
                out[i, g * V_ROWS:g * V_ROWS + HEAD_DIM, :] = yT[r0:r0 + HEAD_DIM, i * tk:(i + 1) * tk].astype(BF16)
                out[i, g * V_ROWS + HEAD_DIM:(g + 1) * V_ROWS, :] = ones
    gT_ref[...] = jax.nn.sigmoid(yT[NSA_DIM + 2 * KV_DIM:N_TR, :])


def _proj(x2, wstd, wtr, convw, cosk, sink, cosq, sinq, B, S):
    tm, tk = PROJ_TM, ATT_TK
    ns = S // tm
    row = lambda b, s: (b * ns + s, 0)
    const = lambda b, s: (0, 0)
    tok = lambda c, dt: jax.ShapeDtypeStruct((B * S, c), dt)
    cw = CMP_STRIDE * KV_DIM
    chunked = jax.ShapeDtypeStruct((B, S // CMP_STRIDE, cw), BF16)
    chunk_spec = pl.BlockSpec((None, tm // CMP_STRIDE, cw), lambda b, s: (b, s, 0))
    out_shape = (
        tok(CONV_DIM, BF16), chunked, chunked, tok(KV_DIM, BF16), tok(KV_DIM, BF16),
        jax.ShapeDtypeStruct((B, NSA_DIM, S), BF16), jax.ShapeDtypeStruct((B, NSA_DIM, S), BF16),
        jax.ShapeDtypeStruct((B, S // tk, N_KV_GROUPS * V_ROWS, tk), BF16),
        jax.ShapeDtypeStruct((B, S // tk, N_KV_GROUPS * V_ROWS, tk), BF16),
        jax.ShapeDtypeStruct((B, N_KV_GROUPS * GATE_ROWS, S), F32),
    )
    feat = lambda r: pl.BlockSpec((None, r, tm), lambda b, s: (b, 0, s))
    vt = pl.BlockSpec((None, tm // tk, N_KV_GROUPS * V_ROWS, tk), lambda b, s: (b, s, 0, 0))
    return pl.pallas_call(
        _proj_kernel,
        grid=(B, ns),
        in_specs=[
            pl.BlockSpec((tm, D_MODEL), row),
            pl.BlockSpec((D_MODEL, N_STD), const),
            pl.BlockSpec((N_TR, D_MODEL), const),
            pl.BlockSpec((CONV_WIDTH, CONV_DIM), const),
            pl.BlockSpec((tm, KV_DIM), lambda b, s: (s, 0)),
            pl.BlockSpec((tm, KV_DIM), lambda b, s: (s, 0)),
            pl.BlockSpec((ROPE_HALF, tm), lambda b, s: (0, s)),
            pl.BlockSpec((ROPE_HALF, tm), lambda b, s: (0, s)),
        ],
        out_specs=(
            pl.BlockSpec((tm, CONV_DIM), row), chunk_spec, chunk_spec,
            pl.BlockSpec((tm, KV_DIM), row), pl.BlockSpec((tm, KV_DIM), row),
            feat(NSA_DIM), feat(NSA_DIM), vt, vt, feat(N_KV_GROUPS * GATE_ROWS),
        ),
        out_shape=out_shape,
        scratch_shapes=[pltpu.VMEM((SUBLANES, CONV_DIM), F32), pltpu.VMEM((2, tm, KV_DIM), F32)],
        compiler_params=pltpu.CompilerParams(
            dimension_semantics=("arbitrary", "arbitrary"), vmem_limit_bytes=VMEM_LIMIT),
        name="proj",
    )(x2, wstd, wtr, convw, cosk, sink, cosq, sinq)


def _gelu_tanh(x):
    return x * (0.5 * (1.0 + jnp.tanh(np.sqrt(2.0 / np.pi).astype(np.float32) * (x + 0.044715 * (x * x * x)))))


def _compress_kernel(kch_ref, vch_ref, pek_ref, pev_ref, wk1_ref, wv1_ref, wk2_ref, wv2t_ref, kc_ref, vcT_ref,
                     wk1_big, wv1_big):
    @pl.when(pl.program_id(0) == 0)
    def _():
        for w_ref, big in ((wk1_ref, wk1_big), (wv1_ref, wv1_big)):
            big[...] = jnp.zeros(big.shape, BF16)
            for half in range(2):
                for r in range(CMP_STRIDE):
                    for g in range(N_KV_GROUPS):
                        r0 = r * KV_DIM + g * HEAD_DIM
                        big[half, r0:r0 + HEAD_DIM, g * CMP_HIDDEN:(g + 1) * CMP_HIDDEN] = (
                            w_ref[half * CMP_STRIDE + r])

    def hidden(ch_ref, pe_ref, big):
        ch = ch_ref[...]
        a = _dot(ch, big[0])
        b = _dot(ch, big[1])
        pe = _dot(pe_ref[0], big[0]) + _dot(pe_ref[1], big[1])
        b_next = pltpu.roll(b, b.shape[0] - 1, 0)
        return _gelu_tanh(a + b_next + pe[0:1, :]).astype(BF16)

    kc_ref[...] = _dot(hidden(kch_ref, pek_ref, wk1_big), wk2_ref[...]).astype(BF16)
    vcT_ref[...] = _dot_nt(wv2t_ref[...], hidden(vch_ref, pev_ref, wv1_big)).astype(BF16)


def _compress(kch, vch, pek, pev, wk1, wv1, wk2, wv2t, B, ncp):
    cw = CMP_STRIDE * KV_DIM
    gh = N_KV_GROUPS * CMP_HIDDEN
    c3 = lambda b: (0, 0, 0)
    c2 = lambda b: (0, 0)
    return pl.pallas_call(
        _compress_kernel,
        grid=(B,),
        in_specs=[
            pl.BlockSpec((None, ncp, cw), lambda b: (b, 0, 0)),
            pl.BlockSpec((None, ncp, cw), lambda b: (b, 0, 0)),
            pl.BlockSpec((2, SUBLANES, cw), c3), pl.BlockSpec((2, SUBLANES, cw), c3),
            pl.BlockSpec((CMP_BLOCK, HEAD_DIM, CMP_HIDDEN), c3), pl.BlockSpec((CMP_BLOCK, HEAD_DIM, CMP_HIDDEN), c3),
            pl.BlockSpec((gh, KV_DIM), c2), pl.BlockSpec((KV_DIM, gh), c2),
        ],
        out_specs=(pl.BlockSpec((None, ncp, KV_DIM), lambda b: (b, 0, 0)),
                   pl.BlockSpec((None, KV_DIM, ncp), lambda b: (b, 0, 0))),
        out_shape=(jax.ShapeDtypeStruct((B, ncp, KV_DIM), BF16), jax.ShapeDtypeStruct((B, KV_DIM, ncp), BF16)),
        scratch_shapes=[pltpu.VMEM((2, cw, gh), BF16), pltpu.VMEM((2, cw, gh), BF16)],
        compiler_params=pltpu.CompilerParams(dimension_semantics=("arbitrary",), vmem_limit_bytes=VMEM_LIMIT),
        name="compress",
    )(kch, vch, pek, pev, wk1, wv1, wk2, wv2t)


def _attn_kernel(qT_ref, qrT_ref, kc_ref, vcT_ref, ksel_ref, vselT_ref, kwin_ref, vwinT_ref, gT_ref, mT_ref,
                 wbias_ref, o_ref, selb_scr, q_scr, qr_scr, s_scr, swin_scr, m_scr, acc_scr, oT_scr):
    tq = qT_ref.shape[1]
    tk = ksel_ref.shape[1]
    nsub = tq // LANES
    nchunk = N_HEADS * nsub
    gchunk = HEADS_PER_GROUP * nsub
    gl = HEADS_PER_GROUP * tq
    ns = mT_ref.shape[0]
    bpt = tk // SEL_BLOCK
    p = pl.program_id(1)
    qbase = p * tq

    def chunk(c):
        return slice(c * LANES, (c + 1) * LANES)

    def sub(tsub):
        return slice(tsub * LANES, (tsub + 1) * LANES)

    def stage_q(ref, scr):
        zeros = jnp.zeros((HEAD_DIM, tq), BF16)
        for h in range(N_HEADS):
            qh = ref[h * HEAD_DIM:(h + 1) * HEAD_DIM, :]
            halves = [qh, zeros] if h // HEADS_PER_GROUP == 0 else [zeros, qh]
            scr[:, h * tq:(h + 1) * tq] = jnp.concatenate(halves, axis=0)

    gt = gT_ref[...]

    def gate(br, c):
        h, tsub = divmod(c, nsub)
        g, z = divmod(h, HEADS_PER_GROUP)
        r = g * GATE_ROWS + br * HEADS_PER_GROUP + z
        return gt[r:r + 1, sub(tsub)]

    stage_q(qT_ref, q_scr)
    stage_q(qrT_ref, qr_scr)
    start = p % 2
    n_back = WINDOW // tk
    case = jnp.minimum(p, n_back)
    lo = p - case

    def qk_sel(kt, slot):
        s_scr[slot] = _dot(ksel_ref[kt], qr_scr[...])

    kc = kc_ref[...]
    mT = mT_ref[...]
    ncp = kc.shape[0]
    s_cmp = _dot(kc, q_scr[...])
    k_win = kwin_ref[pl.ds(lo, n_back + 1)].reshape((n_back + 1) * tk, KV_DIM)
    swin_scr[...] = _dot(k_win, qr_scr[...])
    qk_sel(start, 0)

    imp = []
    for g in range(N_KV_GROUPS):
        vcT = vcT_ref[g * HEAD_DIM:(g + 1) * HEAD_DIM, :]
        imp_parts = []
        for tsub in range(nsub):
            n_idx = lax.broadcasted_iota(jnp.int32, (ncp, LANES), 0)
            t_idx = qbase + tsub * LANES + lax.broadcasted_iota(jnp.int32, (ncp, LANES), 1)
            ok = (n_idx * CMP_STRIDE + (CMP_BLOCK - 1)) <= t_idx
            bias = jnp.where(ok, 0.0, NEG).astype(F32)
            keep = ok.astype(F32)
            p_sum = None
            for z in range(HEADS_PER_GROUP):
                c = (g * HEADS_PER_GROUP + z) * nsub + tsub
                s = s_cmp[:, chunk(c)] + bias
                e = jnp.exp2(s - jnp.max(s, axis=0, keepdims=True)) * keep
                inv = 1.0 / jnp.maximum(jnp.sum(e, axis=0, keepdims=True), 1e-30)
                oT_scr[:, chunk(c)] = _dot(vcT, e.astype(BF16)) * (inv * gate(0, c))
                p_sum = e * inv if p_sum is None else p_sum + e * inv
            imp_parts.append(jnp.dot(mT, p_sum, preferred_element_type=F32, precision=lax.Precision.HIGHEST))
        imp.append(jnp.concatenate(imp_parts, axis=1))

    j_idx = lax.broadcasted_iota(jnp.int32, (ns, tq), 0)
    cur = (qbase + lax.broadcasted_iota(jnp.int32, (ns, tq), 1)) // SEL_BLOCK
    forced = (j_idx == 0) | (j_idx == cur) | (j_idx == cur - 1)
    for g in range(N_KV_GROUPS):
        impv = jnp.where(j_idx <= cur, jnp.where(forced, FORCE, imp[g]), NEG)
        cnt = jnp.zeros((ns, tq), F32)
        for jp in range(ns):
            row = impv[jp:jp + 1, :]
            beats = (row > impv) | ((row == impv) & (j_idx > jp))
            cnt = cnt + beats.astype(F32)
        sel = (cnt < float(min(N_SEL, ns))) & (impv > 0.5 * NEG)
        selb = jnp.where(sel, 0.0, NEG).astype(F32)
        for kt in range(ns // bpt):
            selb_scr[g, kt, 0:bpt, :] = selb[kt * bpt:(kt + 1) * bpt, :]

    SEL, WIN = 0, 1
    m_scr[...] = jnp.full(m_scr.shape, NEG, F32)
    acc_scr[...] = jnp.zeros(acc_scr.shape, F32)

    def softmax_pv(st, load_s, vT_t, block_bias, tile_bias):
        for g in range(N_KV_GROUPS):
            es, alphas = [], []
            for cg in range(gchunk):
                c = g * gchunk + cg
                tsub = c % nsub
                s = load_s(c)
                if block_bias is not None:
                    rb = block_bias[g][:, sub(tsub)]
                    s = s + jnp.concatenate(
                        [jnp.broadcast_to(rb[i:i + 1, :], (SEL_BLOCK, LANES)) for i in range(bpt)], axis=0)
                if tile_bias is not None:
                    s = s + tile_bias(tsub)
                m_old = m_scr[st, :, chunk(c)]
                m_new = jnp.maximum(m_old, jnp.max(s, axis=0, keepdims=True))
                es.append(jnp.exp2(s - m_new).astype(BF16))
                alphas.append(jnp.exp2(m_old - m_new))
                m_scr[st, :, chunk(c)] = m_new
            lanes = slice(g * gl, (g + 1) * gl)
            pv = _dot(vT_t[g * V_ROWS:(g + 1) * V_ROWS, :], jnp.concatenate(es, axis=1))
            acc_scr[st, :, lanes] = jnp.concatenate(alphas, axis=1) * acc_scr[st, :, lanes] + pv

    def finish(st, br):
        for c in range(nchunk):
            acc = acc_scr[st, :, chunk(c)]
            inv = 1.0 / jnp.maximum(acc[HEAD_DIM:HEAD_DIM + 1, :], 1e-30)
            oT_scr[:, chunk(c)] += acc[0:HEAD_DIM, :] * (inv * gate(br, c))

    for i in range(n_back, -1, -1):
        rows = slice(i * tk, (i + 1) * tk)
        softmax_pv(WIN, lambda c: swin_scr[rows, chunk(c)], vwinT_ref[lo + i], None,
                   lambda tsub: wbias_ref[case, rows, sub(tsub)])
    finish(WIN, 2)

    def sp_sel(kt, slot, tile_bias):
        softmax_pv(SEL, lambda c: s_scr[slot, :, chunk(c)], vselT_ref[kt],
                   [selb_scr[g, kt, 0:bpt, :] for g in range(N_KV_GROUPS)], tile_bias)

    @pl.when(start == 1)
    def _():
        qk_sel(0, 1)
        sp_sel(0, 1, None)

    def pair_body(j, carry):
        a = start + 2 * j
        qk_sel(a + 1, 1)
        sp_sel(a, 0, None)
        qk_sel(a + 2, 0)
        sp_sel(a + 1, 1, None)
        return carry

    lax.fori_loop(0, p // 2, pair_body, 0)
    sp_sel(p, 0, lambda tsub: wbias_ref[n_back, n_back * tk:(n_back + 1) * tk, sub(tsub)])
    finish(SEL, 1)

    for tsub in range(nsub):
        for i in range(N_HEADS // 2):
            pair = jnp.concatenate([oT_scr[:, chunk((2 * i) * nsub + tsub)],
                                    oT_scr[:, chunk((2 * i + 1) * nsub + tsub)]], axis=0)
            o_ref[sub(tsub), i * 2 * HEAD_DIM:(i + 1) * 2 * HEAD_DIM] = pair.T.astype(BF16)


def _window_bias(t, n_back):
    r = np.arange(t)[:, None]
    c = np.arange(t)[None, :]
    causal = np.where(r <= c, 0.0, NEG)
    edge = np.where(r > c, 0.0, NEG)
    out = np.zeros((n_back + 1, (n_back + 1) * t, t), np.float32)
    for case in range(n_back + 1):
        for i in range(n_back + 1):
            back = case - i
            tile = NEG if back < 0 else causal if back == 0 else edge if back == n_back else 0.0
            out[case, i * t:(i + 1) * t, :] = tile
    return jnp.asarray(out)


def _attn(qT, qrT, kc, vcT, ksel, vselT, kwin, vwinT, gT, mT, B, S):
    tq, tk = ATT_TQ, ATT_TK
    assert tq == tk and WINDOW % tk == 0 and tk % SEL_BLOCK == 0 and tq % LANES == 0
    assert tk // SEL_BLOCK <= SUBLANES
    n_back = WINDOW // tk
    assert S // tk > n_back
    P = S // tq
    ncp = kc.shape[1]
    ns = S // SEL_BLOCK
    nq = N_HEADS * tq
    qspec = pl.BlockSpec((None, NSA_DIM, tq), lambda b, p: (b, 0, p))
    kspec = pl.BlockSpec((None, S // tk, tk, KV_DIM), lambda b, p: (b, 0, 0, 0))
    vspec = pl.BlockSpec((None, S // tk, N_KV_GROUPS * V_ROWS, tk), lambda b, p: (b, 0, 0, 0))
    return pl.pallas_call(
        _attn_kernel,
        grid=(B, P),
        in_specs=[
            qspec, qspec,
            pl.BlockSpec((None, ncp, KV_DIM), lambda b, p: (b, 0, 0)),
            pl.BlockSpec((None, KV_DIM, ncp), lambda b, p: (b, 0, 0)),
            kspec, vspec, kspec, vspec,
            pl.BlockSpec((None, N_KV_GROUPS * GATE_ROWS, tq), lambda b, p: (b, 0, p)),
            pl.BlockSpec((ns, ncp), lambda b, p: (0, 0)),
            pl.BlockSpec((n_back + 1, (n_back + 1) * tk, tq), lambda b, p: (0, 0, 0)),
        ],
        out_specs=pl.BlockSpec((tq, NSA_DIM), lambda b, p: (b * P + p, 0)),
        out_shape=jax.ShapeDtypeStruct((B * S, NSA_DIM), BF16),
        scratch_shapes=[
            pltpu.VMEM((N_KV_GROUPS, S // tk, SUBLANES, tq), F32),
            pltpu.VMEM((KV_DIM, nq), BF16),
            pltpu.VMEM((KV_DIM, nq), BF16),
            pltpu.VMEM((2, tk, nq), F32),
            pltpu.VMEM(((n_back + 1) * tk, nq), F32),
            pltpu.VMEM((2, 1, nq), F32),
            pltpu.VMEM((2, V_ROWS, nq), F32),
            pltpu.VMEM((HEAD_DIM, nq), F32),
        ],
        compiler_params=pltpu.CompilerParams(
            dimension_semantics=("arbitrary", "arbitrary"), vmem_limit_bytes=VMEM_LIMIT),
        name="attn",
    )(qT, qrT, kc, vcT, ksel, vselT, kwin, vwinT, gT, mT, _window_bias(tk, n_back))


def _layer_norm(z, g, b):
    mu = jnp.mean(z, axis=-1, keepdims=True)
    zc = z - mu
    var = jnp.mean(zc * zc, axis=-1, keepdims=True)
    return zc * lax.rsqrt(var + LN_EPS) * g + b


def _tail_kernel(alpha, x_ref, mix_ref, o_ref, wg_ref, wco_ref, wno_ref, wo_ref, g1_ref, b1_ref,
                 wup_ref, wdn_ref, g2_ref, b2_ref, out_ref):
    x = x_ref[...]
    xb = x.astype(BF16)
    y_conv = _dot(mix_ref[...], wco_ref[...])
    y_nsa = _dot(o_ref[...], wno_ref[...])
    mixed = (jax.nn.sigmoid(_dot(xb, wg_ref[:, 0:D_MODEL])) * y_conv
             + jax.nn.sigmoid(_dot(xb, wg_ref[:, D_MODEL:2 * D_MODEL])) * y_nsa)
    x1 = _layer_norm(alpha * x + _dot(mixed.astype(BF16), wo_ref[...]), g1_ref[...], b1_ref[...])
    x1b = x1.astype(BF16)
    ff = jnp.zeros_like(x1)
    for c in range(D_FF // FF_CHUNK):
        h = jnp.maximum(_dot(x1b, wup_ref[:, c * FF_CHUNK:(c + 1) * FF_CHUNK]), 0.0)
        ff = ff + _dot((h * h).astype(BF16), wdn_ref[c * FF_CHUNK:(c + 1) * FF_CHUNK, :])
    out_ref[...] = _layer_norm(alpha * x1 + ff, g2_ref[...], b2_ref[...])


def _tail(x2, mix, o, wg, wco, wno, wo, g1, b1, wup, wdn, g2, b2, alpha):
    n = x2.shape[0]
    tm = TAIL_TM
    row = lambda i: (i, 0)
    const = lambda i: (0, 0)
    resident = lambda shape: pl.BlockSpec(shape, const, pipeline_mode=pl.Buffered(1))
    return pl.pallas_call(
        functools.partial(_tail_kernel, alpha),
        grid=(n // tm,),
        in_specs=[
            pl.BlockSpec((tm, D_MODEL), row), pl.BlockSpec((tm, CONV_DIM), row), pl.BlockSpec((tm, NSA_DIM), row),
            resident((D_MODEL, 2 * D_MODEL)), resident((CONV_DIM, D_MODEL)), resident((NSA_DIM, D_MODEL)),
            resident((D_MODEL, D_MODEL)), resident((1, D_MODEL)), resident((1, D_MODEL)),
            resident((D_MODEL, D_FF)), resident((D_FF, D_MODEL)), resident((1, D_MODEL)), resident((1, D_MODEL)),
        ],
        out_specs=pl.BlockSpec((tm, D_MODEL), row),
        out_shape=jax.ShapeDtypeStruct((n, D_MODEL), F32),
        compiler_params=pltpu.CompilerParams(dimension_semantics=("arbitrary",), vmem_limit_bytes=VMEM_LIMIT),
        name="tail",
    )(x2, mix, o, wg, wco, wno, wo, g1, b1, wup, wdn, g2, b2)


def _rope_tables(S):
    inv = ROPE_THETA ** (-jnp.arange(0, ROPE_DIM, 2, dtype=F32) / ROPE_DIM)
    ang = jnp.arange(S, dtype=F32)[:, None] * inv[None, :]
    cos, sin = jnp.cos(ang), jnp.sin(ang)
    pad_c = jnp.ones((S, HEAD_DIM - ROPE_DIM), F32)
    pad_s = jnp.zeros((S, HEAD_DIM - ROPE_DIM), F32)
    ck = jnp.tile(jnp.concatenate([cos, cos, pad_c], axis=1), (1, N_KV_GROUPS))
    sk = jnp.tile(jnp.concatenate([-sin, sin, pad_s], axis=1), (1, N_KV_GROUPS))
    return ck, sk, cos.T, sin.T


def _overlap_matrix_t(ncp, ns):
    nc = ncp - 1
    i = np.arange(ncp)[None, :]
    j = np.arange(ns)[:, None]
    m = (i * CMP_STRIDE < (j + 1) * SEL_BLOCK) & (i * CMP_STRIDE + CMP_BLOCK > j * SEL_BLOCK) & (i < nc)
    return jnp.asarray(m.astype(np.float32))


def _compress_weights(pe, w1, w2):
    eye = jnp.eye(N_KV_GROUPS, dtype=F32)
    peh = pe.reshape(2, CMP_STRIDE, 1, HEAD_DIM)
    pebig = jnp.broadcast_to(peh, (2, CMP_STRIDE, N_KV_GROUPS, HEAD_DIM)).reshape(2, 1, CMP_STRIDE * KV_DIM)
    pebig = jnp.broadcast_to(pebig, (2, SUBLANES, CMP_STRIDE * KV_DIM))
    w2big = jnp.einsum('ab,hd->ahbd', eye, w2).reshape(N_KV_GROUPS * CMP_HIDDEN, KV_DIM)
    return pebig.astype(BF16), w1.astype(BF16), w2big.astype(BF16)


def _layer(x2, B, S, w_in, conv_w, w_conv_out, pe_k, wk1, wk2, pe_v, wv1, wv2, w_nsa_out, w_o,
           ln1_g, ln1_b, w_up, w_down, ln2_g, ln2_b, alpha):
    ncp = S // CMP_STRIDE
    ns = S // SEL_BLOCK
    wstd, wg, wtr = _wprep(w_in)
    ck, sk, cq, sq = _rope_tables(S)

    mix, kch, vch, ksel, kwin, qT, qrT, vselT, vwinT, gT = _proj(
        x2, wstd, wtr, conv_w.reshape(CONV_WIDTH, CONV_DIM), ck, sk, cq, sq, B, S)

    pek, wk1b, wk2b = _compress_weights(pe_k, wk1, wk2)
    pev, wv1b, wv2b = _compress_weights(pe_v, wv1, wv2)
    kc, vcT = _compress(kch, vch, pek, pev, wk1b, wv1b, wk2b, wv2b.T, B, ncp)

    tk = ATT_TK
    o = _attn(qT, qrT, kc, vcT, ksel.reshape(B, S // tk, tk, KV_DIM), vselT,
              kwin.reshape(B, S // tk, tk, KV_DIM), vwinT, gT, _overlap_matrix_t(ncp, ns), B, S)

    row = lambda v: v.reshape(1, D_MODEL).astype(F32)
    return _tail(x2, mix, o, wg, w_conv_out.astype(BF16), w_nsa_out.astype(BF16), w_o.astype(BF16),
                 row(ln1_g), row(ln1_b), w_up.astype(BF16), w_down.astype(BF16), row(ln2_g), row(ln2_b), alpha)


def kernel(x, w_in, conv_w, w_conv_out, pe_k_cmp, w_k_cmp1, w_k_cmp2, pe_v_cmp, w_v_cmp1, w_v_cmp2,
           w_nsa_out, w_o, ln1_g, ln1_b, w_up, w_down, ln2_g, ln2_b):
    B, S, D = x.shape
    depth = w_in.shape[0]
    assert D == D_MODEL and w_in.shape[2] == O_END
    assert S % PROJ_TM == 0 and S % ATT_TQ == 0 and (B * S) % TAIL_TM == 0 and S >= WINDOW
    alpha = float((2 * depth) ** 0.25)
    x2 = x.reshape(B * S, D)
    for l in range(depth):
        x2 = _layer(x2, B, S, w_in[l], conv_w[l], w_conv_out[l], pe_k_cmp[l], w_k_cmp1[l], w_k_cmp2[l],
                    pe_v_cmp[l], w_v_cmp1[l], w_v_cmp2[l], w_nsa_out[l], w_o[l], ln1_g[l], ln1_b[l],
                    w_up[l], w_down[l], ln2_g[l], ln2_b[l], alpha)
    return x2.reshape(B, S, D)
```

```python
import functools

import numpy as np
import jax
import jax.numpy as jnp
from jax import lax
from jax.experimental import pallas as pl
from jax.experimental.pallas import tpu as pltpu

F32 = jnp.float32
BF16 = jnp.bfloat16

D_MODEL = 1024
CONV_DIM = D_MODEL // 2
CONV_WIDTH = 3
N_HEADS = 8
HEAD_DIM = 64
N_KV_GROUPS = 2
HEADS_PER_GROUP = N_HEADS // N_KV_GROUPS
NSA_DIM = N_HEADS * HEAD_DIM
KV_DIM = N_KV_GROUPS * HEAD_DIM
ROPE_DIM = HEAD_DIM // 4
ROPE_HALF = ROPE_DIM // 2
ROPE_THETA = 500000.0
CMP_BLOCK = 32
CMP_STRIDE = 16
CMP_HIDDEN = 2 * HEAD_DIM
SEL_BLOCK = 64
N_SEL = 16
WINDOW = 512
N_NSA_BRANCHES = 3
D_FF = 4 * D_MODEL
LN_EPS = 1e-5
NEG = -1e30
FORCE = 1e9
SCALE = HEAD_DIM ** -0.5
LOG2E = 1.4426950408889634

SUBLANES = 8
LANES = 128
BF16_ROWS = 16
V_ROWS = HEAD_DIM + BF16_ROWS

_SPLITS = (CONV_DIM, CONV_DIM, CONV_DIM, NSA_DIM, KV_DIM, KV_DIM, KV_DIM, KV_DIM, KV_DIM, KV_DIM,
           N_HEADS * N_NSA_BRANCHES, D_MODEL, D_MODEL)
_OFFS = np.concatenate([[0], np.cumsum(_SPLITS)]).tolist()
(O_H, O_B, O_C, O_Q, O_KCMP, O_VCMP, O_KSEL, O_VSEL, O_KWIN, O_VWIN, O_GATE, O_GCONV, O_GNSA, O_END) = _OFFS

GATE_ROWS = 16
N_STD = 3 * CONV_DIM + 4 * KV_DIM
N_TR = NSA_DIM + 2 * KV_DIM + N_KV_GROUPS * GATE_ROWS

PROJ_TM = 512
ATT_TQ = 256
ATT_TK = 256
TAIL_TM = 512
FF_CHUNK = 1024
VMEM_LIMIT = 56 * 1024 * 1024


def _dot(a, b):
    return jnp.dot(a, b, preferred_element_type=F32)


def _dot_nt(a, b):
    return lax.dot_general(a, b, (((1,), (1,)), ((), ())), preferred_element_type=F32)


WPREP_COLS = 256


def _wprep_kernel(wT_ref, wstdT_ref, wgT_ref, wtr_ref):
    r = 0
    for a, b in ((O_H, O_Q), (O_KCMP, O_KSEL), (O_KSEL, O_VSEL), (O_KWIN, O_VWIN)):
        wstdT_ref[r:r + b - a, :] = wT_ref[a:b, :].astype(BF16)
        r += b - a
    wgT_ref[...] = wT_ref[O_GCONV:O_END, :].astype(BF16)
    r = 0
    for a, b in ((O_Q, O_KCMP), (O_VSEL, O_KWIN), (O_VWIN, O_GATE)):
        wtr_ref[r:r + b - a, :] = wT_ref[a:b, :].astype(BF16)
        r += b - a
    gates = wT_ref[O_GATE:O_GCONV, :]
    rows = []
    for g in range(N_KV_GROUPS):
        for br in range(N_NSA_BRANCHES):
            for z in range(HEADS_PER_GROUP):
                src = (g * HEADS_PER_GROUP + z) * N_NSA_BRANCHES + br
                rows.append(gates[src:src + 1, :])
        rows.append(jnp.zeros((GATE_ROWS - N_NSA_BRANCHES * HEADS_PER_GROUP, gates.shape[1]), F32))
    wtr_ref[r:r + N_KV_GROUPS * GATE_ROWS, :] = jnp.concatenate(rows, axis=0).astype(BF16)


def _wprep(wT):
    d = wT.shape[1]
    cb = WPREP_COLS
    assert d % cb == 0 and O_GCONV % SUBLANES == 0
    strip = lambda rows: pl.BlockSpec((rows, cb), lambda i: (0, i))
    return pl.pallas_call(
        _wprep_kernel,
        grid=(d // cb,),
        in_specs=[strip(O_END)],
        out_specs=(strip(N_STD), strip(2 * D_MODEL), strip(N_TR)),
        out_shape=(jax.ShapeDtypeStruct((N_STD, d), BF16), jax.ShapeDtypeStruct((2 * D_MODEL, d), BF16),
                   jax.ShapeDtypeStruct((N_TR, d), BF16)),
        compiler_params=pltpu.CompilerParams(dimension_semantics=("arbitrary",), vmem_limit_bytes=VMEM_LIMIT),
        name="wprep",
    )(wT)


def _proj_kernel(x_ref, wstd_ref, wtr_ref, convw_ref, cosk_ref, sink_ref, cosq_ref, sinq_ref,
                 mix_ref, kch_ref, vch_ref, ksel_ref, kwin_ref,
                 qT_ref, qrT_ref, vselT_ref, vwinT_ref, gT_ref, ubuf, kvbuf):
    tm = x_ref.shape[0]

    @pl.when(pl.program_id(1) == 0)
    def _():
        ubuf[...] = jnp.zeros(ubuf.shape, F32)

    xb = x_ref[...].astype(BF16)
    hbc = _dot_nt(xb, wstd_ref[0:3 * CONV_DIM, :])
    kk = _dot_nt(xb, wstd_ref[3 * CONV_DIM:N_STD, :])
    yT = _dot_nt(wtr_ref[...], xb)

    u = hbc[:, 2 * CONV_DIM:3 * CONV_DIM] * hbc[:, 0:CONV_DIM]
    prev = ubuf[...]
    ubuf[...] = u[tm - SUBLANES:tm, :]
    head_rows = lax.broadcasted_iota(jnp.int32, (SUBLANES, CONV_DIM), 0)

    def shifted(k):
        r = pltpu.roll(u, k, 0)
        first = jnp.where(head_rows < k, pltpu.roll(prev, k, 0), r[0:SUBLANES, :])
        return jnp.concatenate([first, r[SUBLANES:, :]], axis=0)

    cw = convw_ref[...]
    y = cw[0:1, :] * shifted(2) + cw[1:2, :] * shifted(1) + cw[2:3, :] * u
    mix_ref[...] = (hbc[:, CONV_DIM:2 * CONV_DIM] * y).astype(BF16)

    for j, out in enumerate((kch_ref, vch_ref)):
        kvbuf[j] = kk[:, j * KV_DIM:(j + 1) * KV_DIM]
        for r in range(CMP_STRIDE):
            rows = kvbuf[j, pl.ds(r, tm // CMP_STRIDE, stride=CMP_STRIDE), :]
            out[:, r * KV_DIM:(r + 1) * KV_DIM] = rows.astype(BF16)
    lane = lax.broadcasted_iota(jnp.int32, (tm, KV_DIM), 1)
    first_half = (lane % ROPE_DIM) < ROPE_HALF
    ck = cosk_ref[...]
    sk = sink_ref[...]
    for j, out in ((2, ksel_ref), (3, kwin_ref)):
        k = kk[:, j * KV_DIM:(j + 1) * KV_DIM]
        partner = jnp.where(first_half, pltpu.roll(k, KV_DIM - ROPE_HALF, 1), pltpu.roll(k, ROPE_HALF, 1))
        out[...] = (k * ck + partner * sk).astype(BF16)

    q = yT[0:NSA_DIM, :] * (SCALE * LOG2E)
    qT_ref[...] = q.astype(BF16)
    qrT_ref[...] = q.astype(BF16)
    cq = cosq_ref[...]
    sq = sinq_ref[...]
    for h in range(N_HEADS):
        r0 = h * HEAD_DIM
        x1 = q[r0:r0 + ROPE_HALF, :]
        x2 = q[r0 + ROPE_HALF:r0 + ROPE_DIM, :]
        rot = jnp.concatenate([x1 * cq - x2 * sq, x2 * cq + x1 * sq], axis=0)
        qrT_ref[r0:r0 + ROPE_DIM, :] = rot.astype(BF16)
    tk = vselT_ref.shape[-1]
    ones = jnp.ones((BF16_ROWS, tk), BF16)
    for j, out in enumerate((vselT_ref, vwinT_ref)):
        for i in range(tm // tk):
            for g in range(N_KV_GROUPS):
                r0 = NSA_DIM + j * KV_DIM + g * HEAD_DIM
                out[i, g * V_ROWS:g * V_ROWS + HEAD_DIM, :] = yT[r0:r0 + HEAD_DIM, i * tk:(i + 1) * tk].astype(BF16)
                out[i, g * V_ROWS + HEAD_DIM:(g + 1) * V_ROWS, :] = ones
    gT_ref[...] = jax.nn.sigmoid(yT[NSA_DIM + 2 * KV_DIM:N_TR, :])


def _proj(x2, wstd, wtr, convw, cosk, sink, cosq, sinq, B, S):
    tm, tk = PROJ_TM, ATT_TK
    ns = S // tm
    row = lambda b, s: (b * ns + s, 0)
    const = lambda b, s: (0, 0)
    tok = lambda c, dt: jax.ShapeDtypeStruct((B * S, c), dt)
    cw = CMP_STRIDE * KV_DIM
    chunked = jax.ShapeDtypeStruct((B, S // CMP_STRIDE, cw), BF16)
    chunk_spec = pl.BlockSpec((None, tm // CMP_STRIDE, cw), lambda b, s: (b, s, 0))
    out_shape = (
        tok(CONV_DIM, BF16), chunked, chunked, tok(KV_DIM, BF16), tok(KV_DIM, BF16),
        jax.ShapeDtypeStruct((B, NSA_DIM, S), BF16), jax.ShapeDtypeStruct((B, NSA_DIM, S), BF16),
        jax.ShapeDtypeStruct((B, S // tk, N_KV_GROUPS * V_ROWS, tk), BF16),
        jax.ShapeDtypeStruct((B, S // tk, N_KV_GROUPS * V_ROWS, tk), BF16),
        jax.ShapeDtypeStruct((B, N_KV_GROUPS * GATE_ROWS, S), F32),
    )
    feat = lambda r: pl.BlockSpec((None, r, tm), lambda b, s: (b, 0, s))
    vt = pl.BlockSpec((None, tm // tk, N_KV_GROUPS * V_ROWS, tk), lambda b, s: (b, s, 0, 0))
    return pl.pallas_call(
        _proj_kernel,
        grid=(B, ns),
        in_specs=[
            pl.BlockSpec((tm, D_MODEL), row),
            pl.BlockSpec((N_STD, D_MODEL), const),
            pl.BlockSpec((N_TR, D_MODEL), const),
            pl.BlockSpec((CONV_WIDTH, CONV_DIM), const),
            pl.BlockSpec((tm, KV_DIM), lambda b, s: (s, 0)),
            pl.BlockSpec((tm, KV_DIM), lambda b, s: (s, 0)),
            pl.BlockSpec((ROPE_HALF, tm), lambda b, s: (0, s)),
            pl.BlockSpec((ROPE_HALF, tm), lambda b, s: (0, s)),
        ],
        out_specs=(
            pl.BlockSpec((tm, CONV_DIM), row), chunk_spec, chunk_spec,
            pl.BlockSpec((tm, KV_DIM), row), pl.BlockSpec((tm, KV_DIM), row),
            feat(NSA_DIM), feat(NSA_DIM), vt, vt, feat(N_KV_GROUPS * GATE_ROWS),
        ),
        out_shape=out_shape,
        scratch_shapes=[pltpu.VMEM((SUBLANES, CONV_DIM), F32), pltpu.VMEM((2, tm, KV_DIM), F32)],
        compiler_params=pltpu.CompilerParams(
            dimension_semantics=("arbitrary", "arbitrary"), vmem_limit_bytes=VMEM_LIMIT),
        name="proj",
    )(x2, wstd, wtr, convw, cosk, sink, cosq, sinq)


def _gelu_tanh(x):
    return x * (0.5 * (1.0 + jnp.tanh(np.sqrt(2.0 / np.pi).astype(np.float32) * (x + 0.044715 * (x * x * x)))))


def _compress_kernel(kch_ref, vch_ref, pek_ref, pev_ref, wk1_ref, wv1_ref, wk2_ref, wv2t_ref, kc_ref, vcT_ref,
                     wk1_big, wv1_big):
    @pl.when(pl.program_id(0) == 0)
    def _():
        for w_ref, big in ((wk1_ref, wk1_big), (wv1_ref, wv1_big)):
            big[...] = jnp.zeros(big.shape, BF16)
            for half in range(2):
                for r in range(CMP_STRIDE):
                    for g in range(N_KV_GROUPS):
                        r0 = r * KV_DIM + g * HEAD_DIM
                        big[half, r0:r0 + HEAD_DIM, g * CMP_HIDDEN:(g + 1) * CMP_HIDDEN] = (
                            w_ref[half * CMP_STRIDE + r])

    def hidden(ch_ref, pe_ref, big):
        ch = ch_ref[...]
        a = _dot(ch, big[0])
        b = _dot(ch, big[1])
        pe = _dot(pe_ref[0], big[0]) + _dot(pe_ref[1], big[1])
        b_next = pltpu.roll(b, b.shape[0] - 1, 0)
        return _gelu_tanh(a + b_next + pe[0:1, :]).astype(BF16)

    kc_ref[...] = _dot(hidden(kch_ref, pek_ref, wk1_big), wk2_ref[...]).astype(BF16)
    vcT_ref[...] = _dot_nt(wv2t_ref[...], hidden(vch_ref, pev_ref, wv1_big)).astype(BF16)


def _compress(kch, vch, pek, pev, wk1, wv1, wk2, wv2t, B, ncp):
    cw = CMP_STRIDE * KV_DIM
    gh = N_KV_GROUPS * CMP_HIDDEN
    c3 = lambda b: (0, 0, 0)
    c2 = lambda b: (0, 0)
    return pl.pallas_call(
        _compress_kernel,
        grid=(B,),
        in_specs=[
            pl.BlockSpec((None, ncp, cw), lambda b: (b, 0, 0)),
            pl.BlockSpec((None, ncp, cw), lambda b: (b, 0, 0)),
            pl.BlockSpec((2, SUBLANES, cw), c3), pl.BlockSpec((2, SUBLANES, cw), c3),
            pl.BlockSpec((CMP_BLOCK, HEAD_DIM, CMP_HIDDEN), c3), pl.BlockSpec((CMP_BLOCK, HEAD_DIM, CMP_HIDDEN), c3),
            pl.BlockSpec((gh, KV_DIM), c2), pl.BlockSpec((KV_DIM, gh), c2),
        ],
        out_specs=(pl.BlockSpec((None, ncp, KV_DIM), lambda b: (b, 0, 0)),
                   pl.BlockSpec((None, KV_DIM, ncp), lambda b: (b, 0, 0))),
        out_shape=(jax.ShapeDtypeStruct((B, ncp, KV_DIM), BF16), jax.ShapeDtypeStruct((B, KV_DIM, ncp), BF16)),
        scratch_shapes=[pltpu.VMEM((2, cw, gh), BF16), pltpu.VMEM((2, cw, gh), BF16)],
        compiler_params=pltpu.CompilerParams(dimension_semantics=("arbitrary",), vmem_limit_bytes=VMEM_LIMIT),
        name="compress",
    )(kch, vch, pek, pev, wk1, wv1, wk2, wv2t)


def _attn_kernel(qT_ref, qrT_ref, kc_ref, vcT_ref, ksel_ref, vselT_ref, kwin_ref, vwinT_ref, gT_ref, mT_ref,
                 wbias_ref, o_ref, selb_scr, q_scr, qr_scr, s_scr, swin_scr, m_scr, acc_scr, oT_scr):
    tq = qT_ref.shape[1]
    tk = ksel_ref.shape[1]
    nsub = tq // LANES
    nchunk = N_HEADS * nsub
    gchunk = HEADS_PER_GROUP * nsub
    gl = HEADS_PER_GROUP * tq
    ns = mT_ref.shape[0]
    bpt = tk // SEL_BLOCK
    p = pl.program_id(1)
    qbase = p * tq

    def chunk(c):
        return slice(c * LANES, (c + 1) * LANES)

    def sub(tsub):
        return slice(tsub * LANES, (tsub + 1) * LANES)

    def stage_q(ref, scr):
        zeros = jnp.zeros((HEAD_DIM, tq), BF16)
        for h in range(N_HEADS):
            qh = ref[h * HEAD_DIM:(h + 1) * HEAD_DIM, :]
            halves = [qh, zeros] if h // HEADS_PER_GROUP == 0 else [zeros, qh]
            scr[:, h * tq:(h + 1) * tq] = jnp.concatenate(halves, axis=0)

    gt = gT_ref[...]

    def gate(br, c):
        h, tsub = divmod(c, nsub)
        g, z = divmod(h, HEADS_PER_GROUP)
        r = g * GATE_ROWS + br * HEADS_PER_GROUP + z
        return gt[r:r + 1, sub(tsub)]

    stage_q(qT_ref, q_scr)
    stage_q(qrT_ref, qr_scr)
    start = p % 2
    n_back = WINDOW // tk
    case = jnp.minimum(p, n_back)
    lo = p - case

    def qk_sel(kt, slot):
        s_scr[slot] = _dot(ksel_ref[kt], qr_scr[...])

    kc = kc_ref[...]
    mT = mT_ref[...]
    ncp = kc.shape[0]
    s_cmp = _dot(kc, q_scr[...])
    k_win = kwin_ref[pl.ds(lo, n_back + 1)].reshape((n_back + 1) * tk, KV_DIM)
    swin_scr[...] = _dot(k_win, qr_scr[...])
    qk_sel(start, 0)

    imp = []
    for g in range(N_KV_GROUPS):
        vcT = vcT_ref[g * HEAD_DIM:(g + 1) * HEAD_DIM, :]
        imp_parts = []
        for tsub in range(nsub):
            n_idx = lax.broadcasted_iota(jnp.int32, (ncp, LANES), 0)
            t_idx = qbase + tsub * LANES + lax.broadcasted_iota(jnp.int32, (ncp, LANES), 1)
            ok = (n_idx * CMP_STRIDE + (CMP_BLOCK - 1)) <= t_idx
            bias = jnp.where(ok, 0.0, NEG).astype(F32)
            keep = ok.astype(F32)
            p_sum = None
            for z in range(HEADS_PER_GROUP):
                c = (g * HEADS_PER_GROUP + z) * nsub + tsub
                s = s_cmp[:, chunk(c)] + bias
                e = jnp.exp2(s - jnp.max(s, axis=0, keepdims=True)) * keep
                inv = 1.0 / jnp.maximum(jnp.sum(e, axis=0, keepdims=True), 1e-30)
                oT_scr[:, chunk(c)] = _dot(vcT, e.astype(BF16)) * (inv * gate(0, c))
                p_sum = e * inv if p_sum is None else p_sum + e * inv
            imp_parts.append(jnp.dot(mT, p_sum, preferred_element_type=F32, precision=lax.Precision.HIGHEST))
        imp.append(jnp.concatenate(imp_parts, axis=1))

    j_idx = lax.broadcasted_iota(jnp.int32, (ns, tq), 0)
    cur = (qbase + lax.broadcasted_iota(jnp.int32, (ns, tq), 1)) // SEL_BLOCK
    forced = (j_idx == 0) | (j_idx == cur) | (j_idx == cur - 1)
    for g in range(N_KV_GROUPS):
        impv = jnp.where(j_idx <= cur, jnp.where(forced, FORCE, imp[g]), NEG)
        cnt = jnp.zeros((ns, tq), F32)
        for jp in range(ns):
            row = impv[jp:jp + 1, :]
            beats = (row > impv) | ((row == impv) & (j_idx > jp))
            cnt = cnt + beats.astype(F32)
        sel = (cnt < float(min(N_SEL, ns))) & (impv > 0.5 * NEG)
        selb = jnp.where(sel, 0.0, NEG).astype(F32)
        for kt in range(ns // bpt):
            selb_scr[g, kt, 0:bpt, :] = selb[kt * bpt:(kt + 1) * bpt, :]

    SEL, WIN = 0, 1
    m_scr[...] = jnp.full(m_scr.shape, NEG, F32)
    acc_scr[...] = jnp.zeros(acc_scr.shape, F32)

    def softmax_pv(st, load_s, vT_t, block_bias, tile_bias):
        for g in range(N_KV_GROUPS):
            es, alphas = [], []
            for cg in range(gchunk):
                c = g * gchunk + cg
                tsub = c % nsub
                s = load_s(c)
                if block_bias is not None:
                    rb = block_bias[g][:, sub(tsub)]
                    s = s + jnp.concatenate(
                        [jnp.broadcast_to(rb[i:i + 1, :], (SEL_BLOCK, LANES)) for i in range(bpt)], axis=0)
                if tile_bias is not None:
                    s = s + tile_bias(tsub)
                m_old = m_scr[st, :, chunk(c)]
                m_new = jnp.maximum(m_old, jnp.max(s, axis=0, keepdims=True))
                es.append(jnp.exp2(s - m_new).astype(BF16))
                alphas.append(jnp.exp2(m_old - m_new))
                m_scr[st, :, chunk(c)] = m_new
            lanes = slice(g * gl, (g + 1) * gl)
            pv = _dot(vT_t[g * V_ROWS:(g + 1) * V_ROWS, :], jnp.concatenate(es, axis=1))
            acc_scr[st, :, lanes] = jnp.concatenate(alphas, axis=1) * acc_scr[st, :, lanes] + pv

    def finish(st, br):
        for c in range(nchunk):
            acc = acc_scr[st, :, chunk(c)]
            inv = 1.0 / jnp.maximum(acc[HEAD_DIM:HEAD_DIM + 1, :], 1e-30)
            oT_scr[:, chunk(c)] += acc[0:HEAD_DIM, :] * (inv * gate(br, c))

    for i in range(n_back, -1, -1):
        rows = slice(i * tk, (i + 1) * tk)
        softmax_pv(WIN, lambda c: swin_scr[rows, chunk(c)], vwinT_ref[lo + i], None,
                   lambda tsub: wbias_ref[case, rows, sub(tsub)])
    finish(WIN, 2)

    def sp_sel(kt, slot, tile_bias):
        softmax_pv(SEL, lambda c: s_scr[slot, :, chunk(c)], vselT_ref[kt],
                   [selb_scr[g, kt, 0:bpt, :] for g in range(N_KV_GROUPS)], tile_bias)

    @pl.when(start == 1)
    def _():
        qk_sel(0, 1)
        sp_sel(0, 1, None)

    def pair_body(j, carry):
        a = start + 2 * j
        qk_sel(a + 1, 1)
        sp_sel(a, 0, None)
        qk_sel(a + 2, 0)
        sp_sel(a + 1, 1, None)
        return carry

    lax.fori_loop(0, p // 2, pair_body, 0)
    sp_sel(p, 0, lambda tsub: wbias_ref[n_back, n_back * tk:(n_back + 1) * tk, sub(tsub)])
    finish(SEL, 1)

    for tsub in range(nsub):
        for i in range(N_HEADS // 2):
            pair = jnp.concatenate([oT_scr[:, chunk((2 * i) * nsub + tsub)],
                                    oT_scr[:, chunk((2 * i + 1) * nsub + tsub)]], axis=0)
            o_ref[sub(tsub), i * 2 * HEAD_DIM:(i + 1) * 2 * HEAD_DIM] = pair.T.astype(BF16)


def _window_bias(t, n_back):
    r = np.arange(t)[:, None]
    c = np.arange(t)[None, :]
    causal = np.where(r <= c, 0.0, NEG)
    edge = np.where(r > c, 0.0, NEG)
    out = np.zeros((n_back + 1, (n_back + 1) * t, t), np.float32)
    for case in range(n_back + 1):
        for i in range(n_back + 1):
            back = case - i
            tile = NEG if back < 0 else causal if back == 0 else edge if back == n_back else 0.0
            out[case, i * t:(i + 1) * t, :] = tile
    return jnp.asarray(out)


def _attn(qT, qrT, kc, vcT, ksel, vselT, kwin, vwinT, gT, mT, B, S):
    tq, tk = ATT_TQ, ATT_TK
    assert tq == tk and WINDOW % tk == 0 and tk % SEL_BLOCK == 0 and tq % LANES == 0
    assert tk // SEL_BLOCK <= SUBLANES
    n_back = WINDOW // tk
    assert S // tk > n_back
    P = S // tq
    ncp = kc.shape[1]
    ns = S // SEL_BLOCK
    nq = N_HEADS * tq
    qspec = pl.BlockSpec((None, NSA_DIM, tq), lambda b, p: (b, 0, p))
    kspec = pl.BlockSpec((None, S // tk, tk, KV_DIM), lambda b, p: (b, 0, 0, 0))
    vspec = pl.BlockSpec((None, S // tk, N_KV_GROUPS * V_ROWS, tk), lambda b, p: (b, 0, 0, 0))
    return pl.pallas_call(
        _attn_kernel,
        grid=(B, P),
        in_specs=[
            qspec, qspec,
            pl.BlockSpec((None, ncp, KV_DIM), lambda b, p: (b, 0, 0)),
            pl.BlockSpec((None, KV_DIM, ncp), lambda b, p: (b, 0, 0)),
            kspec, vspec, kspec, vspec,
            pl.BlockSpec((None, N_KV_GROUPS * GATE_ROWS, tq), lambda b, p: (b, 0, p)),
            pl.BlockSpec((ns, ncp), lambda b, p: (0, 0)),
            pl.BlockSpec((n_back + 1, (n_back + 1) * tk, tq), lambda b, p: (0, 0, 0)),
        ],
        out_specs=pl.BlockSpec((tq, NSA_DIM), lambda b, p: (b * P + p, 0)),
        out_shape=jax.ShapeDtypeStruct((B * S, NSA_DIM), BF16),
        scratch_shapes=[
            pltpu.VMEM((N_KV_GROUPS, S // tk, SUBLANES, tq), F32),
            pltpu.VMEM((KV_DIM, nq), BF16),
            pltpu.VMEM((KV_DIM, nq), BF16),
            pltpu.VMEM((2, tk, nq), F32),
            pltpu.VMEM(((n_back + 1) * tk, nq), F32),
            pltpu.VMEM((2, 1, nq), F32),
            pltpu.VMEM((2, V_ROWS, nq), F32),
            pltpu.VMEM((HEAD_DIM, nq), F32),
        ],
        compiler_params=pltpu.CompilerParams(
            dimension_semantics=("arbitrary", "arbitrary"), vmem_limit_bytes=VMEM_LIMIT),
        name="attn",
    )(qT, qrT, kc, vcT, ksel, vselT, kwin, vwinT, gT, mT, _window_bias(tk, n_back))


def _layer_norm(z, g, b):
    mu = jnp.mean(z, axis=-1, keepdims=True)
    zc = z - mu
    var = jnp.mean(zc * zc, axis=-1, keepdims=True)
    return zc * lax.rsqrt(var + LN_EPS) * g + b


def _tail_kernel(alpha, x_ref, mix_ref, o_ref, wg_ref, wco_ref, wno_ref, wo_ref, g1_ref, b1_ref,
                 wup_ref, wdn_ref, g2_ref, b2_ref, out_ref):
    x = x_ref[...]
    xb = x.astype(BF16)
    y_conv = _dot(mix_ref[...], wco_ref[...])
    y_nsa = _dot(o_ref[...], wno_ref[...])
    mixed = (jax.nn.sigmoid(_dot_nt(xb, wg_ref[0:D_MODEL, :])) * y_conv
             + jax.nn.sigmoid(_dot_nt(xb, wg_ref[D_MODEL:2 * D_MODEL, :])) * y_nsa)
    x1 = _layer_norm(alpha * x + _dot(mixed.astype(BF16), wo_ref[...]), g1_ref[...], b1_ref[...])
    x1b = x1.astype(BF16)
    ff = jnp.zeros_like(x1)
    for c in range(D_FF // FF_CHUNK):
        h = jnp.maximum(_dot(x1b, wup_ref[:, c * FF_CHUNK:(c + 1) * FF_CHUNK]), 0.0)
        ff = ff + _dot((h * h).astype(BF16), wdn_ref[c * FF_CHUNK:(c + 1) * FF_CHUNK, :])
    out_ref[...] = _layer_norm(alpha * x1 + ff, g2_ref[...], b2_ref[...])


def _tail(x2, mix, o, wg, wco, wno, wo, g1, b1, wup, wdn, g2, b2, alpha):
    n = x2.shape[0]
    tm = TAIL_TM
    row = lambda i: (i, 0)
    const = lambda i: (0, 0)
    resident = lambda shape: pl.BlockSpec(shape, const, pipeline_mode=pl.Buffered(1))
    return pl.pallas_call(
        functools.partial(_tail_kernel, alpha),
        grid=(n // tm,),
        in_specs=[
            pl.BlockSpec((tm, D_MODEL), row), pl.BlockSpec((tm, CONV_DIM), row), pl.BlockSpec((tm, NSA_DIM), row),
            resident((2 * D_MODEL, D_MODEL)), resident((CONV_DIM, D_MODEL)), resident((NSA_DIM, D_MODEL)),
            resident((D_MODEL, D_MODEL)), resident((1, D_MODEL)), resident((1, D_MODEL)),
            resident((D_MODEL, D_FF)), resident((D_FF, D_MODEL)), resident((1, D_MODEL)), resident((1, D_MODEL)),
        ],
        out_specs=pl.BlockSpec((tm, D_MODEL), row),
        out_shape=jax.ShapeDtypeStruct((n, D_MODEL), F32),
        compiler_params=pltpu.CompilerParams(dimension_semantics=("arbitrary",), vmem_limit_bytes=VMEM_LIMIT),
        name="tail",
    )(x2, mix, o, wg, wco, wno, wo, g1, b1, wup, wdn, g2, b2)


def _rope_tables(S):
    inv = ROPE_THETA ** (-jnp.arange(0, ROPE_DIM, 2, dtype=F32) / ROPE_DIM)
    ang = jnp.arange(S, dtype=F32)[:, None] * inv[None, :]
    cos, sin = jnp.cos(ang), jnp.sin(ang)
    pad_c = jnp.ones((S, HEAD_DIM - ROPE_DIM), F32)
    pad_s = jnp.zeros((S, HEAD_DIM - ROPE_DIM), F32)
    ck = jnp.tile(jnp.concatenate([cos, cos, pad_c], axis=1), (1, N_KV_GROUPS))
    sk = jnp.tile(jnp.concatenate([-sin, sin, pad_s], axis=1), (1, N_KV_GROUPS))
    return ck, sk, cos.T, sin.T


def _overlap_matrix_t(ncp, ns):
    nc = ncp - 1
    i = np.arange(ncp)[None, :]
    j = np.arange(ns)[:, None]
    m = (i * CMP_STRIDE < (j + 1) * SEL_BLOCK) & (i * CMP_STRIDE + CMP_BLOCK > j * SEL_BLOCK) & (i < nc)
    return jnp.asarray(m.astype(np.float32))


def _compress_weights(pe, w1, w2):
    eye = jnp.eye(N_KV_GROUPS, dtype=F32)
    peh = pe.reshape(2, CMP_STRIDE, 1, HEAD_DIM)
    pebig = jnp.broadcast_to(peh, (2, CMP_STRIDE, N_KV_GROUPS, HEAD_DIM)).reshape(2, 1, CMP_STRIDE * KV_DIM)
    pebig = jnp.broadcast_to(pebig, (2, SUBLANES, CMP_STRIDE * KV_DIM))
    w2big = jnp.einsum('ab,hd->ahbd', eye, w2).reshape(N_KV_GROUPS * CMP_HIDDEN, KV_DIM)
    return pebig.astype(BF16), w1.astype(BF16), w2big.astype(BF16)


def _layer(x2, B, S, w_in, conv_w, w_conv_out, pe_k, wk1, wk2, pe_v, wv1, wv2, w_nsa_out, w_o,
           ln1_g, ln1_b, w_up, w_down, ln2_g, ln2_b, alpha):
    ncp = S // CMP_STRIDE
    ns = S // SEL_BLOCK
    wstd, wg, wtr = _wprep(jnp.swapaxes(w_in, 0, 1))
    ck, sk, cq, sq = _rope_tables(S)

    mix, kch, vch, ksel, kwin, qT, qrT, vselT, vwinT, gT = _proj(
        x2, wstd, wtr, conv_w.reshape(CONV_WIDTH, CONV_DIM), ck, sk, cq, sq, B, S)

    pek, wk1b, wk2b = _compress_weights(pe_k, wk1, wk2)
    pev, wv1b, wv2b = _compress_weights(pe_v, wv1, wv2)
    kc, vcT = _compress(kch, vch, pek, pev, wk1b, wv1b, wk2b, wv2b.T, B, ncp)

    tk = ATT_TK
    o = _attn(qT, qrT, kc, vcT, ksel.reshape(B, S // tk, tk, KV_DIM), vselT,
              kwin.reshape(B, S // tk, tk, KV_DIM), vwinT, gT, _overlap_matrix_t(ncp, ns), B, S)

    row = lambda v: v.reshape(1, D_MODEL).astype(F32)
    return _tail(x2, mix, o, wg, w_conv_out.astype(BF16), w_nsa_out.astype(BF16), w_o.astype(BF16),
                 row(ln1_g), row(ln1_b), w_up.astype(BF16), w_down.astype(BF16), row(ln2_g), row(ln2_b), alpha)


def kernel(x, w_in, conv_w, w_conv_out, pe_k_cmp, w_k_cmp1, w_k_cmp2, pe_v_cmp, w_v_cmp1, w_v_cmp2,
           w_nsa_out, w_o, ln1_g, ln1_b, w_up, w_down, ln2_g, ln2_b):
    B, S, D = x.shape
    depth = w_in.shape[0]
    assert D == D_MODEL and w_in.shape[2] == O_END
    assert S % PROJ_TM == 0 and S % ATT_TQ == 0 and (B * S) % TAIL_TM == 0 and S >= WINDOW
    alpha = float((2 * depth) ** 0.25)
    x2 = x.reshape(B * S, D)
    for l in range(depth):
        x2 = _layer(x2, B, S, w_in[l], conv_w[l], w_conv_out[l], pe_k_cmp[l], w_k_cmp1[l], w_k_cmp2[l],
                    pe_v_cmp[l], w_v_cmp1[l], w_v_cmp2[l], w_nsa_out[l], w_o[l], ln1_g[l], ln1_b[l],
                    w_up[l], w_down[l], ln2_g[l], ln2_b[l], alpha)
    return x2.reshape(B, S, D)
```

```python
import functools

import numpy as np
import jax
import jax.numpy as jnp
from jax import lax
from jax.experimental import pallas as pl
from jax.experimental.pallas import tpu as pltpu

F32 = jnp.float32
BF16 = jnp.bfloat16

D_MODEL = 1024
CONV_DIM = D_MODEL // 2
CONV_WIDTH = 3
N_HEADS = 8
HEAD_DIM = 64
N_KV_GROUPS = 2
HEADS_PER_GROUP = N_HEADS // N_KV_GROUPS
NSA_DIM = N_HEADS * HEAD_DIM
KV_DIM = N_KV_GROUPS * HEAD_DIM
ROPE_DIM = HEAD_DIM // 4
ROPE_HALF = ROPE_DIM // 2
ROPE_THETA = 500000.0
CMP_BLOCK = 32
CMP_STRIDE = 16
CMP_HIDDEN = 2 * HEAD_DIM
SEL_BLOCK = 64
N_SEL = 16
WINDOW = 512
N_NSA_BRANCHES = 3
D_FF = 4 * D_MODEL
LN_EPS = 1e-5
NEG = -1e30
FORCE = 1e9
SCALE = HEAD_DIM ** -0.5
LOG2E = 1.4426950408889634

SUBLANES = 8
LANES = 128
BF16_ROWS = 16
V_ROWS = HEAD_DIM + BF16_ROWS

_SPLITS = (CONV_DIM, CONV_DIM, CONV_DIM, NSA_DIM, KV_DIM, KV_DIM, KV_DIM, KV_DIM, KV_DIM, KV_DIM,
           N_HEADS * N_NSA_BRANCHES, D_MODEL, D_MODEL)
_OFFS = np.concatenate([[0], np.cumsum(_SPLITS)]).tolist()
(O_H, O_B, O_C, O_Q, O_KCMP, O_VCMP, O_KSEL, O_VSEL, O_KWIN, O_VWIN, O_GATE, O_GCONV, O_GNSA, O_END) = _OFFS

GATE_ROWS = 16
N_STD = 3 * CONV_DIM + 4 * KV_DIM
N_TR = NSA_DIM + 2 * KV_DIM + N_KV_GROUPS * GATE_ROWS

PROJ_TM = 512
ATT_TQ = 256
ATT_TK = 256
TAIL_TM = 512
FF_CHUNK = 1024
VMEM_LIMIT = 56 * 1024 * 1024


def _dot(a, b):
    return jnp.dot(a, b, preferred_element_type=F32)


def _dot_nt(a, b):
    return lax.dot_general(a, b, (((1,), (1,)), ((), ())), preferred_element_type=F32)


WPREP_COLS = 256


def _wprep_kernel(wT_ref, wstdT_ref, wgT_ref, wtr_ref):
    r = 0
    for a, b in ((O_H, O_Q), (O_KCMP, O_KSEL), (O_KSEL, O_VSEL), (O_KWIN, O_VWIN)):
        wstdT_ref[r:r + b - a, :] = wT_ref[a:b, :].astype(BF16)
        r += b - a
    wgT_ref[...] = wT_ref[O_GCONV:O_END, :].astype(BF16)
    r = 0
    for a, b in ((O_Q, O_KCMP), (O_VSEL, O_KWIN), (O_VWIN, O_GATE)):
        wtr_ref[r:r + b - a, :] = wT_ref[a:b, :].astype(BF16)
        r += b - a
    gates = wT_ref[O_GATE:O_GCONV, :]
    rows = []
    for g in range(N_KV_GROUPS):
        for br in range(N_NSA_BRANCHES):
            for z in range(HEADS_PER_GROUP):
                src = (g * HEADS_PER_GROUP + z) * N_NSA_BRANCHES + br
                rows.append(gates[src:src + 1, :])
        rows.append(jnp.zeros((GATE_ROWS - N_NSA_BRANCHES * HEADS_PER_GROUP, gates.shape[1]), F32))
    wtr_ref[r:r + N_KV_GROUPS * GATE_ROWS, :] = jnp.concatenate(rows, axis=0).astype(BF16)


def _wprep(wT):
    d = wT.shape[1]
    cb = WPREP_COLS
    assert d % cb == 0 and O_GCONV % SUBLANES == 0
    strip = lambda rows: pl.BlockSpec((rows, cb), lambda i: (0, i))
    return pl.pallas_call(
        _wprep_kernel,
        grid=(d // cb,),
        in_specs=[strip(O_END)],
        out_specs=(strip(N_STD), strip(2 * D_MODEL), strip(N_TR)),
        out_shape=(jax.ShapeDtypeStruct((N_STD, d), BF16), jax.ShapeDtypeStruct((2 * D_MODEL, d), BF16),
                   jax.ShapeDtypeStruct((N_TR, d), BF16)),
        compiler_params=pltpu.CompilerParams(dimension_semantics=("arbitrary",), vmem_limit_bytes=VMEM_LIMIT),
        name="wprep",
    )(wT)


def _proj_kernel(x_ref, wstd_ref, wtr_ref, convw_ref, cosk_ref, sink_ref, cosq_ref, sinq_ref,
                 mix_ref, kch_ref, vch_ref, ksel_ref, kwin_ref,
                 qT_ref, qrT_ref, vselT_ref, vwinT_ref, gT_ref, ubuf, kvbuf):
    tm = x_ref.shape[0]

    @pl.when(pl.program_id(1) == 0)
    def _():
        ubuf[...] = jnp.zeros(ubuf.shape, F32)

    xb = x_ref[...].astype(BF16)
    hbc = _dot_nt(xb, wstd_ref[0:3 * CONV_DIM, :])
    kk = _dot_nt(xb, wstd_ref[3 * CONV_DIM:N_STD, :])
    yT = _dot_nt(wtr_ref[...], xb)

    u = hbc[:, 2 * CONV_DIM:3 * CONV_DIM] * hbc[:, 0:CONV_DIM]
    prev = ubuf[...]
    ubuf[...] = u[tm - SUBLANES:tm, :]
    head_rows = lax.broadcasted_iota(jnp.int32, (SUBLANES, CONV_DIM), 0)

    def shifted(k):
        r = pltpu.roll(u, k, 0)
        first = jnp.where(head_rows < k, pltpu.roll(prev, k, 0), r[0:SUBLANES, :])
        return jnp.concatenate([first, r[SUBLANES:, :]], axis=0)

    cw = convw_ref[...]
    y = cw[0:1, :] * shifted(2) + cw[1:2, :] * shifted(1) + cw[2:3, :] * u
    mix_ref[...] = (hbc[:, CONV_DIM:2 * CONV_DIM] * y).astype(BF16)

    for j, out in enumerate((kch_ref, vch_ref)):
        kvbuf[j] = kk[:, j * KV_DIM:(j + 1) * KV_DIM]
        for r in range(CMP_STRIDE):
            rows = kvbuf[j, pl.ds(r, tm // CMP_STRIDE, stride=CMP_STRIDE), :]
            out[:, r * KV_DIM:(r + 1) * KV_DIM] = rows.astype(BF16)
    lane = lax.broadcasted_iota(jnp.int32, (tm, KV_DIM), 1)
    first_half = (lane % ROPE_DIM) < ROPE_HALF
    ck = cosk_ref[...]
    sk = sink_ref[...]
    for j, out in ((2, ksel_ref), (3, kwin_ref)):
        k = kk[:, j * KV_DIM:(j + 1) * KV_DIM]
        partner = jnp.where(first_half, pltpu.roll(k, KV_DIM - ROPE_HALF, 1), pltpu.roll(k, ROPE_HALF, 1))
        out[...] = (k * ck + partner * sk).astype(BF16)

    q = yT[0:NSA_DIM, :] * (SCALE * LOG2E)
    qT_ref[...] = q.astype(BF16)
    qrT_ref[...] = q.astype(BF16)
    cq = cosq_ref[...]
    sq = sinq_ref[...]
    for h in range(N_HEADS):
        r0 = h * HEAD_DIM
        x1 = q[r0:r0 + ROPE_HALF, :]
        x2 = q[r0 + ROPE_HALF:r0 + ROPE_DIM, :]
        rot = jnp.concatenate([x1 * cq - x2 * sq, x2 * cq + x1 * sq], axis=0)
        qrT_ref[r0:r0 + ROPE_DIM, :] = rot.astype(BF16)
    tk = vselT_ref.shape[-1]
    ones = jnp.ones((BF16_ROWS, tk), BF16)
    for j, out in enumerate((vselT_ref, vwinT_ref)):
        for i in range(tm // tk):
            for g in range(N_KV_GROUPS):
                r0 = NSA_DIM + j * KV_DIM + g * HEAD_DIM
                out[i, g * V_ROWS:g * V_ROWS + HEAD_DIM, :] = yT[r0:r0 + HEAD_DIM, i * tk:(i + 1) * tk].astype(BF16)
                out[i, g * V_ROWS + HEAD_DIM:(g + 1) * V_ROWS, :] = ones
    gT_ref[...] = jax.nn.sigmoid(yT[NSA_DIM + 2 * KV_DIM:N_TR, :])


def _proj(x2, wstd, wtr, convw, cosk, sink, cosq, sinq, B, S):
    tm, tk = PROJ_TM, ATT_TK
    ns = S // tm
    row = lambda b, s: (b * ns + s, 0)
    const = lambda b, s: (0, 0)
    tok = lambda c, dt: jax.ShapeDtypeStruct((B * S, c), dt)
    cw = CMP_STRIDE * KV_DIM
    chunked = jax.ShapeDtypeStruct((B, S // CMP_STRIDE, cw), BF16)
    chunk_spec = pl.BlockSpec((None, tm // CMP_STRIDE, cw), lambda b, s: (b, s, 0))
    out_shape = (
        tok(CONV_DIM, BF16), chunked, chunked, tok(KV_DIM, BF16), tok(KV_DIM, BF16),
        jax.ShapeDtypeStruct((B, NSA_DIM, S), BF16), jax.ShapeDtypeStruct((B, NSA_DIM, S), BF16),
        jax.ShapeDtypeStruct((B, S // tk, N_KV_GROUPS * V_ROWS, tk), BF16),
        jax.ShapeDtypeStruct((B, S // tk, N_KV_GROUPS * V_ROWS, tk), BF16),
        jax.ShapeDtypeStruct((B, N_KV_GROUPS * GATE_ROWS, S), F32),
    )
    feat = lambda r: pl.BlockSpec((None, r, tm), lambda b, s: (b, 0, s))
    vt = pl.BlockSpec((None, tm // tk, N_KV_GROUPS * V_ROWS, tk), lambda b, s: (b, s, 0, 0))
    return pl.pallas_call(
        _proj_kernel,
        grid=(B, ns),
        in_specs=[
            pl.BlockSpec((tm, D_MODEL), row),
            pl.BlockSpec((N_STD, D_MODEL), const),
            pl.BlockSpec((N_TR, D_MODEL), const),
            pl.BlockSpec((CONV_WIDTH, CONV_DIM), const),
            pl.BlockSpec((tm, KV_DIM), lambda b, s: (s, 0)),
            pl.BlockSpec((tm, KV_DIM), lambda b, s: (s, 0)),
            pl.BlockSpec((ROPE_HALF, tm), lambda b, s: (0, s)),
            pl.BlockSpec((ROPE_HALF, tm), lambda b, s: (0, s)),
        ],
        out_specs=(
            pl.BlockSpec((tm, CONV_DIM), row), chunk_spec, chunk_spec,
            pl.BlockSpec((tm, KV_DIM), row), pl.BlockSpec((tm, KV_DIM), row),
            feat(NSA_DIM), feat(NSA_DIM), vt, vt, feat(N_KV_GROUPS * GATE_ROWS),
        ),
        out_shape=out_shape,
        scratch_shapes=[pltpu.VMEM((SUBLANES, CONV_DIM), F32), pltpu.VMEM((2, tm, KV_DIM), F32)],
        compiler_params=pltpu.CompilerParams(
            dimension_semantics=("arbitrary", "arbitrary"), vmem_limit_bytes=VMEM_LIMIT),
        name="proj",
    )(x2, wstd, wtr, convw, cosk, sink, cosq, sinq)


def _gelu_tanh(x):
    return x * (0.5 * (1.0 + jnp.tanh(np.sqrt(2.0 / np.pi).astype(np.float32) * (x + 0.044715 * (x * x * x)))))


def _compress_kernel(kch_ref, vch_ref, pek_ref, pev_ref, wk1_ref, wv1_ref, wk2_ref, wv2t_ref, kc_ref, vcT_ref,
                     wk1_big, wv1_big):
    nb, ncp = kch_ref.shape[0], kch_ref.shape[1]
    for w_ref, big in ((wk1_ref, wk1_big), (wv1_ref, wv1_big)):
        big[...] = jnp.zeros(big.shape, BF16)
        for half in range(2):
            for r in range(CMP_STRIDE):
                for g in range(N_KV_GROUPS):
                    r0 = r * KV_DIM + g * HEAD_DIM
                    big[half, r0:r0 + HEAD_DIM, g * CMP_HIDDEN:(g + 1) * CMP_HIDDEN] = (
                        w_ref[half * CMP_STRIDE + r])

    def hidden(ch_ref, pe_ref, big):
        ch = ch_ref[...].reshape(nb * ncp, ch_ref.shape[2])
        a = _dot(ch, big[0])
        b = _dot(ch, big[1])
        pe = _dot(pe_ref[0], big[0]) + _dot(pe_ref[1], big[1])
        b_next = pltpu.roll(b, b.shape[0] - 1, 0)
        return _gelu_tanh(a + b_next + pe[0:1, :]).astype(BF16)

    kc = _dot(hidden(kch_ref, pek_ref, wk1_big), wk2_ref[...]).astype(BF16)
    kc_ref[...] = kc.reshape(nb, ncp, kc.shape[1])
    vT = _dot_nt(wv2t_ref[...], hidden(vch_ref, pev_ref, wv1_big)).astype(BF16)
    for b in range(nb):
        vcT_ref[b] = vT[:, b * ncp:(b + 1) * ncp]


def _compress(kch, vch, pek, pev, wk1, wv1, wk2, wv2t, B, ncp):
    cw = CMP_STRIDE * KV_DIM
    gh = N_KV_GROUPS * CMP_HIDDEN
    full = lambda shape: pl.BlockSpec(shape, lambda i: (0,) * len(shape))
    return pl.pallas_call(
        _compress_kernel,
        grid=(1,),
        in_specs=[
            full((B, ncp, cw)), full((B, ncp, cw)),
            full((2, SUBLANES, cw)), full((2, SUBLANES, cw)),
            full((CMP_BLOCK, HEAD_DIM, CMP_HIDDEN)), full((CMP_BLOCK, HEAD_DIM, CMP_HIDDEN)),
            full((gh, KV_DIM)), full((KV_DIM, gh)),
        ],
        out_specs=(full((B, ncp, KV_DIM)), full((B, KV_DIM, ncp))),
        out_shape=(jax.ShapeDtypeStruct((B, ncp, KV_DIM), BF16), jax.ShapeDtypeStruct((B, KV_DIM, ncp), BF16)),
        scratch_shapes=[pltpu.VMEM((2, cw, gh), BF16), pltpu.VMEM((2, cw, gh), BF16)],
        compiler_params=pltpu.CompilerParams(dimension_semantics=("arbitrary",), vmem_limit_bytes=VMEM_LIMIT),
        name="compress",
    )(kch, vch, pek, pev, wk1, wv1, wk2, wv2t)


def _attn_kernel(qT_ref, qrT_ref, kc_ref, vcT_ref, ksel_ref, vselT_ref, kwin_ref, vwinT_ref, gT_ref, mT_ref,
                 wbias_ref, o_ref, selb_scr, q_scr, qr_scr, s_scr, swin_scr, m_scr, acc_scr, oT_scr):
    tq = qT_ref.shape[1]
    tk = ksel_ref.shape[1]
    nsub = tq // LANES
    nchunk = N_HEADS * nsub
    gchunk = HEADS_PER_GROUP * nsub
    gl = HEADS_PER_GROUP * tq
    ns = mT_ref.shape[0]
    bpt = tk // SEL_BLOCK
    p = pl.program_id(1)
    qbase = p * tq

    def chunk(c):
        return slice(c * LANES, (c + 1) * LANES)

    def sub(tsub):
        return slice(tsub * LANES, (tsub + 1) * LANES)

    def stage_q(ref, scr):
        zeros = jnp.zeros((HEAD_DIM, tq), BF16)
        for h in range(N_HEADS):
            qh = ref[h * HEAD_DIM:(h + 1) * HEAD_DIM, :]
            halves = [qh, zeros] if h // HEADS_PER_GROUP == 0 else [zeros, qh]
            scr[:, h * tq:(h + 1) * tq] = jnp.concatenate(halves, axis=0)

    gt = gT_ref[...]

    def gate(br, c):
        h, tsub = divmod(c, nsub)
        g, z = divmod(h, HEADS_PER_GROUP)
        r = g * GATE_ROWS + br * HEADS_PER_GROUP + z
        return gt[r:r + 1, sub(tsub)]

    stage_q(qT_ref, q_scr)
    stage_q(qrT_ref, qr_scr)
    start = p % 2
    n_back = WINDOW // tk
    case = jnp.minimum(p, n_back)
    lo = p - case

    def qk_sel(kt, slot):
        s_all = _dot(ksel_ref[kt], qr_scr[...])
        for c in range(nchunk):
            rb = selb_scr[c // gchunk, kt, 0:bpt, sub(c % nsub)]
            s_scr[slot, :, chunk(c)] = s_all[:, chunk(c)] + jnp.concatenate(
                [jnp.broadcast_to(rb[i:i + 1, :], (SEL_BLOCK, LANES)) for i in range(bpt)], axis=0)

    kc = kc_ref[...]
    mT = mT_ref[...]
    ncp = kc.shape[0]
    s_cmp = _dot(kc, q_scr[...])
    k_win = kwin_ref[pl.ds(lo, n_back + 1)].reshape((n_back + 1) * tk, KV_DIM)
    s_win = _dot(k_win, qr_scr[...])
    for c in range(nchunk):
        swin_scr[:, chunk(c)] = s_win[:, chunk(c)] + wbias_ref[case, :, sub(c % nsub)]

    imp = []
    for g in range(N_KV_GROUPS):
        vcT = vcT_ref[g * HEAD_DIM:(g + 1) * HEAD_DIM, :]
        imp_parts = []
        for tsub in range(nsub):
            n_idx = lax.broadcasted_iota(jnp.int32, (ncp, LANES), 0)
            t_idx = qbase + tsub * LANES + lax.broadcasted_iota(jnp.int32, (ncp, LANES), 1)
            ok = (n_idx * CMP_STRIDE + (CMP_BLOCK - 1)) <= t_idx
            bias = jnp.where(ok, 0.0, NEG).astype(F32)
            keep = ok.astype(F32)
            p_sum = None
            for z in range(HEADS_PER_GROUP):
                c = (g * HEADS_PER_GROUP + z) * nsub + tsub
                s = s_cmp[:, chunk(c)] + bias
                e = jnp.exp2(s - jnp.max(s, axis=0, keepdims=True)) * keep
                inv = 1.0 / jnp.maximum(jnp.sum(e, axis=0, keepdims=True), 1e-30)
                oT_scr[:, chunk(c)] = _dot(vcT, e.astype(BF16)) * (inv * gate(0, c))
                p_sum = e * inv if p_sum is None else p_sum + e * inv
            imp_parts.append(jnp.dot(mT, p_sum, preferred_element_type=F32, precision=lax.Precision.HIGHEST))
        imp.append(jnp.concatenate(imp_parts, axis=1))

    j_idx = lax.broadcasted_iota(jnp.int32, (ns, tq), 0)
    cur = (qbase + lax.broadcasted_iota(jnp.int32, (ns, tq), 1)) // SEL_BLOCK
    forced = (j_idx == 0) | (j_idx == cur) | (j_idx == cur - 1)
    for g in range(N_KV_GROUPS):
        impv = jnp.where(j_idx <= cur, jnp.where(forced, FORCE, imp[g]), NEG)
        groups = [impv[a:a + SUBLANES, :] for a in range(0, ns, SUBLANES)]
        sub_idx = lax.broadcasted_iota(jnp.int32, (SUBLANES, tq), 0)
        cnts = [jnp.zeros((SUBLANES, tq), F32) for _ in groups]
        for jp in range(ns):
            row = impv[jp:jp + 1, :]
            for gi, grp in enumerate(groups):
                if gi > jp // SUBLANES:
                    beats = row >= grp
                elif gi < jp // SUBLANES:
                    beats = row > grp
                else:
                    beats = (row > grp) | ((row == grp) & (sub_idx > jp % SUBLANES))
                cnts[gi] = cnts[gi] + beats.astype(F32)
        cnt = jnp.concatenate(cnts, axis=0)
        sel = (cnt < float(min(N_SEL, ns))) & (impv > 0.5 * NEG)
        selb = jnp.where(sel, 0.0, NEG).astype(F32)
        for kt in range(ns // bpt):
            selb_scr[g, kt, 0:bpt, :] = selb[kt * bpt:(kt + 1) * bpt, :]
    qk_sel(start, 0)

    SEL, WIN = 0, 1
    m_scr[...] = jnp.full(m_scr.shape, NEG, F32)
    acc_scr[...] = jnp.zeros(acc_scr.shape, F32)

    def softmax_pv(st, load_s, vT_t, tile_bias=None):
        for g in range(N_KV_GROUPS):
            es, alphas = [], []
            for cg in range(gchunk):
                c = g * gchunk + cg

                def scores():
                    s = load_s(c)
                    return s if tile_bias is None else s + tile_bias(c % nsub)

                m_old = m_scr[st, :, chunk(c)]
                m_new = jnp.maximum(m_old, jnp.max(scores(), axis=0, keepdims=True))
                m_scr[st, :, chunk(c)] = m_new
                es.append(jnp.exp2(scores() - m_new).astype(BF16))
                alphas.append(jnp.exp2(m_old - m_new))
            lanes = slice(g * gl, (g + 1) * gl)
            pv = _dot(vT_t[g * V_ROWS:(g + 1) * V_ROWS, :], jnp.concatenate(es, axis=1))
            acc_scr[st, :, lanes] = jnp.concatenate(alphas, axis=1) * acc_scr[st, :, lanes] + pv

    def finish(st, br):
        for c in range(nchunk):
            acc = acc_scr[st, :, chunk(c)]
            inv = 1.0 / jnp.maximum(acc[HEAD_DIM:HEAD_DIM + 1, :], 1e-30)
            oT_scr[:, chunk(c)] += acc[0:HEAD_DIM, :] * (inv * gate(br, c))

    for i in range(n_back, -1, -1):
        rows = slice(i * tk, (i + 1) * tk)
        softmax_pv(WIN, lambda c: swin_scr[rows, chunk(c)], vwinT_ref[lo + i])
    finish(WIN, 2)

    def sp_sel(kt, slot, tile_bias=None):
        softmax_pv(SEL, lambda c: s_scr[slot, :, chunk(c)], vselT_ref[kt], tile_bias)

    @pl.when(start == 1)
    def _():
        qk_sel(0, 1)
        sp_sel(0, 1)

    def pair_body(j, carry):
        a = start + 2 * j
        qk_sel(a + 1, 1)
        sp_sel(a, 0)
        qk_sel(a + 2, 0)
        sp_sel(a + 1, 1)
        return carry

    lax.fori_loop(0, p // 2, pair_body, 0)
    sp_sel(p, 0, lambda tsub: wbias_ref[n_back, n_back * tk:(n_back + 1) * tk, sub(tsub)])
    finish(SEL, 1)

    for tsub in range(nsub):
        for i in range(N_HEADS // 2):
            pair = jnp.concatenate([oT_scr[:, chunk((2 * i) * nsub + tsub)],
                                    oT_scr[:, chunk((2 * i + 1) * nsub + tsub)]], axis=0)
            o_ref[sub(tsub), i * 2 * HEAD_DIM:(i + 1) * 2 * HEAD_DIM] = pair.T.astype(BF16)


def _window_bias(t, n_back):
    r = np.arange(t)[:, None]
    c = np.arange(t)[None, :]
    causal = np.where(r <= c, 0.0, NEG)
    edge = np.where(r > c, 0.0, NEG)
    out = np.zeros((n_back + 1, (n_back + 1) * t, t), np.float32)
    for case in range(n_back + 1):
        for i in range(n_back + 1):
            back = case - i
            tile = NEG if back < 0 else causal if back == 0 else edge if back == n_back else 0.0
            out[case, i * t:(i + 1) * t, :] = tile
    return jnp.asarray(out)


def _attn(qT, qrT, kc, vcT, ksel, vselT, kwin, vwinT, gT, mT, B, S):
    tq, tk = ATT_TQ, ATT_TK
    assert tq == tk and WINDOW % tk == 0 and tk % SEL_BLOCK == 0 and tq % LANES == 0
    assert tk // SEL_BLOCK <= SUBLANES
    n_back = WINDOW // tk
    assert S // tk > n_back
    P = S // tq
    ncp = kc.shape[1]
    ns = S // SEL_BLOCK
    nq = N_HEADS * tq
    qspec = pl.BlockSpec((None, NSA_DIM, tq), lambda b, p: (b, 0, p))
    kspec = pl.BlockSpec((None, S // tk, tk, KV_DIM), lambda b, p: (b, 0, 0, 0))
    vspec = pl.BlockSpec((None, S // tk, N_KV_GROUPS * V_ROWS, tk), lambda b, p: (b, 0, 0, 0))
    return pl.pallas_call(
        _attn_kernel,
        grid=(B, P),
        in_specs=[
            qspec, qspec,
            pl.BlockSpec((None, ncp, KV_DIM), lambda b, p: (b, 0, 0)),
            pl.BlockSpec((None, KV_DIM, ncp), lambda b, p: (b, 0, 0)),
            kspec, vspec, kspec, vspec,
            pl.BlockSpec((None, N_KV_GROUPS * GATE_ROWS, tq), lambda b, p: (b, 0, p)),
            pl.BlockSpec((ns, ncp), lambda b, p: (0, 0)),
            pl.BlockSpec((n_back + 1, (n_back + 1) * tk, tq), lambda b, p: (0, 0, 0)),
        ],
        out_specs=pl.BlockSpec((tq, NSA_DIM), lambda b, p: (b * P + p, 0)),
        out_shape=jax.ShapeDtypeStruct((B * S, NSA_DIM), BF16),
        scratch_shapes=[
            pltpu.VMEM((N_KV_GROUPS, S // tk, SUBLANES, tq), F32),
            pltpu.VMEM((KV_DIM, nq), BF16),
            pltpu.VMEM((KV_DIM, nq), BF16),
            pltpu.VMEM((2, tk, nq), F32),
            pltpu.VMEM(((n_back + 1) * tk, nq), F32),
            pltpu.VMEM((2, 1, nq), F32),
            pltpu.VMEM((2, V_ROWS, nq), F32),
            pltpu.VMEM((HEAD_DIM, nq), F32),
        ],
        compiler_params=pltpu.CompilerParams(
            dimension_semantics=("arbitrary", "arbitrary"), vmem_limit_bytes=VMEM_LIMIT),
        name="attn",
    )(qT, qrT, kc, vcT, ksel, vselT, kwin, vwinT, gT, mT, _window_bias(tk, n_back))


def _layer_norm(z, g, b):
    mu = jnp.mean(z, axis=-1, keepdims=True)
    zc = z - mu
    var = jnp.mean(zc * zc, axis=-1, keepdims=True)
    return zc * lax.rsqrt(var + LN_EPS) * g + b


def _tail_kernel(alpha, x_ref, mix_ref, o_ref, wg_ref, wco_ref, wno_ref, wo_ref, g1_ref, b1_ref,
                 wup_ref, wdn_ref, g2_ref, b2_ref, out_ref):
    x = x_ref[...]
    xb = x.astype(BF16)
    y_conv = _dot(mix_ref[...], wco_ref[...])
    y_nsa = _dot(o_ref[...], wno_ref[...])
    mixed = (jax.nn.sigmoid(_dot_nt(xb, wg_ref[0:D_MODEL, :])) * y_conv
             + jax.nn.sigmoid(_dot_nt(xb, wg_ref[D_MODEL:2 * D_MODEL, :])) * y_nsa)
    x1 = _layer_norm(alpha * x + _dot(mixed.astype(BF16), wo_ref[...]), g1_ref[...], b1_ref[...])
    x1b = x1.astype(BF16)
    ff = jnp.zeros_like(x1)
    for c in range(D_FF // FF_CHUNK):
        h = jnp.maximum(_dot(x1b, wup_ref[:, c * FF_CHUNK:(c + 1) * FF_CHUNK]), 0.0)
        ff = ff + _dot((h * h).astype(BF16), wdn_ref[c * FF_CHUNK:(c + 1) * FF_CHUNK, :])
    out_ref[...] = _layer_norm(alpha * x1 + ff, g2_ref[...], b2_ref[...])


def _tail(x2, mix, o, wg, wco, wno, wo, g1, b1, wup, wdn, g2, b2, alpha):
    n = x2.shape[0]
    tm = TAIL_TM
    row = lambda i: (i, 0)
    const = lambda i: (0, 0)
    resident = lambda shape: pl.BlockSpec(shape, const, pipeline_mode=pl.Buffered(1))
    return pl.pallas_call(
        functools.partial(_tail_kernel, alpha),
        grid=(n // tm,),
        in_specs=[
            pl.BlockSpec((tm, D_MODEL), row), pl.BlockSpec((tm, CONV_DIM), row), pl.BlockSpec((tm, NSA_DIM), row),
            resident((2 * D_MODEL, D_MODEL)), resident((CONV_DIM, D_MODEL)), resident((NSA_DIM, D_MODEL)),
            resident((D_MODEL, D_MODEL)), resident((1, D_MODEL)), resident((1, D_MODEL)),
            resident((D_MODEL, D_FF)), resident((D_FF, D_MODEL)), resident((1, D_MODEL)), resident((1, D_MODEL)),
        ],
        out_specs=pl.BlockSpec((tm, D_MODEL), row),
        out_shape=jax.ShapeDtypeStruct((n, D_MODEL), F32),
        compiler_params=pltpu.CompilerParams(dimension_semantics=("arbitrary",), vmem_limit_bytes=VMEM_LIMIT),
        name="tail",
    )(x2, mix, o, wg, wco, wno, wo, g1, b1, wup, wdn, g2, b2)


def _rope_tables(S):
    inv = ROPE_THETA ** (-jnp.arange(0, ROPE_DIM, 2, dtype=F32) / ROPE_DIM)
    ang = jnp.arange(S, dtype=F32)[:, None] * inv[None, :]
    cos, sin = jnp.cos(ang), jnp.sin(ang)
    pad_c = jnp.ones((S, HEAD_DIM - ROPE_DIM), F32)
    pad_s = jnp.zeros((S, HEAD_DIM - ROPE_DIM), F32)
    ck = jnp.tile(jnp.concatenate([cos, cos, pad_c], axis=1), (1, N_KV_GROUPS))
    sk = jnp.tile(jnp.concatenate([-sin, sin, pad_s], axis=1), (1, N_KV_GROUPS))
    return ck, sk, cos.T, sin.T


def _overlap_matrix_t(ncp, ns):
    nc = ncp - 1
    i = np.arange(ncp)[None, :]
    j = np.arange(ns)[:, None]
    m = (i * CMP_STRIDE < (j + 1) * SEL_BLOCK) & (i * CMP_STRIDE + CMP_BLOCK > j * SEL_BLOCK) & (i < nc)
    return jnp.asarray(m.astype(np.float32))


def _compress_weights(pe, w1, w2):
    eye = jnp.eye(N_KV_GROUPS, dtype=F32)
    peh = pe.reshape(2, CMP_STRIDE, 1, HEAD_DIM)
    pebig = jnp.broadcast_to(peh, (2, CMP_STRIDE, N_KV_GROUPS, HEAD_DIM)).reshape(2, 1, CMP_STRIDE * KV_DIM)
    pebig = jnp.broadcast_to(pebig, (2, SUBLANES, CMP_STRIDE * KV_DIM))
    w2big = jnp.einsum('ab,hd->ahbd', eye, w2).reshape(N_KV_GROUPS * CMP_HIDDEN, KV_DIM)
    return pebig.astype(BF16), w1.astype(BF16), w2big.astype(BF16)


def _layer(x2, B, S, w_in, conv_w, w_conv_out, pe_k, wk1, wk2, pe_v, wv1, wv2, w_nsa_out, w_o,
           ln1_g, ln1_b, w_up, w_down, ln2_g, ln2_b, alpha):
    ncp = S // CMP_STRIDE
    ns = S // SEL_BLOCK
    wstd, wg, wtr = _wprep(jnp.swapaxes(w_in, 0, 1))
    ck, sk, cq, sq = _rope_tables(S)

    mix, kch, vch, ksel, kwin, qT, qrT, vselT, vwinT, gT = _proj(
        x2, wstd, wtr, conv_w.reshape(CONV_WIDTH, CONV_DIM), ck, sk, cq, sq, B, S)

    pek, wk1b, wk2b = _compress_weights(pe_k, wk1, wk2)
    pev, wv1b, wv2b = _compress_weights(pe_v, wv1, wv2)
    kc, vcT = _compress(kch, vch, pek, pev, wk1b, wv1b, wk2b, wv2b.T, B, ncp)

    tk = ATT_TK
    o = _attn(qT, qrT, kc, vcT, ksel.reshape(B, S // tk, tk, KV_DIM), vselT,
              kwin.reshape(B, S // tk, tk, KV_DIM), vwinT, gT, _overlap_matrix_t(ncp, ns), B, S)

    row = lambda v: v.reshape(1, D_MODEL).astype(F32)
    return _tail(x2, mix, o, wg, w_conv_out.astype(BF16), w_nsa_out.astype(BF16), w_o.astype(BF16),
                 row(ln1_g), row(ln1_b), w_up.astype(BF16), w_down.astype(BF16), row(ln2_g), row(ln2_b), alpha)


def kernel(x, w_in, conv_w, w_conv_out, pe_k_cmp, w_k_cmp1, w_k_cmp2, pe_v_cmp, w_v_cmp1, w_v_cmp2,
           w_nsa_out, w_o, ln1_g, ln1_b, w_up, w_down, ln2_g, ln2_b):
    B, S, D = x.shape
    depth = w_in.shape[0]
    assert D == D_MODEL and w_in.shape[2] == O_END
    assert S % PROJ_TM == 0 and S % ATT_TQ == 0 and (B * S) % TAIL_TM == 0 and S >= WINDOW
    alpha = float((2 * depth) ** 0.25)
    x2 = x.reshape(B * S, D)
    for l in range(depth):
        x2 = _layer(x2, B, S, w_in[l], conv_w[l], w_conv_out[l], pe_k_cmp[l], w_k_cmp1[l], w_k_cmp2[l],
                    pe_v_cmp[l], w_v_cmp1[l], w_v_cmp2[l], w_nsa_out[l], w_o[l], ln1_g[l], ln1_b[l],
                    w_up[l], w_down[l], ln2_g[l], ln2_b[l], alpha)
    return x2.reshape(B, S, D)
```

```python
import functools

import numpy as np
import jax
import jax.numpy as jnp
from jax import lax
from jax.experimental import pallas as pl
from jax.experimental.pallas import tpu as pltpu

F32 = jnp.float32
BF16 = jnp.bfloat16

D_MODEL = 1024
CONV_DIM = D_MODEL // 2
CONV_WIDTH = 3
N_HEADS = 8
HEAD_DIM = 64
N_KV_GROUPS = 2
HEADS_PER_GROUP = N_HEADS // N_KV_GROUPS
NSA_DIM = N_HEADS * HEAD_DIM
KV_DIM = N_KV_GROUPS * HEAD_DIM
ROPE_DIM = HEAD_DIM // 4
ROPE_HALF = ROPE_DIM // 2
ROPE_THETA = 500000.0
CMP_BLOCK = 32
CMP_STRIDE = 16
CMP_HIDDEN = 2 * HEAD_DIM
SEL_BLOCK = 64
N_SEL = 16
WINDOW = 512
N_NSA_BRANCHES = 3
D_FF = 4 * D_MODEL
LN_EPS = 1e-5
NEG = -1e30
FORCE = 1e9
SCALE = HEAD_DIM ** -0.5
LOG2E = 1.4426950408889634

SUBLANES = 8
LANES = 128
BF16_ROWS = 16
V_ROWS = HEAD_DIM + BF16_ROWS

_SPLITS = (CONV_DIM, CONV_DIM, CONV_DIM, NSA_DIM, KV_DIM, KV_DIM, KV_DIM, KV_DIM, KV_DIM, KV_DIM,
           N_HEADS * N_NSA_BRANCHES, D_MODEL, D_MODEL)
_OFFS = np.concatenate([[0], np.cumsum(_SPLITS)]).tolist()
(O_H, O_B, O_C, O_Q, O_KCMP, O_VCMP, O_KSEL, O_VSEL, O_KWIN, O_VWIN, O_GATE, O_GCONV, O_GNSA, O_END) = _OFFS

GATE_ROWS = 16
N_STD = 3 * CONV_DIM + 4 * KV_DIM
N_TR = NSA_DIM + 2 * KV_DIM + N_KV_GROUPS * GATE_ROWS

PROJ_TM = 512
ATT_TQ = 256
ATT_TK = 256
TAIL_TM = 512
FF_CHUNK = 1024
VMEM_LIMIT = 56 * 1024 * 1024


def _dot(a, b):
    return jnp.dot(a, b, preferred_element_type=F32)


def _dot_nt(a, b):
    return lax.dot_general(a, b, (((1,), (1,)), ((), ())), preferred_element_type=F32)


WPREP_COLS = 256


def _wprep_kernel(wT_ref, wstdT_ref, wgT_ref, wtr_ref):
    r = 0
    for a, b in ((O_H, O_Q), (O_KCMP, O_KSEL), (O_KSEL, O_VSEL), (O_KWIN, O_VWIN)):
        wstdT_ref[r:r + b - a, :] = wT_ref[a:b, :].astype(BF16)
        r += b - a
    wgT_ref[...] = wT_ref[O_GCONV:O_END, :].astype(BF16)
    r = 0
    for a, b in ((O_Q, O_KCMP), (O_VSEL, O_KWIN), (O_VWIN, O_GATE)):
        wtr_ref[r:r + b - a, :] = wT_ref[a:b, :].astype(BF16)
        r += b - a
    gates = wT_ref[O_GATE:O_GCONV, :]
    rows = []
    for g in range(N_KV_GROUPS):
        for br in range(N_NSA_BRANCHES):
            for z in range(HEADS_PER_GROUP):
                src = (g * HEADS_PER_GROUP + z) * N_NSA_BRANCHES + br
                rows.append(gates[src:src + 1, :])
        rows.append(jnp.zeros((GATE_ROWS - N_NSA_BRANCHES * HEADS_PER_GROUP, gates.shape[1]), F32))
    wtr_ref[r:r + N_KV_GROUPS * GATE_ROWS, :] = jnp.concatenate(rows, axis=0).astype(BF16)


def _wprep(wT):
    d = wT.shape[1]
    cb = WPREP_COLS
    assert d % cb == 0 and O_GCONV % SUBLANES == 0
    strip = lambda rows: pl.BlockSpec((rows, cb), lambda i: (0, i))
    return pl.pallas_call(
        _wprep_kernel,
        grid=(d // cb,),
        in_specs=[strip(O_END)],
        out_specs=(strip(N_STD), strip(2 * D_MODEL), strip(N_TR)),
        out_shape=(jax.ShapeDtypeStruct((N_STD, d), BF16), jax.ShapeDtypeStruct((2 * D_MODEL, d), BF16),
                   jax.ShapeDtypeStruct((N_TR, d), BF16)),
        compiler_params=pltpu.CompilerParams(dimension_semantics=("arbitrary",), vmem_limit_bytes=VMEM_LIMIT),
        name="wprep",
    )(wT)


def _proj_kernel(x_ref, wstd_ref, wtr_ref, convw_ref, cosk_ref, sink_ref, cosq_ref, sinq_ref,
                 mix_ref, kch_ref, vch_ref, ksel_ref, kwin_ref,
                 qT_ref, qrT_ref, vselT_ref, vwinT_ref, gT_ref, ubuf, kvbuf):
    tm = x_ref.shape[0]

    @pl.when(pl.program_id(1) == 0)
    def _():
        ubuf[...] = jnp.zeros(ubuf.shape, F32)

    xb = x_ref[...].astype(BF16)
    hbc = _dot_nt(xb, wstd_ref[0:3 * CONV_DIM, :])
    kk = _dot_nt(xb, wstd_ref[3 * CONV_DIM:N_STD, :])
    yT = _dot_nt(wtr_ref[...], xb)

    u = hbc[:, 2 * CONV_DIM:3 * CONV_DIM] * hbc[:, 0:CONV_DIM]
    prev = ubuf[...]
    ubuf[...] = u[tm - SUBLANES:tm, :]
    head_rows = lax.broadcasted_iota(jnp.int32, (SUBLANES, CONV_DIM), 0)

    def shifted(k):
        r = pltpu.roll(u, k, 0)
        first = jnp.where(head_rows < k, pltpu.roll(prev, k, 0), r[0:SUBLANES, :])
        return jnp.concatenate([first, r[SUBLANES:, :]], axis=0)

    cw = convw_ref[...]
    y = cw[0:1, :] * shifted(2) + cw[1:2, :] * shifted(1) + cw[2:3, :] * u
    mix_ref[...] = (hbc[:, CONV_DIM:2 * CONV_DIM] * y).astype(BF16)

    for j, out in enumerate((kch_ref, vch_ref)):
        kvbuf[j] = kk[:, j * KV_DIM:(j + 1) * KV_DIM]
        for r in range(CMP_STRIDE):
            rows = kvbuf[j, pl.ds(r, tm // CMP_STRIDE, stride=CMP_STRIDE), :]
            out[:, r * KV_DIM:(r + 1) * KV_DIM] = rows.astype(BF16)
    lane = lax.broadcasted_iota(jnp.int32, (tm, KV_DIM), 1)
    first_half = (lane % ROPE_DIM) < ROPE_HALF
    ck = cosk_ref[...]
    sk = sink_ref[...]
    for j, out in ((2, ksel_ref), (3, kwin_ref)):
        k = kk[:, j * KV_DIM:(j + 1) * KV_DIM]
        partner = jnp.where(first_half, pltpu.roll(k, KV_DIM - ROPE_HALF, 1), pltpu.roll(k, ROPE_HALF, 1))
        out[...] = (k * ck + partner * sk).astype(BF16)

    q = yT[0:NSA_DIM, :] * (SCALE * LOG2E)
    qT_ref[...] = q.astype(BF16)
    qrT_ref[...] = q.astype(BF16)
    cq = cosq_ref[...]
    sq = sinq_ref[...]
    for h in range(N_HEADS):
        r0 = h * HEAD_DIM
        x1 = q[r0:r0 + ROPE_HALF, :]
        x2 = q[r0 + ROPE_HALF:r0 + ROPE_DIM, :]
        rot = jnp.concatenate([x1 * cq - x2 * sq, x2 * cq + x1 * sq], axis=0)
        qrT_ref[r0:r0 + ROPE_DIM, :] = rot.astype(BF16)
    tk = vselT_ref.shape[-1]
    ones = jnp.ones((BF16_ROWS, tk), BF16)
    for j, out in enumerate((vselT_ref, vwinT_ref)):
        for i in range(tm // tk):
            for g in range(N_KV_GROUPS):
                r0 = NSA_DIM + j * KV_DIM + g * HEAD_DIM
                out[i, g * V_ROWS:g * V_ROWS + HEAD_DIM, :] = yT[r0:r0 + HEAD_DIM, i * tk:(i + 1) * tk].astype(BF16)
                out[i, g * V_ROWS + HEAD_DIM:(g + 1) * V_ROWS, :] = ones
    gT_ref[...] = jax.nn.sigmoid(yT[NSA_DIM + 2 * KV_DIM:N_TR, :])


def _proj(x2, wstd, wtr, convw, cosk, sink, cosq, sinq, B, S):
    tm, tk = PROJ_TM, ATT_TK
    ns = S // tm
    row = lambda b, s: (b * ns + s, 0)
    const = lambda b, s: (0, 0)
    tok = lambda c, dt: jax.ShapeDtypeStruct((B * S, c), dt)
    cw = CMP_STRIDE * KV_DIM
    chunked = jax.ShapeDtypeStruct((B, S // CMP_STRIDE, cw), BF16)
    chunk_spec = pl.BlockSpec((None, tm // CMP_STRIDE, cw), lambda b, s: (b, s, 0))
    out_shape = (
        tok(CONV_DIM, BF16), chunked, chunked, tok(KV_DIM, BF16), tok(KV_DIM, BF16),
        jax.ShapeDtypeStruct((B, NSA_DIM, S), BF16), jax.ShapeDtypeStruct((B, NSA_DIM, S), BF16),
        jax.ShapeDtypeStruct((B, S // tk, N_KV_GROUPS * V_ROWS, tk), BF16),
        jax.ShapeDtypeStruct((B, S // tk, N_KV_GROUPS * V_ROWS, tk), BF16),
        jax.ShapeDtypeStruct((B, N_KV_GROUPS * GATE_ROWS, S), F32),
    )
    feat = lambda r: pl.BlockSpec((None, r, tm), lambda b, s: (b, 0, s))
    vt = pl.BlockSpec((None, tm // tk, N_KV_GROUPS * V_ROWS, tk), lambda b, s: (b, s, 0, 0))
    return pl.pallas_call(
        _proj_kernel,
        grid=(B, ns),
        in_specs=[
            pl.BlockSpec((tm, D_MODEL), row),
            pl.BlockSpec((N_STD, D_MODEL), const),
            pl.BlockSpec((N_TR, D_MODEL), const),
            pl.BlockSpec((CONV_WIDTH, CONV_DIM), const),
            pl.BlockSpec((tm, KV_DIM), lambda b, s: (s, 0)),
            pl.BlockSpec((tm, KV_DIM), lambda b, s: (s, 0)),
            pl.BlockSpec((ROPE_HALF, tm), lambda b, s: (0, s)),
            pl.BlockSpec((ROPE_HALF, tm), lambda b, s: (0, s)),
        ],
        out_specs=(
            pl.BlockSpec((tm, CONV_DIM), row), chunk_spec, chunk_spec,
            pl.BlockSpec((tm, KV_DIM), row), pl.BlockSpec((tm, KV_DIM), row),
            feat(NSA_DIM), feat(NSA_DIM), vt, vt, feat(N_KV_GROUPS * GATE_ROWS),
        ),
        out_shape=out_shape,
        scratch_shapes=[pltpu.VMEM((SUBLANES, CONV_DIM), F32), pltpu.VMEM((2, tm, KV_DIM), F32)],
        compiler_params=pltpu.CompilerParams(
            dimension_semantics=("arbitrary", "arbitrary"), vmem_limit_bytes=VMEM_LIMIT),
        name="proj",
    )(x2, wstd, wtr, convw, cosk, sink, cosq, sinq)


def _gelu_tanh(x):
    return x * (0.5 * (1.0 + jnp.tanh(np.sqrt(2.0 / np.pi).astype(np.float32) * (x + 0.044715 * (x * x * x)))))


def _compress_kernel(kch_ref, vch_ref, pek_ref, pev_ref, wk1_ref, wv1_ref, wk2_ref, wv2t_ref, kc_ref, vcT_ref,
                     wk1_big, wv1_big):
    nb, ncp = kch_ref.shape[0], kch_ref.shape[1]
    for w_ref, big in ((wk1_ref, wk1_big), (wv1_ref, wv1_big)):
        big[...] = jnp.zeros(big.shape, BF16)
        for half in range(2):
            for r in range(CMP_STRIDE):
                for g in range(N_KV_GROUPS):
                    r0 = r * KV_DIM + g * HEAD_DIM
                    big[half, r0:r0 + HEAD_DIM, g * CMP_HIDDEN:(g + 1) * CMP_HIDDEN] = (
                        w_ref[half * CMP_STRIDE + r])

    def hidden(ch_ref, pe_ref, big):
        ch = ch_ref[...].reshape(nb * ncp, ch_ref.shape[2])
        a = _dot(ch, big[0])
        b = _dot(ch, big[1])
        pe = _dot(pe_ref[0], big[0]) + _dot(pe_ref[1], big[1])
        b_next = pltpu.roll(b, b.shape[0] - 1, 0)
        return _gelu_tanh(a + b_next + pe[0:1, :]).astype(BF16)

    kc = _dot(hidden(kch_ref, pek_ref, wk1_big), wk2_ref[...]).astype(BF16)
    kc_ref[...] = kc.reshape(nb, ncp, kc.shape[1])
    vT = _dot_nt(wv2t_ref[...], hidden(vch_ref, pev_ref, wv1_big)).astype(BF16)
    for b in range(nb):
        vcT_ref[b] = vT[:, b * ncp:(b + 1) * ncp]


def _compress(kch, vch, pek, pev, wk1, wv1, wk2, wv2t, B, ncp):
    cw = CMP_STRIDE * KV_DIM
    gh = N_KV_GROUPS * CMP_HIDDEN
    full = lambda shape: pl.BlockSpec(shape, lambda i: (0,) * len(shape))
    return pl.pallas_call(
        _compress_kernel,
        grid=(1,),
        in_specs=[
            full((B, ncp, cw)), full((B, ncp, cw)),
            full((2, SUBLANES, cw)), full((2, SUBLANES, cw)),
            full((CMP_BLOCK, HEAD_DIM, CMP_HIDDEN)), full((CMP_BLOCK, HEAD_DIM, CMP_HIDDEN)),
            full((gh, KV_DIM)), full((KV_DIM, gh)),
        ],
        out_specs=(full((B, ncp, KV_DIM)), full((B, KV_DIM, ncp))),
        out_shape=(jax.ShapeDtypeStruct((B, ncp, KV_DIM), BF16), jax.ShapeDtypeStruct((B, KV_DIM, ncp), BF16)),
        scratch_shapes=[pltpu.VMEM((2, cw, gh), BF16), pltpu.VMEM((2, cw, gh), BF16)],
        compiler_params=pltpu.CompilerParams(dimension_semantics=("arbitrary",), vmem_limit_bytes=VMEM_LIMIT),
        name="compress",
    )(kch, vch, pek, pev, wk1, wv1, wk2, wv2t)


def _attn_kernel(n_cast, qT_ref, qrT_ref, kc_ref, vcT_ref, ksel_ref, vselT_ref, kwin_ref, vwinT_ref, gT_ref,
                 mT_ref, wbias_ref, *rest):
    cast_in, o_ref, cast_out = rest[:n_cast], rest[n_cast], rest[n_cast + 1:2 * n_cast + 1]
    selb_scr, q_scr, qr_scr, s_scr, swin_scr, m_scr, acc_scr, oT_scr = rest[2 * n_cast + 1:]

    tq = qT_ref.shape[1]
    tk = ksel_ref.shape[1]
    nsub = tq // LANES
    nchunk = N_HEADS * nsub
    gchunk = HEADS_PER_GROUP * nsub
    gl = HEADS_PER_GROUP * tq
    ns = mT_ref.shape[0]
    bpt = tk // SEL_BLOCK
    p = pl.program_id(1)
    qbase = p * tq

    def chunk(c):
        return slice(c * LANES, (c + 1) * LANES)

    def sub(tsub):
        return slice(tsub * LANES, (tsub + 1) * LANES)

    def stage_q(ref, scr):
        zeros = jnp.zeros((HEAD_DIM, tq), BF16)
        for h in range(N_HEADS):
            qh = ref[h * HEAD_DIM:(h + 1) * HEAD_DIM, :]
            halves = [qh, zeros] if h // HEADS_PER_GROUP == 0 else [zeros, qh]
            scr[:, h * tq:(h + 1) * tq] = jnp.concatenate(halves, axis=0)

    gt = gT_ref[...]

    def gate(br, c):
        h, tsub = divmod(c, nsub)
        g, z = divmod(h, HEADS_PER_GROUP)
        r = g * GATE_ROWS + br * HEADS_PER_GROUP + z
        return gt[r:r + 1, sub(tsub)]

    stage_q(qT_ref, q_scr)
    stage_q(qrT_ref, qr_scr)
    start = p % 2
    n_back = WINDOW // tk
    case = jnp.minimum(p, n_back)
    lo = p - case

    def qk_sel(kt, slot):
        s_all = _dot(ksel_ref[kt], qr_scr[...])
        for c in range(nchunk):
            rb = selb_scr[c // gchunk, kt, 0:bpt, sub(c % nsub)]
            s_scr[slot, :, chunk(c)] = s_all[:, chunk(c)] + jnp.concatenate(
                [jnp.broadcast_to(rb[i:i + 1, :], (SEL_BLOCK, LANES)) for i in range(bpt)], axis=0)

    kc = kc_ref[...]
    mT = mT_ref[...]
    ncp = kc.shape[0]
    s_cmp = _dot(kc, q_scr[...])
    k_win = kwin_ref[pl.ds(lo, n_back + 1)].reshape((n_back + 1) * tk, KV_DIM)
    s_win = _dot(k_win, qr_scr[...])
    for c in range(nchunk):
        swin_scr[:, chunk(c)] = s_win[:, chunk(c)] + wbias_ref[case, :, sub(c % nsub)]

    imp = []
    for g in range(N_KV_GROUPS):
        vcT = vcT_ref[g * HEAD_DIM:(g + 1) * HEAD_DIM, :]
        imp_parts = []
        for tsub in range(nsub):
            n_idx = lax.broadcasted_iota(jnp.int32, (ncp, LANES), 0)
            t_idx = qbase + tsub * LANES + lax.broadcasted_iota(jnp.int32, (ncp, LANES), 1)
            ok = (n_idx * CMP_STRIDE + (CMP_BLOCK - 1)) <= t_idx
            bias = jnp.where(ok, 0.0, NEG).astype(F32)
            keep = ok.astype(F32)
            p_sum = None
            for z in range(HEADS_PER_GROUP):
                c = (g * HEADS_PER_GROUP + z) * nsub + tsub
                s = s_cmp[:, chunk(c)] + bias
                e = jnp.exp2(s - jnp.max(s, axis=0, keepdims=True)) * keep
                inv = 1.0 / jnp.maximum(jnp.sum(e, axis=0, keepdims=True), 1e-30)
                oT_scr[:, chunk(c)] = _dot(vcT, e.astype(BF16)) * (inv * gate(0, c))
                p_sum = e * inv if p_sum is None else p_sum + e * inv
            imp_parts.append(jnp.dot(mT, p_sum, preferred_element_type=F32, precision=lax.Precision.HIGHEST))
        imp.append(jnp.concatenate(imp_parts, axis=1))

    j_idx = lax.broadcasted_iota(jnp.int32, (ns, tq), 0)
    cur = (qbase + lax.broadcasted_iota(jnp.int32, (ns, tq), 1)) // SEL_BLOCK
    forced = (j_idx == 0) | (j_idx == cur) | (j_idx == cur - 1)
    for g in range(N_KV_GROUPS):
        impv = jnp.where(j_idx <= cur, jnp.where(forced, FORCE, imp[g]), NEG)
        groups = [impv[a:a + SUBLANES, :] for a in range(0, ns, SUBLANES)]
        sub_idx = lax.broadcasted_iota(jnp.int32, (SUBLANES, tq), 0)
        cnts = [jnp.zeros((SUBLANES, tq), F32) for _ in groups]
        for jp in range(ns):
            row = impv[jp:jp + 1, :]
            for gi, grp in enumerate(groups):
                if gi > jp // SUBLANES:
                    beats = row >= grp
                elif gi < jp // SUBLANES:
                    beats = row > grp
                else:
                    beats = (row > grp) | ((row == grp) & (sub_idx > jp % SUBLANES))
                cnts[gi] = cnts[gi] + beats.astype(F32)
        cnt = jnp.concatenate(cnts, axis=0)
        sel = (cnt < float(min(N_SEL, ns))) & (impv > 0.5 * NEG)
        selb = jnp.where(sel, 0.0, NEG).astype(F32)
        for kt in range(ns // bpt):
            selb_scr[g, kt, 0:bpt, :] = selb[kt * bpt:(kt + 1) * bpt, :]
    qk_sel(start, 0)

    SEL, WIN = 0, 1
    m_scr[...] = jnp.full(m_scr.shape, NEG, F32)
    acc_scr[...] = jnp.zeros(acc_scr.shape, F32)

    def softmax_pv(st, load_s, vT_t, tile_bias=None):
        for g in range(N_KV_GROUPS):
            es, alphas = [], []
            for cg in range(gchunk):
                c = g * gchunk + cg

                def scores():
                    s = load_s(c)
                    return s if tile_bias is None else s + tile_bias(c % nsub)

                m_old = m_scr[st, :, chunk(c)]
                m_new = jnp.maximum(m_old, jnp.max(scores(), axis=0, keepdims=True))
                m_scr[st, :, chunk(c)] = m_new
                es.append(jnp.exp2(scores() - m_new).astype(BF16))
                alphas.append(jnp.exp2(m_old - m_new))
            lanes = slice(g * gl, (g + 1) * gl)
            pv = _dot(vT_t[g * V_ROWS:(g + 1) * V_ROWS, :], jnp.concatenate(es, axis=1))
            acc_scr[st, :, lanes] = jnp.concatenate(alphas, axis=1) * acc_scr[st, :, lanes] + pv

    def finish(st, br):
        for c in range(nchunk):
            acc = acc_scr[st, :, chunk(c)]
            inv = 1.0 / jnp.maximum(acc[HEAD_DIM:HEAD_DIM + 1, :], 1e-30)
            oT_scr[:, chunk(c)] += acc[0:HEAD_DIM, :] * (inv * gate(br, c))

    for i in range(n_back, -1, -1):
        rows = slice(i * tk, (i + 1) * tk)
        softmax_pv(WIN, lambda c: swin_scr[rows, chunk(c)], vwinT_ref[lo + i])
    finish(WIN, 2)

    def sp_sel(kt, slot, tile_bias=None):
        softmax_pv(SEL, lambda c: s_scr[slot, :, chunk(c)], vselT_ref[kt], tile_bias)

    @pl.when(start == 1)
    def _():
        qk_sel(0, 1)
        sp_sel(0, 1)

    def pair_body(j, carry):
        a = start + 2 * j
        qk_sel(a + 1, 1)
        sp_sel(a, 0)
        qk_sel(a + 2, 0)
        sp_sel(a + 1, 1)
        return carry

    lax.fori_loop(0, p // 2, pair_body, 0)
    sp_sel(p, 0, lambda tsub: wbias_ref[n_back, n_back * tk:(n_back + 1) * tk, sub(tsub)])
    finish(SEL, 1)

    for tsub in range(nsub):
        for i in range(N_HEADS // 2):
            pair = jnp.concatenate([oT_scr[:, chunk((2 * i) * nsub + tsub)],
                                    oT_scr[:, chunk((2 * i + 1) * nsub + tsub)]], axis=0)
            o_ref[sub(tsub), i * 2 * HEAD_DIM:(i + 1) * 2 * HEAD_DIM] = pair.T.astype(BF16)

    for src, dst in zip(cast_in, cast_out):
        dst[...] = src[...].astype(BF16)


def _window_bias(t, n_back):
    r = np.arange(t)[:, None]
    c = np.arange(t)[None, :]
    causal = np.where(r <= c, 0.0, NEG)
    edge = np.where(r > c, 0.0, NEG)
    out = np.zeros((n_back + 1, (n_back + 1) * t, t), np.float32)
    for case in range(n_back + 1):
        for i in range(n_back + 1):
            back = case - i
            tile = NEG if back < 0 else causal if back == 0 else edge if back == n_back else 0.0
            out[case, i * t:(i + 1) * t, :] = tile
    return jnp.asarray(out)


def _attn(qT, qrT, kc, vcT, ksel, vselT, kwin, vwinT, gT, mT, B, S, cast=()):
    tq, tk = ATT_TQ, ATT_TK
    assert tq == tk and WINDOW % tk == 0 and tk % SEL_BLOCK == 0 and tq % LANES == 0
    assert tk // SEL_BLOCK <= SUBLANES
    n_back = WINDOW // tk
    assert S // tk > n_back
    P = S // tq
    ncp = kc.shape[1]
    ns = S // SEL_BLOCK
    nq = N_HEADS * tq
    qspec = pl.BlockSpec((None, NSA_DIM, tq), lambda b, p: (b, 0, p))
    kspec = pl.BlockSpec((None, S // tk, tk, KV_DIM), lambda b, p: (b, 0, 0, 0))
    vspec = pl.BlockSpec((None, S // tk, N_KV_GROUPS * V_ROWS, tk), lambda b, p: (b, 0, 0, 0))
    steps = B * P
    assert all(w.shape[0] % (steps * BF16_ROWS) == 0 for w in cast)
    slab = lambda w: pl.BlockSpec((w.shape[0] // steps, w.shape[1]), lambda b, p: (b * P + p, 0))
    out = pl.pallas_call(
        functools.partial(_attn_kernel, len(cast)),
        grid=(B, P),
        in_specs=[
            qspec, qspec,
            pl.BlockSpec((None, ncp, KV_DIM), lambda b, p: (b, 0, 0)),
            pl.BlockSpec((None, KV_DIM, ncp), lambda b, p: (b, 0, 0)),
            kspec, vspec, kspec, vspec,
            pl.BlockSpec((None, N_KV_GROUPS * GATE_ROWS, tq), lambda b, p: (b, 0, p)),
            pl.BlockSpec((ns, ncp), lambda b, p: (0, 0)),
            pl.BlockSpec((n_back + 1, (n_back + 1) * tk, tq), lambda b, p: (0, 0, 0)),
        ] + [slab(w) for w in cast],
        out_specs=[pl.BlockSpec((tq, NSA_DIM), lambda b, p: (b * P + p, 0))] + [slab(w) for w in cast],
        out_shape=[jax.ShapeDtypeStruct((B * S, NSA_DIM), BF16)]
        + [jax.ShapeDtypeStruct(w.shape, BF16) for w in cast],
        scratch_shapes=[
            pltpu.VMEM((N_KV_GROUPS, S // tk, SUBLANES, tq), F32),
            pltpu.VMEM((KV_DIM, nq), BF16),
            pltpu.VMEM((KV_DIM, nq), BF16),
            pltpu.VMEM((2, tk, nq), F32),
            pltpu.VMEM(((n_back + 1) * tk, nq), F32),
            pltpu.VMEM((2, 1, nq), F32),
            pltpu.VMEM((2, V_ROWS, nq), F32),
            pltpu.VMEM((HEAD_DIM, nq), F32),
        ],
        compiler_params=pltpu.CompilerParams(
            dimension_semantics=("arbitrary", "arbitrary"), vmem_limit_bytes=VMEM_LIMIT),
        name="attn",
    )(qT, qrT, kc, vcT, ksel, vselT, kwin, vwinT, gT, mT, _window_bias(tk, n_back), *cast)
    return out[0], tuple(out[1:])


def _layer_norm(z, g, b):
    mu = jnp.mean(z, axis=-1, keepdims=True)
    zc = z - mu
    var = jnp.mean(zc * zc, axis=-1, keepdims=True)
    return zc * lax.rsqrt(var + LN_EPS) * g + b


def _tail_kernel(alpha, x_ref, mix_ref, o_ref, wg_ref, wco_ref, wno_ref, wo_ref, g1_ref, b1_ref,
                 wup_ref, wdn_ref, g2_ref, b2_ref, out_ref):
    x = x_ref[...]
    xb = x.astype(BF16)
    y_conv = _dot(mix_ref[...], wco_ref[...])
    y_nsa = _dot(o_ref[...], wno_ref[...])
    mixed = (jax.nn.sigmoid(_dot_nt(xb, wg_ref[0:D_MODEL, :])) * y_conv
             + jax.nn.sigmoid(_dot_nt(xb, wg_ref[D_MODEL:2 * D_MODEL, :])) * y_nsa)
    x1 = _layer_norm(alpha * x + _dot(mixed.astype(BF16), wo_ref[...]), g1_ref[...], b1_ref[...])
    x1b = x1.astype(BF16)
    ff = jnp.zeros_like(x1)
    for c in range(D_FF // FF_CHUNK):
        h = jnp.maximum(_dot(x1b, wup_ref[:, c * FF_CHUNK:(c + 1) * FF_CHUNK]), 0.0)
        ff = ff + _dot((h * h).astype(BF16), wdn_ref[c * FF_CHUNK:(c + 1) * FF_CHUNK, :])
    out_ref[...] = _layer_norm(alpha * x1 + ff, g2_ref[...], b2_ref[...])


def _tail(x2, mix, o, wg, wco, wno, wo, g1, b1, wup, wdn, g2, b2, alpha):
    n = x2.shape[0]
    tm = TAIL_TM
    row = lambda i: (i, 0)
    const = lambda i: (0, 0)
    resident = lambda shape: pl.BlockSpec(shape, const, pipeline_mode=pl.Buffered(1))
    return pl.pallas_call(
        functools.partial(_tail_kernel, alpha),
        grid=(n // tm,),
        in_specs=[
            pl.BlockSpec((tm, D_MODEL), row), pl.BlockSpec((tm, CONV_DIM), row), pl.BlockSpec((tm, NSA_DIM), row),
            resident((2 * D_MODEL, D_MODEL)), resident((CONV_DIM, D_MODEL)), resident((NSA_DIM, D_MODEL)),
            resident((D_MODEL, D_MODEL)), resident((1, D_MODEL)), resident((1, D_MODEL)),
            resident((D_MODEL, D_FF)), resident((D_FF, D_MODEL)), resident((1, D_MODEL)), resident((1, D_MODEL)),
        ],
        out_specs=pl.BlockSpec((tm, D_MODEL), row),
        out_shape=jax.ShapeDtypeStruct((n, D_MODEL), F32),
        compiler_params=pltpu.CompilerParams(dimension_semantics=("arbitrary",), vmem_limit_bytes=VMEM_LIMIT),
        name="tail",
    )(x2, mix, o, wg, wco, wno, wo, g1, b1, wup, wdn, g2, b2)


def _rope_tables(S):
    inv = ROPE_THETA ** (-jnp.arange(0, ROPE_DIM, 2, dtype=F32) / ROPE_DIM)
    ang = jnp.arange(S, dtype=F32)[:, None] * inv[None, :]
    cos, sin = jnp.cos(ang), jnp.sin(ang)
    pad_c = jnp.ones((S, HEAD_DIM - ROPE_DIM), F32)
    pad_s = jnp.zeros((S, HEAD_DIM - ROPE_DIM), F32)
    ck = jnp.tile(jnp.concatenate([cos, cos, pad_c], axis=1), (1, N_KV_GROUPS))
    sk = jnp.tile(jnp.concatenate([-sin, sin, pad_s], axis=1), (1, N_KV_GROUPS))
    return ck, sk, cos.T, sin.T


def _overlap_matrix_t(ncp, ns):
    nc = ncp - 1
    i = np.arange(ncp)[None, :]
    j = np.arange(ns)[:, None]
    m = (i * CMP_STRIDE < (j + 1) * SEL_BLOCK) & (i * CMP_STRIDE + CMP_BLOCK > j * SEL_BLOCK) & (i < nc)
    return jnp.asarray(m.astype(np.float32))


def _compress_weights(pe, w1, w2):
    eye = jnp.eye(N_KV_GROUPS, dtype=F32)
    peh = pe.reshape(2, CMP_STRIDE, 1, HEAD_DIM)
    pebig = jnp.broadcast_to(peh, (2, CMP_STRIDE, N_KV_GROUPS, HEAD_DIM)).reshape(2, 1, CMP_STRIDE * KV_DIM)
    pebig = jnp.broadcast_to(pebig, (2, SUBLANES, CMP_STRIDE * KV_DIM))
    w2big = jnp.einsum('ab,hd->ahbd', eye, w2).reshape(N_KV_GROUPS * CMP_HIDDEN, KV_DIM)
    return pebig.astype(BF16), w1.astype(BF16), w2big.astype(BF16)


def _layer(x2, B, S, w_in, conv_w, w_conv_out, pe_k, wk1, wk2, pe_v, wv1, wv2, w_nsa_out, w_o,
           ln1_g, ln1_b, w_up, w_down, ln2_g, ln2_b, alpha):
    ncp = S // CMP_STRIDE
    ns = S // SEL_BLOCK
    wstd, wg, wtr = _wprep(jnp.swapaxes(w_in, 0, 1))
    ck, sk, cq, sq = _rope_tables(S)

    mix, kch, vch, ksel, kwin, qT, qrT, vselT, vwinT, gT = _proj(
        x2, wstd, wtr, conv_w.reshape(CONV_WIDTH, CONV_DIM), ck, sk, cq, sq, B, S)

    pek, wk1b, wk2b = _compress_weights(pe_k, wk1, wk2)
    pev, wv1b, wv2b = _compress_weights(pe_v, wv1, wv2)
    kc, vcT = _compress(kch, vch, pek, pev, wk1b, wv1b, wk2b, wv2b.T, B, ncp)

    tk = ATT_TK
    steps = B * (S // ATT_TQ)
    big = (w_up, w_down, w_o)
    in_attn = all(w.shape[0] % (steps * BF16_ROWS) == 0 for w in big)
    o, cast = _attn(qT, qrT, kc, vcT, ksel.reshape(B, S // tk, tk, KV_DIM), vselT,
                    kwin.reshape(B, S // tk, tk, KV_DIM), vwinT, gT, _overlap_matrix_t(ncp, ns), B, S,
                    cast=big if in_attn else ())
    w_up_b, w_down_b, w_o_b = cast if in_attn else tuple(w.astype(BF16) for w in big)

    row = lambda v: v.reshape(1, D_MODEL).astype(F32)
    return _tail(x2, mix, o, wg, w_conv_out.astype(BF16), w_nsa_out.astype(BF16), w_o_b,
                 row(ln1_g), row(ln1_b), w_up_b, w_down_b, row(ln2_g), row(ln2_b), alpha)


def kernel(x, w_in, conv_w, w_conv_out, pe_k_cmp, w_k_cmp1, w_k_cmp2, pe_v_cmp, w_v_cmp1, w_v_cmp2,
           w_nsa_out, w_o, ln1_g, ln1_b, w_up, w_down, ln2_g, ln2_b):
    B, S, D = x.shape
    depth = w_in.shape[0]
    assert D == D_MODEL and w_in.shape[2] == O_END
    assert S % PROJ_TM == 0 and S % ATT_TQ == 0 and (B * S) % TAIL_TM == 0 and S >= WINDOW
    alpha = float((2 * depth) ** 0.25)
    x2 = x.reshape(B * S, D)
    for l in range(depth):
        x2 = _layer(x2, B, S, w_in[l], conv_w[l], w_conv_out[l], pe_k_cmp[l], w_k_cmp1[l], w_k_cmp2[l],
                    pe_v_cmp[l], w_v_cmp1[l], w_v_cmp2[l], w_nsa_out[l], w_o[l], ln1_g[l], ln1_b[l],
                    w_up[l], w_down[l], ln2_g[l], ln2_b[l], alpha)
    return x2.reshape(B, S, D)
```

```python
import functools

import numpy as np
import jax
import jax.numpy as jnp
from jax import lax
from jax.experimental import pallas as pl
from jax.experimental.pallas import tpu as pltpu

F32 = jnp.float32
BF16 = jnp.bfloat16

D_MODEL = 1024
CONV_DIM = D_MODEL // 2
CONV_WIDTH = 3
N_HEADS = 8
HEAD_DIM = 64
N_KV_GROUPS = 2
HEADS_PER_GROUP = N_HEADS // N_KV_GROUPS
NSA_DIM = N_HEADS * HEAD_DIM
KV_DIM = N_KV_GROUPS * HEAD_DIM
ROPE_DIM = HEAD_DIM // 4
ROPE_HALF = ROPE_DIM // 2
ROPE_THETA = 500000.0
CMP_BLOCK = 32
CMP_STRIDE = 16
CMP_HIDDEN = 2 * HEAD_DIM
SEL_BLOCK = 64
N_SEL = 16
WINDOW = 512
N_NSA_BRANCHES = 3
D_FF = 4 * D_MODEL
LN_EPS = 1e-5
NEG = -1e30
FORCE = 1e9
SCALE = HEAD_DIM ** -0.5
LOG2E = 1.4426950408889634

SUBLANES = 8
LANES = 128
BF16_ROWS = 16
V_ROWS = HEAD_DIM + BF16_ROWS

_SPLITS = (CONV_DIM, CONV_DIM, CONV_DIM, NSA_DIM, KV_DIM, KV_DIM, KV_DIM, KV_DIM, KV_DIM, KV_DIM,
           N_HEADS * N_NSA_BRANCHES, D_MODEL, D_MODEL)
_OFFS = np.concatenate([[0], np.cumsum(_SPLITS)]).tolist()
(O_H, O_B, O_C, O_Q, O_KCMP, O_VCMP, O_KSEL, O_VSEL, O_KWIN, O_VWIN, O_GATE, O_GCONV, O_GNSA, O_END) = _OFFS

GATE_ROWS = 16
N_STD = 3 * CONV_DIM + 4 * KV_DIM
N_TR = NSA_DIM + 2 * KV_DIM + N_KV_GROUPS * GATE_ROWS

PROJ_TM = 512
ATT_TQ = 256
ATT_TK = 256
TAIL_TM = 512
TAIL_SUB = 256
FF_CHUNK = 1024
VMEM_LIMIT = 56 * 1024 * 1024


def _dot(a, b):
    return jnp.dot(a, b, preferred_element_type=F32)


def _dot_nt(a, b):
    return lax.dot_general(a, b, (((1,), (1,)), ((), ())), preferred_element_type=F32)


WPREP_COLS = 256


def _wprep_kernel(wT_ref, wstdT_ref, wgT_ref, wtr_ref):
    r = 0
    for a, b in ((O_H, O_Q), (O_KCMP, O_KSEL), (O_KSEL, O_VSEL), (O_KWIN, O_VWIN)):
        wstdT_ref[r:r + b - a, :] = wT_ref[a:b, :].astype(BF16)
        r += b - a
    wgT_ref[...] = wT_ref[O_GCONV:O_END, :].astype(BF16)
    r = 0
    for a, b in ((O_Q, O_KCMP), (O_VSEL, O_KWIN), (O_VWIN, O_GATE)):
        wtr_ref[r:r + b - a, :] = wT_ref[a:b, :].astype(BF16)
        r += b - a
    gates = wT_ref[O_GATE:O_GCONV, :]
    rows = []
    for g in range(N_KV_GROUPS):
        for br in range(N_NSA_BRANCHES):
            for z in range(HEADS_PER_GROUP):
                src = (g * HEADS_PER_GROUP + z) * N_NSA_BRANCHES + br
                rows.append(gates[src:src + 1, :])
        rows.append(jnp.zeros((GATE_ROWS - N_NSA_BRANCHES * HEADS_PER_GROUP, gates.shape[1]), F32))
    wtr_ref[r:r + N_KV_GROUPS * GATE_ROWS, :] = jnp.concatenate(rows, axis=0).astype(BF16)


def _wprep(wT):
    d = wT.shape[1]
    cb = WPREP_COLS
    assert d % cb == 0 and O_GCONV % SUBLANES == 0
    strip = lambda rows: pl.BlockSpec((rows, cb), lambda i: (0, i))
    return pl.pallas_call(
        _wprep_kernel,
        grid=(d // cb,),
        in_specs=[strip(O_END)],
        out_specs=(strip(N_STD), strip(2 * D_MODEL), strip(N_TR)),
        out_shape=(jax.ShapeDtypeStruct((N_STD, d), BF16), jax.ShapeDtypeStruct((2 * D_MODEL, d), BF16),
                   jax.ShapeDtypeStruct((N_TR, d), BF16)),
        compiler_params=pltpu.CompilerParams(dimension_semantics=("arbitrary",), vmem_limit_bytes=VMEM_LIMIT),
        name="wprep",
    )(wT)


def _proj_kernel(x_ref, wstd_ref, wtr_ref, convw_ref, cosk_ref, sink_ref, cosq_ref, sinq_ref,
                 mix_ref, kch_ref, vch_ref, ksel_ref, kwin_ref,
                 qT_ref, qrT_ref, vselT_ref, vwinT_ref, gT_ref, ubuf, kvbuf):
    tm = x_ref.shape[0]

    @pl.when(pl.program_id(1) == 0)
    def _():
        ubuf[...] = jnp.zeros(ubuf.shape, F32)

    xb = x_ref[...].astype(BF16)
    hbc = _dot_nt(xb, wstd_ref[0:3 * CONV_DIM, :])
    kk = _dot_nt(xb, wstd_ref[3 * CONV_DIM:N_STD, :])
    yT = _dot_nt(wtr_ref[...], xb)

    u = hbc[:, 2 * CONV_DIM:3 * CONV_DIM] * hbc[:, 0:CONV_DIM]
    prev = ubuf[...]
    ubuf[...] = u[tm - SUBLANES:tm, :]
    head_rows = lax.broadcasted_iota(jnp.int32, (SUBLANES, CONV_DIM), 0)

    def shifted(k):
        r = pltpu.roll(u, k, 0)
        first = jnp.where(head_rows < k, pltpu.roll(prev, k, 0), r[0:SUBLANES, :])
        return jnp.concatenate([first, r[SUBLANES:, :]], axis=0)

    cw = convw_ref[...]
    y = cw[0:1, :] * shifted(2) + cw[1:2, :] * shifted(1) + cw[2:3, :] * u
    mix_ref[...] = (hbc[:, CONV_DIM:2 * CONV_DIM] * y).astype(BF16)

    for j, out in enumerate((kch_ref, vch_ref)):
        kvbuf[j] = kk[:, j * KV_DIM:(j + 1) * KV_DIM]
        for r in range(CMP_STRIDE):
            rows = kvbuf[j, pl.ds(r, tm // CMP_STRIDE, stride=CMP_STRIDE), :]
            out[:, r * KV_DIM:(r + 1) * KV_DIM] = rows.astype(BF16)
    lane = lax.broadcasted_iota(jnp.int32, (tm, KV_DIM), 1)
    first_half = (lane % ROPE_DIM) < ROPE_HALF
    ck = cosk_ref[...]
    sk = sink_ref[...]
    for j, out in ((2, ksel_ref), (3, kwin_ref)):
        k = kk[:, j * KV_DIM:(j + 1) * KV_DIM]
        partner = jnp.where(first_half, pltpu.roll(k, KV_DIM - ROPE_HALF, 1), pltpu.roll(k, ROPE_HALF, 1))
        out[...] = (k * ck + partner * sk).astype(BF16)

    q = yT[0:NSA_DIM, :] * (SCALE * LOG2E)
    qT_ref[...] = q.astype(BF16)
    qrT_ref[...] = q.astype(BF16)
    cq = cosq_ref[...]
    sq = sinq_ref[...]
    for h in range(N_HEADS):
        r0 = h * HEAD_DIM
        x1 = q[r0:r0 + ROPE_HALF, :]
        x2 = q[r0 + ROPE_HALF:r0 + ROPE_DIM, :]
        rot = jnp.concatenate([x1 * cq - x2 * sq, x2 * cq + x1 * sq], axis=0)
        qrT_ref[r0:r0 + ROPE_DIM, :] = rot.astype(BF16)
    tk = vselT_ref.shape[-1]
    ones = jnp.ones((BF16_ROWS, tk), BF16)
    for j, out in enumerate((vselT_ref, vwinT_ref)):
        for i in range(tm // tk):
            for g in range(N_KV_GROUPS):
                r0 = NSA_DIM + j * KV_DIM + g * HEAD_DIM
                out[i, g * V_ROWS:g * V_ROWS + HEAD_DIM, :] = yT[r0:r0 + HEAD_DIM, i * tk:(i + 1) * tk].astype(BF16)
                out[i, g * V_ROWS + HEAD_DIM:(g + 1) * V_ROWS, :] = ones
    gT_ref[...] = jax.nn.sigmoid(yT[NSA_DIM + 2 * KV_DIM:N_TR, :])


def _proj(x2, wstd, wtr, convw, cosk, sink, cosq, sinq, B, S):
    tm, tk = PROJ_TM, ATT_TK
    ns = S // tm
    row = lambda b, s: (b * ns + s, 0)
    const = lambda b, s: (0, 0)
    tok = lambda c, dt: jax.ShapeDtypeStruct((B * S, c), dt)
    cw = CMP_STRIDE * KV_DIM
    chunked = jax.ShapeDtypeStruct((B, S // CMP_STRIDE, cw), BF16)
    chunk_spec = pl.BlockSpec((None, tm // CMP_STRIDE, cw), lambda b, s: (b, s, 0))
    out_shape = (
        tok(CONV_DIM, BF16), chunked, chunked, tok(KV_DIM, BF16), tok(KV_DIM, BF16),
        jax.ShapeDtypeStruct((B, NSA_DIM, S), BF16), jax.ShapeDtypeStruct((B, NSA_DIM, S), BF16),
        jax.ShapeDtypeStruct((B, S // tk, N_KV_GROUPS * V_ROWS, tk), BF16),
        jax.ShapeDtypeStruct((B, S // tk, N_KV_GROUPS * V_ROWS, tk), BF16),
        jax.ShapeDtypeStruct((B, N_KV_GROUPS * GATE_ROWS, S), F32),
    )
    feat = lambda r: pl.BlockSpec((None, r, tm), lambda b, s: (b, 0, s))
    vt = pl.BlockSpec((None, tm // tk, N_KV_GROUPS * V_ROWS, tk), lambda b, s: (b, s, 0, 0))
    return pl.pallas_call(
        _proj_kernel,
        grid=(B, ns),
        in_specs=[
            pl.BlockSpec((tm, D_MODEL), row),
            pl.BlockSpec((N_STD, D_MODEL), const),
            pl.BlockSpec((N_TR, D_MODEL), const),
            pl.BlockSpec((CONV_WIDTH, CONV_DIM), const),
            pl.BlockSpec((tm, KV_DIM), lambda b, s: (s, 0)),
            pl.BlockSpec((tm, KV_DIM), lambda b, s: (s, 0)),
            pl.BlockSpec((ROPE_HALF, tm), lambda b, s: (0, s)),
            pl.BlockSpec((ROPE_HALF, tm), lambda b, s: (0, s)),
        ],
        out_specs=(
            pl.BlockSpec((tm, CONV_DIM), row), chunk_spec, chunk_spec,
            pl.BlockSpec((tm, KV_DIM), row), pl.BlockSpec((tm, KV_DIM), row),
            feat(NSA_DIM), feat(NSA_DIM), vt, vt, feat(N_KV_GROUPS * GATE_ROWS),
        ),
        out_shape=out_shape,
        scratch_shapes=[pltpu.VMEM((SUBLANES, CONV_DIM), F32), pltpu.VMEM((2, tm, KV_DIM), F32)],
        compiler_params=pltpu.CompilerParams(
            dimension_semantics=("arbitrary", "arbitrary"), vmem_limit_bytes=VMEM_LIMIT),
        name="proj",
    )(x2, wstd, wtr, convw, cosk, sink, cosq, sinq)


def _gelu_tanh(x):
    return x * (0.5 * (1.0 + jnp.tanh(np.sqrt(2.0 / np.pi).astype(np.float32) * (x + 0.044715 * (x * x * x)))))


def _compress_kernel(kch_ref, vch_ref, pek_ref, pev_ref, wk1_ref, wv1_ref, wk2_ref, wv2t_ref, kc_ref, vcT_ref,
                     wk1_big, wv1_big):
    nb, ncp = kch_ref.shape[0], kch_ref.shape[1]
    for w_ref, big in ((wk1_ref, wk1_big), (wv1_ref, wv1_big)):
        big[...] = jnp.zeros(big.shape, BF16)
        for half in range(2):
            for r in range(CMP_STRIDE):
                for g in range(N_KV_GROUPS):
                    r0 = r * KV_DIM + g * HEAD_DIM
                    big[half, r0:r0 + HEAD_DIM, g * CMP_HIDDEN:(g + 1) * CMP_HIDDEN] = (
                        w_ref[half * CMP_STRIDE + r])

    def hidden(ch_ref, pe_ref, big):
        ch = ch_ref[...].reshape(nb * ncp, ch_ref.shape[2])
        a = _dot(ch, big[0])
        b = _dot(ch, big[1])
        pe = _dot(pe_ref[0], big[0]) + _dot(pe_ref[1], big[1])
        b_next = pltpu.roll(b, b.shape[0] - 1, 0)
        return _gelu_tanh(a + b_next + pe[0:1, :]).astype(BF16)

    kc = _dot(hidden(kch_ref, pek_ref, wk1_big), wk2_ref[...]).astype(BF16)
    kc_ref[...] = kc.reshape(nb, ncp, kc.shape[1])
    vT = _dot_nt(wv2t_ref[...], hidden(vch_ref, pev_ref, wv1_big)).astype(BF16)
    for b in range(nb):
        vcT_ref[b] = vT[:, b * ncp:(b + 1) * ncp]


def _compress(kch, vch, pek, pev, wk1, wv1, wk2, wv2t, B, ncp):
    cw = CMP_STRIDE * KV_DIM
    gh = N_KV_GROUPS * CMP_HIDDEN
    full = lambda shape: pl.BlockSpec(shape, lambda i: (0,) * len(shape))
    return pl.pallas_call(
        _compress_kernel,
        grid=(1,),
        in_specs=[
            full((B, ncp, cw)), full((B, ncp, cw)),
            full((2, SUBLANES, cw)), full((2, SUBLANES, cw)),
            full((CMP_BLOCK, HEAD_DIM, CMP_HIDDEN)), full((CMP_BLOCK, HEAD_DIM, CMP_HIDDEN)),
            full((gh, KV_DIM)), full((KV_DIM, gh)),
        ],
        out_specs=(full((B, ncp, KV_DIM)), full((B, KV_DIM, ncp))),
        out_shape=(jax.ShapeDtypeStruct((B, ncp, KV_DIM), BF16), jax.ShapeDtypeStruct((B, KV_DIM, ncp), BF16)),
        scratch_shapes=[pltpu.VMEM((2, cw, gh), BF16), pltpu.VMEM((2, cw, gh), BF16)],
        compiler_params=pltpu.CompilerParams(dimension_semantics=("arbitrary",), vmem_limit_bytes=VMEM_LIMIT),
        name="compress",
    )(kch, vch, pek, pev, wk1, wv1, wk2, wv2t)


def _attn_kernel(n_cast, qT_ref, qrT_ref, kc_ref, vcT_ref, ksel_ref, vselT_ref, kwin_ref, vwinT_ref, gT_ref,
                 mT_ref, wbias_ref, *rest):
    cast_in, o_ref, cast_out = rest[:n_cast], rest[n_cast], rest[n_cast + 1:2 * n_cast + 1]
    selb_scr, q_scr, qr_scr, s_scr, swin_scr, m_scr, acc_scr, oT_scr = rest[2 * n_cast + 1:]

    tq = qT_ref.shape[1]
    tk = ksel_ref.shape[1]
    nsub = tq // LANES
    nchunk = N_HEADS * nsub
    gchunk = HEADS_PER_GROUP * nsub
    gl = HEADS_PER_GROUP * tq
    ns = mT_ref.shape[0]
    bpt = tk // SEL_BLOCK
    p = pl.program_id(1)
    qbase = p * tq

    def chunk(c):
        return slice(c * LANES, (c + 1) * LANES)

    def sub(tsub):
        return slice(tsub * LANES, (tsub + 1) * LANES)

    def stage_q(ref, scr):
        zeros = jnp.zeros((HEAD_DIM, tq), BF16)
        for h in range(N_HEADS):
            qh = ref[h * HEAD_DIM:(h + 1) * HEAD_DIM, :]
            halves = [qh, zeros] if h // HEADS_PER_GROUP == 0 else [zeros, qh]
            scr[:, h * tq:(h + 1) * tq] = jnp.concatenate(halves, axis=0)

    gt = gT_ref[...]

    def gate(br, c):
        h, tsub = divmod(c, nsub)
        g, z = divmod(h, HEADS_PER_GROUP)
        r = g * GATE_ROWS + br * HEADS_PER_GROUP + z
        return gt[r:r + 1, sub(tsub)]

    stage_q(qT_ref, q_scr)
    stage_q(qrT_ref, qr_scr)
    start = p % 2
    n_back = WINDOW // tk
    case = jnp.minimum(p, n_back)
    lo = p - case

    def qk_sel(kt, slot):
        s_all = _dot(ksel_ref[kt], qr_scr[...])
        for c in range(nchunk):
            rb = selb_scr[c // gchunk, kt, 0:bpt, sub(c % nsub)]
            s_scr[slot, :, chunk(c)] = s_all[:, chunk(c)] + jnp.concatenate(
                [jnp.broadcast_to(rb[i:i + 1, :], (SEL_BLOCK, LANES)) for i in range(bpt)], axis=0)

    kc = kc_ref[...]
    mT = mT_ref[...]
    ncp = kc.shape[0]
    s_cmp = _dot(kc, q_scr[...])
    k_win = kwin_ref[pl.ds(lo, n_back + 1)].reshape((n_back + 1) * tk, KV_DIM)
    s_win = _dot(k_win, qr_scr[...])
    for c in range(nchunk):
        swin_scr[:, chunk(c)] = s_win[:, chunk(c)] + wbias_ref[case, :, sub(c % nsub)]

    imp = []
    for g in range(N_KV_GROUPS):
        vcT = vcT_ref[g * HEAD_DIM:(g + 1) * HEAD_DIM, :]
        imp_parts = []
        for tsub in range(nsub):
            n_idx = lax.broadcasted_iota(jnp.int32, (ncp, LANES), 0)
            t_idx = qbase + tsub * LANES + lax.broadcasted_iota(jnp.int32, (ncp, LANES), 1)
            ok = (n_idx * CMP_STRIDE + (CMP_BLOCK - 1)) <= t_idx
            bias = jnp.where(ok, 0.0, NEG).astype(F32)
            keep = ok.astype(F32)
            p_sum = None
            for z in range(HEADS_PER_GROUP):
                c = (g * HEADS_PER_GROUP + z) * nsub + tsub
                s = s_cmp[:, chunk(c)] + bias
                e = jnp.exp2(s - jnp.max(s, axis=0, keepdims=True)) * keep
                inv = 1.0 / jnp.maximum(jnp.sum(e, axis=0, keepdims=True), 1e-30)
                oT_scr[:, chunk(c)] = _dot(vcT, e.astype(BF16)) * (inv * gate(0, c))
                p_sum = e * inv if p_sum is None else p_sum + e * inv
            imp_parts.append(jnp.dot(mT, p_sum, preferred_element_type=F32, precision=lax.Precision.HIGHEST))
        imp.append(jnp.concatenate(imp_parts, axis=1))

    j_idx = lax.broadcasted_iota(jnp.int32, (ns, tq), 0)
    cur = (qbase + lax.broadcasted_iota(jnp.int32, (ns, tq), 1)) // SEL_BLOCK
    forced = (j_idx == 0) | (j_idx == cur) | (j_idx == cur - 1)
    for g in range(N_KV_GROUPS):
        impv = jnp.where(j_idx <= cur, jnp.where(forced, FORCE, imp[g]), NEG)
        groups = [impv[a:a + SUBLANES, :] for a in range(0, ns, SUBLANES)]
        sub_idx = lax.broadcasted_iota(jnp.int32, (SUBLANES, tq), 0)
        cnts = [jnp.zeros((SUBLANES, tq), F32) for _ in groups]
        for jp in range(ns):
            row = impv[jp:jp + 1, :]
            for gi, grp in enumerate(groups):
                if gi > jp // SUBLANES:
                    beats = row >= grp
                elif gi < jp // SUBLANES:
                    beats = row > grp
                else:
                    beats = (row > grp) | ((row == grp) & (sub_idx > jp % SUBLANES))
                cnts[gi] = cnts[gi] + beats.astype(F32)
        cnt = jnp.concatenate(cnts, axis=0)
        sel = (cnt < float(min(N_SEL, ns))) & (impv > 0.5 * NEG)
        selb = jnp.where(sel, 0.0, NEG).astype(F32)
        for kt in range(ns // bpt):
            selb_scr[g, kt, 0:bpt, :] = selb[kt * bpt:(kt + 1) * bpt, :]
    qk_sel(start, 0)

    SEL, WIN = 0, 1
    m_scr[...] = jnp.full(m_scr.shape, NEG, F32)
    acc_scr[...] = jnp.zeros(acc_scr.shape, F32)

    def softmax_pv(st, load_s, vT_t, tile_bias=None):
        for g in range(N_KV_GROUPS):
            es, alphas = [], []
            for cg in range(gchunk):
                c = g * gchunk + cg

                def scores():
                    s = load_s(c)
                    return s if tile_bias is None else s + tile_bias(c % nsub)

                m_old = m_scr[st, :, chunk(c)]
                m_new = jnp.maximum(m_old, jnp.max(scores(), axis=0, keepdims=True))
                m_scr[st, :, chunk(c)] = m_new
                es.append(jnp.exp2(scores() - m_new).astype(BF16))
                alphas.append(jnp.exp2(m_old - m_new))
            lanes = slice(g * gl, (g + 1) * gl)
            pv = _dot(vT_t[g * V_ROWS:(g + 1) * V_ROWS, :], jnp.concatenate(es, axis=1))
            acc_scr[st, :, lanes] = jnp.concatenate(alphas, axis=1) * acc_scr[st, :, lanes] + pv

    def finish(st, br):
        for c in range(nchunk):
            acc = acc_scr[st, :, chunk(c)]
            inv = 1.0 / jnp.maximum(acc[HEAD_DIM:HEAD_DIM + 1, :], 1e-30)
            oT_scr[:, chunk(c)] += acc[0:HEAD_DIM, :] * (inv * gate(br, c))

    for i in range(n_back, -1, -1):
        rows = slice(i * tk, (i + 1) * tk)
        softmax_pv(WIN, lambda c: swin_scr[rows, chunk(c)], vwinT_ref[lo + i])
    finish(WIN, 2)

    def sp_sel(kt, slot, tile_bias=None):
        softmax_pv(SEL, lambda c: s_scr[slot, :, chunk(c)], vselT_ref[kt], tile_bias)

    @pl.when(start == 1)
    def _():
        qk_sel(0, 1)
        sp_sel(0, 1)

    def pair_body(j, carry):
        a = start + 2 * j
        qk_sel(a + 1, 1)
        sp_sel(a, 0)
        qk_sel(a + 2, 0)
        sp_sel(a + 1, 1)
        return carry

    lax.fori_loop(0, p // 2, pair_body, 0)
    sp_sel(p, 0, lambda tsub: wbias_ref[n_back, n_back * tk:(n_back + 1) * tk, sub(tsub)])
    finish(SEL, 1)

    for tsub in range(nsub):
        for i in range(N_HEADS // 2):
            pair = jnp.concatenate([oT_scr[:, chunk((2 * i) * nsub + tsub)],
                                    oT_scr[:, chunk((2 * i + 1) * nsub + tsub)]], axis=0)
            o_ref[sub(tsub), i * 2 * HEAD_DIM:(i + 1) * 2 * HEAD_DIM] = pair.T.astype(BF16)

    for src, dst in zip(cast_in, cast_out):
        dst[...] = src[...].astype(BF16)


def _window_bias(t, n_back):
    r = np.arange(t)[:, None]
    c = np.arange(t)[None, :]
    causal = np.where(r <= c, 0.0, NEG)
    edge = np.where(r > c, 0.0, NEG)
    out = np.zeros((n_back + 1, (n_back + 1) * t, t), np.float32)
    for case in range(n_back + 1):
        for i in range(n_back + 1):
            back = case - i
            tile = NEG if back < 0 else causal if back == 0 else edge if back == n_back else 0.0
            out[case, i * t:(i + 1) * t, :] = tile
    return jnp.asarray(out)


def _attn(qT, qrT, kc, vcT, ksel, vselT, kwin, vwinT, gT, mT, B, S, cast=()):
    tq, tk = ATT_TQ, ATT_TK
    assert tq == tk and WINDOW % tk == 0 and tk % SEL_BLOCK == 0 and tq % LANES == 0
    assert tk // SEL_BLOCK <= SUBLANES
    n_back = WINDOW // tk
    assert S // tk > n_back
    P = S // tq
    ncp = kc.shape[1]
    ns = S // SEL_BLOCK
    nq = N_HEADS * tq
    qspec = pl.BlockSpec((None, NSA_DIM, tq), lambda b, p: (b, 0, p))
    kspec = pl.BlockSpec((None, S // tk, tk, KV_DIM), lambda b, p: (b, 0, 0, 0))
    vspec = pl.BlockSpec((None, S // tk, N_KV_GROUPS * V_ROWS, tk), lambda b, p: (b, 0, 0, 0))
    steps = B * P
    assert all(w.shape[0] % (steps * BF16_ROWS) == 0 for w in cast)
    slab = lambda w: pl.BlockSpec((w.shape[0] // steps, w.shape[1]), lambda b, p: (b * P + p, 0))
    out = pl.pallas_call(
        functools.partial(_attn_kernel, len(cast)),
        grid=(B, P),
        in_specs=[
            qspec, qspec,
            pl.BlockSpec((None, ncp, KV_DIM), lambda b, p: (b, 0, 0)),
            pl.BlockSpec((None, KV_DIM, ncp), lambda b, p: (b, 0, 0)),
            kspec, vspec, kspec, vspec,
            pl.BlockSpec((None, N_KV_GROUPS * GATE_ROWS, tq), lambda b, p: (b, 0, p)),
            pl.BlockSpec((ns, ncp), lambda b, p: (0, 0)),
            pl.BlockSpec((n_back + 1, (n_back + 1) * tk, tq), lambda b, p: (0, 0, 0)),
        ] + [slab(w) for w in cast],
        out_specs=[pl.BlockSpec((tq, NSA_DIM), lambda b, p: (b * P + p, 0))] + [slab(w) for w in cast],
        out_shape=[jax.ShapeDtypeStruct((B * S, NSA_DIM), BF16)]
        + [jax.ShapeDtypeStruct(w.shape, BF16) for w in cast],
        scratch_shapes=[
            pltpu.VMEM((N_KV_GROUPS, S // tk, SUBLANES, tq), F32),
            pltpu.VMEM((KV_DIM, nq), BF16),
            pltpu.VMEM((KV_DIM, nq), BF16),
            pltpu.VMEM((2, tk, nq), F32),
            pltpu.VMEM(((n_back + 1) * tk, nq), F32),
            pltpu.VMEM((2, 1, nq), F32),
            pltpu.VMEM((2, V_ROWS, nq), F32),
            pltpu.VMEM((HEAD_DIM, nq), F32),
        ],
        compiler_params=pltpu.CompilerParams(
            dimension_semantics=("arbitrary", "arbitrary"), vmem_limit_bytes=VMEM_LIMIT),
        name="attn",
    )(qT, qrT, kc, vcT, ksel, vselT, kwin, vwinT, gT, mT, _window_bias(tk, n_back), *cast)
    return out[0], tuple(out[1:])


def _layer_norm(z, g, b):
    mu = jnp.mean(z, axis=-1, keepdims=True)
    zc = z - mu
    var = jnp.mean(zc * zc, axis=-1, keepdims=True)
    return zc * lax.rsqrt(var + LN_EPS) * g + b


def _tail_kernel(alpha, x_ref, mix_ref, o_ref, wg_ref, wco_ref, wno_ref, wo_ref, g1_ref, b1_ref,
                 wup_ref, wdn_ref, g2_ref, b2_ref, out_ref):
    ts = TAIL_SUB
    subs = [slice(i * ts, (i + 1) * ts) for i in range(x_ref.shape[0] // ts)]
    xs = [x_ref[rs, :] for rs in subs]
    xbs = [x.astype(BF16) for x in xs]

    def merge(i):
        rs, x, xb = subs[i], xs[i], xbs[i]
        y_conv = _dot(mix_ref[rs, :], wco_ref[...])
        y_nsa = _dot(o_ref[rs, :], wno_ref[...])
        mixed = (jax.nn.sigmoid(_dot_nt(xb, wg_ref[0:D_MODEL, :])) * y_conv
                 + jax.nn.sigmoid(_dot_nt(xb, wg_ref[D_MODEL:2 * D_MODEL, :])) * y_nsa)
        return alpha * x + _dot(mixed.astype(BF16), wo_ref[...])

    def mlp(x1):
        x1b = x1.astype(BF16)
        ff = jnp.zeros_like(x1)
        for c in range(D_FF // FF_CHUNK):
            h = jnp.maximum(_dot(x1b, wup_ref[:, c * FF_CHUNK:(c + 1) * FF_CHUNK]), 0.0)
            ff = ff + _dot((h * h).astype(BF16), wdn_ref[c * FF_CHUNK:(c + 1) * FF_CHUNK, :])
        return alpha * x1 + ff

    zs = [merge(i) for i in range(len(subs))]
    ys = [mlp(_layer_norm(z, g1_ref[...], b1_ref[...])) for z in zs]
    for rs, y in zip(subs, ys):
        out_ref[rs, :] = _layer_norm(y, g2_ref[...], b2_ref[...])


def _tail(x2, mix, o, wg, wco, wno, wo, g1, b1, wup, wdn, g2, b2, alpha):
    n = x2.shape[0]
    tm = TAIL_TM
    row = lambda i: (i, 0)
    const = lambda i: (0, 0)
    resident = lambda shape: pl.BlockSpec(shape, const, pipeline_mode=pl.Buffered(1))
    return pl.pallas_call(
        functools.partial(_tail_kernel, alpha),
        grid=(n // tm,),
        in_specs=[
            pl.BlockSpec((tm, D_MODEL), row), pl.BlockSpec((tm, CONV_DIM), row), pl.BlockSpec((tm, NSA_DIM), row),
            resident((2 * D_MODEL, D_MODEL)), resident((CONV_DIM, D_MODEL)), resident((NSA_DIM, D_MODEL)),
            resident((D_MODEL, D_MODEL)), resident((1, D_MODEL)), resident((1, D_MODEL)),
            resident((D_MODEL, D_FF)), resident((D_FF, D_MODEL)), resident((1, D_MODEL)), resident((1, D_MODEL)),
        ],
        out_specs=pl.BlockSpec((tm, D_MODEL), row),
        out_shape=jax.ShapeDtypeStruct((n, D_MODEL), F32),
        compiler_params=pltpu.CompilerParams(dimension_semantics=("arbitrary",), vmem_limit_bytes=VMEM_LIMIT),
        name="tail",
    )(x2, mix, o, wg, wco, wno, wo, g1, b1, wup, wdn, g2, b2)


def _rope_tables(S):
    inv = ROPE_THETA ** (-jnp.arange(0, ROPE_DIM, 2, dtype=F32) / ROPE_DIM)
    ang = jnp.arange(S, dtype=F32)[:, None] * inv[None, :]
    cos, sin = jnp.cos(ang), jnp.sin(ang)
    pad_c = jnp.ones((S, HEAD_DIM - ROPE_DIM), F32)
    pad_s = jnp.zeros((S, HEAD_DIM - ROPE_DIM), F32)
    ck = jnp.tile(jnp.concatenate([cos, cos, pad_c], axis=1), (1, N_KV_GROUPS))
    sk = jnp.tile(jnp.concatenate([-sin, sin, pad_s], axis=1), (1, N_KV_GROUPS))
    return ck, sk, cos.T, sin.T


def _overlap_matrix_t(ncp, ns):
    nc = ncp - 1
    i = np.arange(ncp)[None, :]
    j = np.arange(ns)[:, None]
    m = (i * CMP_STRIDE < (j + 1) * SEL_BLOCK) & (i * CMP_STRIDE + CMP_BLOCK > j * SEL_BLOCK) & (i < nc)
    return jnp.asarray(m.astype(np.float32))


def _compress_weights(pe, w1, w2):
    eye = jnp.eye(N_KV_GROUPS, dtype=F32)
    peh = pe.reshape(2, CMP_STRIDE, 1, HEAD_DIM)
    pebig = jnp.broadcast_to(peh, (2, CMP_STRIDE, N_KV_GROUPS, HEAD_DIM)).reshape(2, 1, CMP_STRIDE * KV_DIM)
    pebig = jnp.broadcast_to(pebig, (2, SUBLANES, CMP_STRIDE * KV_DIM))
    w2big = jnp.einsum('ab,hd->ahbd', eye, w2).reshape(N_KV_GROUPS * CMP_HIDDEN, KV_DIM)
    return pebig.astype(BF16), w1.astype(BF16), w2big.astype(BF16)


def _layer(x2, B, S, w_in, conv_w, w_conv_out, pe_k, wk1, wk2, pe_v, wv1, wv2, w_nsa_out, w_o,
           ln1_g, ln1_b, w_up, w_down, ln2_g, ln2_b, alpha):
    ncp = S // CMP_STRIDE
    ns = S // SEL_BLOCK
    wstd, wg, wtr = _wprep(jnp.swapaxes(w_in, 0, 1))
    ck, sk, cq, sq = _rope_tables(S)

    mix, kch, vch, ksel, kwin, qT, qrT, vselT, vwinT, gT = _proj(
        x2, wstd, wtr, conv_w.reshape(CONV_WIDTH, CONV_DIM), ck, sk, cq, sq, B, S)

    pek, wk1b, wk2b = _compress_weights(pe_k, wk1, wk2)
    pev, wv1b, wv2b = _compress_weights(pe_v, wv1, wv2)
    kc, vcT = _compress(kch, vch, pek, pev, wk1b, wv1b, wk2b, wv2b.T, B, ncp)

    tk = ATT_TK
    steps = B * (S // ATT_TQ)
    big = (w_up, w_down, w_o)
    in_attn = all(w.shape[0] % (steps * BF16_ROWS) == 0 for w in big)
    o, cast = _attn(qT, qrT, kc, vcT, ksel.reshape(B, S // tk, tk, KV_DIM), vselT,
                    kwin.reshape(B, S // tk, tk, KV_DIM), vwinT, gT, _overlap_matrix_t(ncp, ns), B, S,
                    cast=big if in_attn else ())
    w_up_b, w_down_b, w_o_b = cast if in_attn else tuple(w.astype(BF16) for w in big)

    row = lambda v: v.reshape(1, D_MODEL).astype(F32)
    return _tail(x2, mix, o, wg, w_conv_out.astype(BF16), w_nsa_out.astype(BF16), w_o_b,
                 row(ln1_g), row(ln1_b), w_up_b, w_down_b, row(ln2_g), row(ln2_b), alpha)


def kernel(x, w_in, conv_w, w_conv_out, pe_k_cmp, w_k_cmp1, w_k_cmp2, pe_v_cmp, w_v_cmp1, w_v_cmp2,
           w_nsa_out, w_o, ln1_g, ln1_b, w_up, w_down, ln2_g, ln2_b):
    B, S, D = x.shape
    depth = w_in.shape[0]
    assert D == D_MODEL and w_in.shape[2] == O_END
    assert S % PROJ_TM == 0 and S % ATT_TQ == 0 and (B * S) % TAIL_TM == 0 and S >= WINDOW
    alpha = float((2 * depth) ** 0.25)
    x2 = x.reshape(B * S, D)
    for l in range(depth):
        x2 = _layer(x2, B, S, w_in[l], conv_w[l], w_conv_out[l], pe_k_cmp[l], w_k_cmp1[l], w_k_cmp2[l],
                    pe_v_cmp[l], w_v_cmp1[l], w_v_cmp2[l], w_nsa_out[l], w_o[l], ln1_g[l], ln1_b[l],
                    w_up[l], w_down[l], ln2_g[l], ln2_b[l], alpha)
    return x2.reshape(B, S, D)
```

```python
import functools

import numpy as np
import jax
import jax.numpy as jnp
from jax import lax
from jax.experimental import pallas as pl
from jax.experimental.pallas import tpu as pltpu

F32 = jnp.float32
BF16 = jnp.bfloat16

D_MODEL = 1024
CONV_DIM = D_MODEL // 2
CONV_WIDTH = 3
N_HEADS = 8
HEAD_DIM = 64
N_KV_GROUPS = 2
HEADS_PER_GROUP = N_HEADS // N_KV_GROUPS
NSA_DIM = N_HEADS * HEAD_DIM
KV_DIM = N_KV_GROUPS * HEAD_DIM
ROPE_DIM = HEAD_DIM // 4
ROPE_HALF = ROPE_DIM // 2
ROPE_THETA = 500000.0
CMP_BLOCK = 32
CMP_STRIDE = 16
CMP_HIDDEN = 2 * HEAD_DIM
SEL_BLOCK = 64
N_SEL = 16
WINDOW = 512
N_NSA_BRANCHES = 3
D_FF = 4 * D_MODEL
LN_EPS = 1e-5
NEG = -1e30
FORCE = 1e9
SCALE = HEAD_DIM ** -0.5
LOG2E = 1.4426950408889634

SUBLANES = 8
LANES = 128
BF16_ROWS = 16
V_ROWS = HEAD_DIM + BF16_ROWS

_SPLITS = (CONV_DIM, CONV_DIM, CONV_DIM, NSA_DIM, KV_DIM, KV_DIM, KV_DIM, KV_DIM, KV_DIM, KV_DIM,
           N_HEADS * N_NSA_BRANCHES, D_MODEL, D_MODEL)
_OFFS = np.concatenate([[0], np.cumsum(_SPLITS)]).tolist()
(O_H, O_B, O_C, O_Q, O_KCMP, O_VCMP, O_KSEL, O_VSEL, O_KWIN, O_VWIN, O_GATE, O_GCONV, O_GNSA, O_END) = _OFFS

GATE_ROWS = 16
N_STD = 3 * CONV_DIM + 4 * KV_DIM
N_TR = NSA_DIM + 2 * KV_DIM + N_KV_GROUPS * GATE_ROWS

PROJ_TM = 1024
PROJ_PIECE = 256
ATT_TQ = 256
ATT_TK = 256
ATT_MEMBERS = 2
TAIL_TM = 512
TAIL_SUB = 256
FF_CHUNK = 1024
VMEM_LIMIT = 56 * 1024 * 1024


def _dot(a, b):
    return jnp.dot(a, b, preferred_element_type=F32)


def _dot_nt(a, b):
    return lax.dot_general(a, b, (((1,), (1,)), ((), ())), preferred_element_type=F32)


WPREP_COLS = 256


def _wprep_kernel(wT_ref, wstdT_ref, wgT_ref, wtr_ref):
    r = 0
    for a, b in ((O_H, O_Q), (O_KCMP, O_KSEL), (O_KSEL, O_VSEL), (O_KWIN, O_VWIN)):
        wstdT_ref[r:r + b - a, :] = wT_ref[a:b, :].astype(BF16)
        r += b - a
    wgT_ref[...] = wT_ref[O_GCONV:O_END, :].astype(BF16)
    r = 0
    for a, b in ((O_Q, O_KCMP), (O_VSEL, O_KWIN), (O_VWIN, O_GATE)):
        wtr_ref[r:r + b - a, :] = wT_ref[a:b, :].astype(BF16)
        r += b - a
    gates = wT_ref[O_GATE:O_GCONV, :]
    rows = []
    for g in range(N_KV_GROUPS):
        for br in range(N_NSA_BRANCHES):
            for z in range(HEADS_PER_GROUP):
                src = (g * HEADS_PER_GROUP + z) * N_NSA_BRANCHES + br
                rows.append(gates[src:src + 1, :])
        rows.append(jnp.zeros((GATE_ROWS - N_NSA_BRANCHES * HEADS_PER_GROUP, gates.shape[1]), F32))
    wtr_ref[r:r + N_KV_GROUPS * GATE_ROWS, :] = jnp.concatenate(rows, axis=0).astype(BF16)


def _wprep(wT):
    d = wT.shape[1]
    cb = WPREP_COLS
    assert d % cb == 0 and O_GCONV % SUBLANES == 0
    strip = lambda rows: pl.BlockSpec((rows, cb), lambda i: (0, i))
    return pl.pallas_call(
        _wprep_kernel,
        grid=(d // cb,),
        in_specs=[strip(O_END)],
        out_specs=(strip(N_STD), strip(2 * D_MODEL), strip(N_TR)),
        out_shape=(jax.ShapeDtypeStruct((N_STD, d), BF16), jax.ShapeDtypeStruct((2 * D_MODEL, d), BF16),
                   jax.ShapeDtypeStruct((N_TR, d), BF16)),
        compiler_params=pltpu.CompilerParams(dimension_semantics=("arbitrary",), vmem_limit_bytes=VMEM_LIMIT),
        name="wprep",
    )(wT)


def _proj_kernel(x_ref, wstd_ref, wtr_ref, convw_ref, cosk_ref, sink_ref, cosq_ref, sinq_ref,
                 mix_ref, kch_ref, vch_ref, ksel_ref, kwin_ref,
                 qT_ref, qrT_ref, vselT_ref, vwinT_ref, gT_ref, ubuf, kvbuf):
    tm = x_ref.shape[0]

    @pl.when(pl.program_id(1) == 0)
    def _():
        ubuf[...] = jnp.zeros(ubuf.shape, F32)

    xb = x_ref[...].astype(BF16)
    yT = _dot_nt(wtr_ref[...], xb)
    kk = _dot_nt(xb, wstd_ref[3 * CONV_DIM:N_STD, :])
    halves = [slice(i * PROJ_PIECE, (i + 1) * PROJ_PIECE) for i in range(tm // PROJ_PIECE)]
    hbcs = [_dot_nt(xb[rs, :], wstd_ref[0:3 * CONV_DIM, :]) for rs in halves]

    cw = convw_ref[...]
    head_rows = lax.broadcasted_iota(jnp.int32, (SUBLANES, CONV_DIM), 0)
    prev = ubuf[...]
    for rs, hbc in zip(halves, hbcs):
        u = hbc[:, 2 * CONV_DIM:3 * CONV_DIM] * hbc[:, 0:CONV_DIM]

        def shifted(k):
            r = pltpu.roll(u, k, 0)
            first = jnp.where(head_rows < k, pltpu.roll(prev, k, 0), r[0:SUBLANES, :])
            return jnp.concatenate([first, r[SUBLANES:, :]], axis=0)

        y = cw[0:1, :] * shifted(2) + cw[1:2, :] * shifted(1) + cw[2:3, :] * u
        mix_ref[rs, :] = (hbc[:, CONV_DIM:2 * CONV_DIM] * y).astype(BF16)
        prev = u[u.shape[0] - SUBLANES:, :]
    ubuf[...] = prev

    for j, out in enumerate((kch_ref, vch_ref)):
        kvbuf[j] = kk[:, j * KV_DIM:(j + 1) * KV_DIM]
        for r in range(CMP_STRIDE):
            rows = kvbuf[j, pl.ds(r, tm // CMP_STRIDE, stride=CMP_STRIDE), :]
            out[:, r * KV_DIM:(r + 1) * KV_DIM] = rows.astype(BF16)
    lane = lax.broadcasted_iota(jnp.int32, (tm, KV_DIM), 1)
    first_half = (lane % ROPE_DIM) < ROPE_HALF
    ck = cosk_ref[...]
    sk = sink_ref[...]
    for j, out in ((2, ksel_ref), (3, kwin_ref)):
        k = kk[:, j * KV_DIM:(j + 1) * KV_DIM]
        partner = jnp.where(first_half, pltpu.roll(k, KV_DIM - ROPE_HALF, 1), pltpu.roll(k, ROPE_HALF, 1))
        out[...] = (k * ck + partner * sk).astype(BF16)

    q = yT[0:NSA_DIM, :] * (SCALE * LOG2E)
    qT_ref[...] = q.astype(BF16)
    qrT_ref[...] = q.astype(BF16)
    cq = cosq_ref[...]
    sq = sinq_ref[...]
    for h in range(N_HEADS):
        r0 = h * HEAD_DIM
        x1 = q[r0:r0 + ROPE_HALF, :]
        x2 = q[r0 + ROPE_HALF:r0 + ROPE_DIM, :]
        rot = jnp.concatenate([x1 * cq - x2 * sq, x2 * cq + x1 * sq], axis=0)
        qrT_ref[r0:r0 + ROPE_DIM, :] = rot.astype(BF16)
    tk = vselT_ref.shape[-1]
    ones = jnp.ones((BF16_ROWS, tk), BF16)
    for j, out in enumerate((vselT_ref, vwinT_ref)):
        for i in range(tm // tk):
            for g in range(N_KV_GROUPS):
                r0 = NSA_DIM + j * KV_DIM + g * HEAD_DIM
                out[i, g * V_ROWS:g * V_ROWS + HEAD_DIM, :] = yT[r0:r0 + HEAD_DIM, i * tk:(i + 1) * tk].astype(BF16)
                out[i, g * V_ROWS + HEAD_DIM:(g + 1) * V_ROWS, :] = ones
    gT_ref[...] = jax.nn.sigmoid(yT[NSA_DIM + 2 * KV_DIM:N_TR, :])


def _proj(x2, wstd, wtr, convw, cosk, sink, cosq, sinq, B, S):
    tm, tk = PROJ_TM, ATT_TK
    ns = S // tm
    row = lambda b, s: (b * ns + s, 0)
    const = lambda b, s: (0, 0)
    tok = lambda c, dt: jax.ShapeDtypeStruct((B * S, c), dt)
    cw = CMP_STRIDE * KV_DIM
    chunked = jax.ShapeDtypeStruct((B, S // CMP_STRIDE, cw), BF16)
    chunk_spec = pl.BlockSpec((None, tm // CMP_STRIDE, cw), lambda b, s: (b, s, 0))
    out_shape = (
        tok(CONV_DIM, BF16), chunked, chunked, tok(KV_DIM, BF16), tok(KV_DIM, BF16),
        jax.ShapeDtypeStruct((B, NSA_DIM, S), BF16), jax.ShapeDtypeStruct((B, NSA_DIM, S), BF16),
        jax.ShapeDtypeStruct((B, S // tk, N_KV_GROUPS * V_ROWS, tk), BF16),
        jax.ShapeDtypeStruct((B, S // tk, N_KV_GROUPS * V_ROWS, tk), BF16),
        jax.ShapeDtypeStruct((B, N_KV_GROUPS * GATE_ROWS, S), F32),
    )
    feat = lambda r: pl.BlockSpec((None, r, tm), lambda b, s: (b, 0, s))
    vt = pl.BlockSpec((None, tm // tk, N_KV_GROUPS * V_ROWS, tk), lambda b, s: (b, s, 0, 0))
    return pl.pallas_call(
        _proj_kernel,
        grid=(B, ns),
        in_specs=[
            pl.BlockSpec((tm, D_MODEL), row),
            pl.BlockSpec((N_STD, D_MODEL), const),
            pl.BlockSpec((N_TR, D_MODEL), const),
            pl.BlockSpec((CONV_WIDTH, CONV_DIM), const),
            pl.BlockSpec((tm, KV_DIM), lambda b, s: (s, 0)),
            pl.BlockSpec((tm, KV_DIM), lambda b, s: (s, 0)),
            pl.BlockSpec((ROPE_HALF, tm), lambda b, s: (0, s)),
            pl.BlockSpec((ROPE_HALF, tm), lambda b, s: (0, s)),
        ],
        out_specs=(
            pl.BlockSpec((tm, CONV_DIM), row), chunk_spec, chunk_spec,
            pl.BlockSpec((tm, KV_DIM), row), pl.BlockSpec((tm, KV_DIM), row),
            feat(NSA_DIM), feat(NSA_DIM), vt, vt, feat(N_KV_GROUPS * GATE_ROWS),
        ),
        out_shape=out_shape,
        scratch_shapes=[pltpu.VMEM((SUBLANES, CONV_DIM), F32), pltpu.VMEM((2, tm, KV_DIM), F32)],
        compiler_params=pltpu.CompilerParams(
            dimension_semantics=("arbitrary", "arbitrary"), vmem_limit_bytes=VMEM_LIMIT),
        name="proj",
    )(x2, wstd, wtr, convw, cosk, sink, cosq, sinq)


def _gelu_tanh(x):
    return x * (0.5 * (1.0 + jnp.tanh(np.sqrt(2.0 / np.pi).astype(np.float32) * (x + 0.044715 * (x * x * x)))))


def _compress_kernel(kch_ref, vch_ref, pek_ref, pev_ref, wk1_ref, wv1_ref, wk2_ref, wv2t_ref, kc_ref, vcT_ref,
                     wk1_big, wv1_big):
    nb, ncp = kch_ref.shape[0], kch_ref.shape[1]
    for w_ref, big in ((wk1_ref, wk1_big), (wv1_ref, wv1_big)):
        big[...] = jnp.zeros(big.shape, BF16)
        for half in range(2):
            for r in range(CMP_STRIDE):
                for g in range(N_KV_GROUPS):
                    r0 = r * KV_DIM + g * HEAD_DIM
                    big[half, r0:r0 + HEAD_DIM, g * CMP_HIDDEN:(g + 1) * CMP_HIDDEN] = (
                        w_ref[half * CMP_STRIDE + r])

    def hidden(ch_ref, pe_ref, big):
        ch = ch_ref[...].reshape(nb * ncp, ch_ref.shape[2])
        a = _dot(ch, big[0])
        b = _dot(ch, big[1])
        pe = _dot(pe_ref[0], big[0]) + _dot(pe_ref[1], big[1])
        b_next = pltpu.roll(b, b.shape[0] - 1, 0)
        return _gelu_tanh(a + b_next + pe[0:1, :]).astype(BF16)

    kc = _dot(hidden(kch_ref, pek_ref, wk1_big), wk2_ref[...]).astype(BF16)
    kc_ref[...] = kc.reshape(nb, ncp, kc.shape[1])
    vT = _dot_nt(wv2t_ref[...], hidden(vch_ref, pev_ref, wv1_big)).astype(BF16)
    for b in range(nb):
        vcT_ref[b] = vT[:, b * ncp:(b + 1) * ncp]


def _compress(kch, vch, pek, pev, wk1, wv1, wk2, wv2t, B, ncp):
    cw = CMP_STRIDE * KV_DIM
    gh = N_KV_GROUPS * CMP_HIDDEN
    full = lambda shape: pl.BlockSpec(shape, lambda i: (0,) * len(shape))
    return pl.pallas_call(
        _compress_kernel,
        grid=(1,),
        in_specs=[
            full((B, ncp, cw)), full((B, ncp, cw)),
            full((2, SUBLANES, cw)), full((2, SUBLANES, cw)),
            full((CMP_BLOCK, HEAD_DIM, CMP_HIDDEN)), full((CMP_BLOCK, HEAD_DIM, CMP_HIDDEN)),
            full((gh, KV_DIM)), full((KV_DIM, gh)),
        ],
        out_specs=(full((B, ncp, KV_DIM)), full((B, KV_DIM, ncp))),
        out_shape=(jax.ShapeDtypeStruct((B, ncp, KV_DIM), BF16), jax.ShapeDtypeStruct((B, KV_DIM, ncp), BF16)),
        scratch_shapes=[pltpu.VMEM((2, cw, gh), BF16), pltpu.VMEM((2, cw, gh), BF16)],
        compiler_params=pltpu.CompilerParams(dimension_semantics=("arbitrary",), vmem_limit_bytes=VMEM_LIMIT),
        name="compress",
    )(kch, vch, pek, pev, wk1, wv1, wk2, wv2t)


def _attn_kernel(n_cast, qT_ref, qrT_ref, kc_ref, vcT_ref, ksel_ref, vselT_ref, kwin_ref, vwinT_ref, gT_ref,
                 mT_ref, wbias_ref, *rest):
    cast_in, o_ref, cast_out = rest[:n_cast], rest[n_cast], rest[n_cast + 1:2 * n_cast + 1]
    scratch = rest[2 * n_cast + 1:]
    nm = qT_ref.shape[0]
    tq = qT_ref.shape[2]
    tk = ksel_ref.shape[2]
    nsub = tq // LANES
    nchunk = N_HEADS * nsub
    gchunk = HEADS_PER_GROUP * nsub
    gl = HEADS_PER_GROUP * tq
    ns = mT_ref.shape[0]
    bpt = tk // SEL_BLOCK
    p = pl.program_id(1)
    qbase = p * tq
    start = p % 2
    n_back = WINDOW // tk
    case = jnp.minimum(p, n_back)
    lo = p - case
    SEL, WIN = 0, 1
    mT = mT_ref[...]

    def chunk(c):
        return slice(c * LANES, (c + 1) * LANES)

    def sub(tsub):
        return slice(tsub * LANES, (tsub + 1) * LANES)

    def member(mb):
        qT, qrT, kc_r, vcT_r, ksel, vselT, kwin, vwinT, gT, o_out = (
            r.at[mb] for r in (qT_ref, qrT_ref, kc_ref, vcT_ref, ksel_ref, vselT_ref, kwin_ref, vwinT_ref,
                               gT_ref, o_ref))
        selb_scr, q_scr, qr_scr, s_scr, swin_scr, m_scr, acc_scr, oT_scr = (r.at[mb] for r in scratch)

        def stage_q(ref, scr):
            zeros = jnp.zeros((HEAD_DIM, tq), BF16)
            for h in range(N_HEADS):
                qh = ref[h * HEAD_DIM:(h + 1) * HEAD_DIM, :]
                halves = [qh, zeros] if h // HEADS_PER_GROUP == 0 else [zeros, qh]
                scr[:, h * tq:(h + 1) * tq] = jnp.concatenate(halves, axis=0)

        gt = gT[...]

        def gate(br, c):
            h, tsub = divmod(c, nsub)
            g, z = divmod(h, HEADS_PER_GROUP)
            r = g * GATE_ROWS + br * HEADS_PER_GROUP + z
            return gt[r:r + 1, sub(tsub)]

        stage_q(qT, q_scr)
        stage_q(qrT, qr_scr)

        def qk_sel(kt, slot):
            s_all = _dot(ksel[kt], qr_scr[...])
            for c in range(nchunk):
                rb = selb_scr[c // gchunk, kt, 0:bpt, sub(c % nsub)]
                s_scr[slot, :, chunk(c)] = s_all[:, chunk(c)] + jnp.concatenate(
                    [jnp.broadcast_to(rb[i:i + 1, :], (SEL_BLOCK, LANES)) for i in range(bpt)], axis=0)

        kc = kc_r[...]
        ncp = kc.shape[0]
        s_cmp = _dot(kc, q_scr[...])
        k_win = kwin[pl.ds(lo, n_back + 1)].reshape((n_back + 1) * tk, KV_DIM)
        s_win = _dot(k_win, qr_scr[...])
        for c in range(nchunk):
            swin_scr[:, chunk(c)] = s_win[:, chunk(c)] + wbias_ref[case, :, sub(c % nsub)]
        m_scr[...] = jnp.full(m_scr.shape, NEG, F32)
        acc_scr[...] = jnp.zeros(acc_scr.shape, F32)
        yield

        imp = []
        for g in range(N_KV_GROUPS):
            vcT = vcT_r[g * HEAD_DIM:(g + 1) * HEAD_DIM, :]
            imp_parts = []
            for tsub in range(nsub):
                n_idx = lax.broadcasted_iota(jnp.int32, (ncp, LANES), 0)
                t_idx = qbase + tsub * LANES + lax.broadcasted_iota(jnp.int32, (ncp, LANES), 1)
                ok = (n_idx * CMP_STRIDE + (CMP_BLOCK - 1)) <= t_idx
                bias = jnp.where(ok, 0.0, NEG).astype(F32)
                keep = ok.astype(F32)
                p_sum = None
                for z in range(HEADS_PER_GROUP):
                    c = (g * HEADS_PER_GROUP + z) * nsub + tsub
                    s = s_cmp[:, chunk(c)] + bias
                    e = jnp.exp2(s - jnp.max(s, axis=0, keepdims=True)) * keep
                    inv = 1.0 / jnp.maximum(jnp.sum(e, axis=0, keepdims=True), 1e-30)
                    oT_scr[:, chunk(c)] = _dot(vcT, e.astype(BF16)) * (inv * gate(0, c))
                    p_sum = e * inv if p_sum is None else p_sum + e * inv
                imp_parts.append(jnp.dot(mT, p_sum, preferred_element_type=F32, precision=lax.Precision.HIGHEST))
            imp.append(jnp.concatenate(imp_parts, axis=1))
        yield

        j_idx = lax.broadcasted_iota(jnp.int32, (ns, tq), 0)
        cur = (qbase + lax.broadcasted_iota(jnp.int32, (ns, tq), 1)) // SEL_BLOCK
        forced = (j_idx == 0) | (j_idx == cur) | (j_idx == cur - 1)
        for g in range(N_KV_GROUPS):
            impv = jnp.where(j_idx <= cur, jnp.where(forced, FORCE, imp[g]), NEG)
            groups = [impv[a:a + SUBLANES, :] for a in range(0, ns, SUBLANES)]
            sub_idx = lax.broadcasted_iota(jnp.int32, (SUBLANES, tq), 0)
            cnts = [jnp.zeros((SUBLANES, tq), F32) for _ in groups]
            for jp in range(ns):
                row = impv[jp:jp + 1, :]
                for gi, grp in enumerate(groups):
                    if gi > jp // SUBLANES:
                        beats = row >= grp
                    elif gi < jp // SUBLANES:
                        beats = row > grp
                    else:
                        beats = (row > grp) | ((row == grp) & (sub_idx > jp % SUBLANES))
                    cnts[gi] = cnts[gi] + beats.astype(F32)
            cnt = jnp.concatenate(cnts, axis=0)
            sel = (cnt < float(min(N_SEL, ns))) & (impv > 0.5 * NEG)
            selb = jnp.where(sel, 0.0, NEG).astype(F32)
            for kt in range(ns // bpt):
                selb_scr[g, kt, 0:bpt, :] = selb[kt * bpt:(kt + 1) * bpt, :]
        qk_sel(start, 0)
        yield

        def softmax_pv(st, load_s, vT_t, tile_bias=None):
            for g in range(N_KV_GROUPS):
                es, alphas = [], []
                for cg in range(gchunk):
                    c = g * gchunk + cg

                    def scores():
                        s = load_s(c)
                        return s if tile_bias is None else s + tile_bias(c % nsub)

                    m_old = m_scr[st, :, chunk(c)]
                    m_new = jnp.maximum(m_old, jnp.max(scores(), axis=0, keepdims=True))
                    m_scr[st, :, chunk(c)] = m_new
                    es.append(jnp.exp2(scores() - m_new).astype(BF16))
                    alphas.append(jnp.exp2(m_old - m_new))
                lanes = slice(g * gl, (g + 1) * gl)
                pv = _dot(vT_t[g * V_ROWS:(g + 1) * V_ROWS, :], jnp.concatenate(es, axis=1))
                acc_scr[st, :, lanes] = jnp.concatenate(alphas, axis=1) * acc_scr[st, :, lanes] + pv

        def finish(st, br):
            for c in range(nchunk):
                acc = acc_scr[st, :, chunk(c)]
                inv = 1.0 / jnp.maximum(acc[HEAD_DIM:HEAD_DIM + 1, :], 1e-30)
                oT_scr[:, chunk(c)] += acc[0:HEAD_DIM, :] * (inv * gate(br, c))

        for i in range(n_back, -1, -1):
            rows = slice(i * tk, (i + 1) * tk)
            softmax_pv(WIN, lambda c: swin_scr[rows, chunk(c)], vwinT[lo + i])
        finish(WIN, 2)

        def sp_sel(kt, slot, tile_bias=None):
            softmax_pv(SEL, lambda c: s_scr[slot, :, chunk(c)], vselT[kt], tile_bias)

        def finalize():
            finish(SEL, 1)
            for tsub in range(nsub):
                for i in range(N_HEADS // 2):
                    pair = jnp.concatenate([oT_scr[:, chunk((2 * i) * nsub + tsub)],
                                            oT_scr[:, chunk((2 * i + 1) * nsub + tsub)]], axis=0)
                    o_out[sub(tsub), i * 2 * HEAD_DIM:(i + 1) * 2 * HEAD_DIM] = pair.T.astype(BF16)

        yield qk_sel, sp_sel, finalize

    members = [member(mb) for mb in range(nm)]
    for _ in range(3):
        for m in members:
            next(m)
    fns = [next(m) for m in members]

    def each(i, *args):
        for f in fns:
            f[i](*args)

    QK, SP, FINALIZE = 0, 1, 2
    @pl.when(start == 1)
    def _():
        each(QK, 0, 1)
        each(SP, 0, 1)

    def pair_body(j, carry):
        a = start + 2 * j
        each(QK, a + 1, 1)
        each(SP, a, 0)
        each(QK, a + 2, 0)
        each(SP, a + 1, 1)
        return carry

    lax.fori_loop(0, p // 2, pair_body, 0)
    each(SP, p, 0, lambda tsub: wbias_ref[n_back, n_back * tk:(n_back + 1) * tk, sub(tsub)])
    each(FINALIZE)

    for src, dst in zip(cast_in, cast_out):
        dst[...] = src[...].astype(BF16)


def _window_bias(t, n_back):
    r = np.arange(t)[:, None]
    c = np.arange(t)[None, :]
    causal = np.where(r <= c, 0.0, NEG)
    edge = np.where(r > c, 0.0, NEG)
    out = np.zeros((n_back + 1, (n_back + 1) * t, t), np.float32)
    for case in range(n_back + 1):
        for i in range(n_back + 1):
            back = case - i
            tile = NEG if back < 0 else causal if back == 0 else edge if back == n_back else 0.0
            out[case, i * t:(i + 1) * t, :] = tile
    return jnp.asarray(out)


def _attn(qT, qrT, kc, vcT, ksel, vselT, kwin, vwinT, gT, mT, B, S, cast=()):
    tq, tk = ATT_TQ, ATT_TK
    nm = ATT_MEMBERS if B % ATT_MEMBERS == 0 else 1
    assert tq == tk and WINDOW % tk == 0 and tk % SEL_BLOCK == 0 and tq % LANES == 0
    assert tk // SEL_BLOCK <= SUBLANES
    n_back = WINDOW // tk
    assert S // tk > n_back
    P = S // tq
    ncp = kc.shape[1]
    ns = S // SEL_BLOCK
    nq = N_HEADS * tq
    qspec = pl.BlockSpec((nm, NSA_DIM, tq), lambda b, p: (b, 0, p))
    kspec = pl.BlockSpec((nm, S // tk, tk, KV_DIM), lambda b, p: (b, 0, 0, 0))
    vspec = pl.BlockSpec((nm, S // tk, N_KV_GROUPS * V_ROWS, tk), lambda b, p: (b, 0, 0, 0))
    steps = (B // nm) * P
    assert all(w.shape[0] % (steps * BF16_ROWS) == 0 for w in cast)
    slab = lambda w: pl.BlockSpec((w.shape[0] // steps, w.shape[1]), lambda b, p: (b * P + p, 0))
    out = pl.pallas_call(
        functools.partial(_attn_kernel, len(cast)),
        grid=(B // nm, P),
        in_specs=[
            qspec, qspec,
            pl.BlockSpec((nm, ncp, KV_DIM), lambda b, p: (b, 0, 0)),
            pl.BlockSpec((nm, KV_DIM, ncp), lambda b, p: (b, 0, 0)),
            kspec, vspec, kspec, vspec,
            pl.BlockSpec((nm, N_KV_GROUPS * GATE_ROWS, tq), lambda b, p: (b, 0, p)),
            pl.BlockSpec((ns, ncp), lambda b, p: (0, 0)),
            pl.BlockSpec((n_back + 1, (n_back + 1) * tk, tq), lambda b, p: (0, 0, 0)),
        ] + [slab(w) for w in cast],
        out_specs=[pl.BlockSpec((nm, tq, NSA_DIM), lambda b, p: (b, p, 0))] + [slab(w) for w in cast],
        out_shape=[jax.ShapeDtypeStruct((B, S, NSA_DIM), BF16)]
        + [jax.ShapeDtypeStruct(w.shape, BF16) for w in cast],
        scratch_shapes=[
            pltpu.VMEM((nm, N_KV_GROUPS, S // tk, SUBLANES, tq), F32),
            pltpu.VMEM((nm, KV_DIM, nq), BF16),
            pltpu.VMEM((nm, KV_DIM, nq), BF16),
            pltpu.VMEM((nm, 2, tk, nq), F32),
            pltpu.VMEM((nm, (n_back + 1) * tk, nq), F32),
            pltpu.VMEM((nm, 2, 1, nq), F32),
            pltpu.VMEM((nm, 2, V_ROWS, nq), F32),
            pltpu.VMEM((nm, HEAD_DIM, nq), F32),
        ],
        compiler_params=pltpu.CompilerParams(
            dimension_semantics=("arbitrary", "arbitrary"), vmem_limit_bytes=VMEM_LIMIT),
        name="attn",
    )(qT, qrT, kc, vcT, ksel, vselT, kwin, vwinT, gT, mT, _window_bias(tk, n_back), *cast)
    return out[0].reshape(B * S, NSA_DIM), tuple(out[1:])


def _layer_norm(z, g, b):
    mu = jnp.mean(z, axis=-1, keepdims=True)
    zc = z - mu
    var = jnp.mean(zc * zc, axis=-1, keepdims=True)
    return zc * lax.rsqrt(var + LN_EPS) * g + b


def _tail_kernel(alpha, x_ref, mix_ref, o_ref, wg_ref, wco_ref, wno_ref, wo_ref, g1_ref, b1_ref,
                 wup_ref, wdn_ref, g2_ref, b2_ref, out_ref):
    ts = TAIL_SUB
    subs = [slice(i * ts, (i + 1) * ts) for i in range(x_ref.shape[0] // ts)]
    xs = [x_ref[rs, :] for rs in subs]
    xbs = [x.astype(BF16) for x in xs]

    def merge(i):
        rs, x, xb = subs[i], xs[i], xbs[i]
        y_conv = _dot(mix_ref[rs, :], wco_ref[...])
        y_nsa = _dot(o_ref[rs, :], wno_ref[...])
        mixed = (jax.nn.sigmoid(_dot_nt(xb, wg_ref[0:D_MODEL, :])) * y_conv
                 + jax.nn.sigmoid(_dot_nt(xb, wg_ref[D_MODEL:2 * D_MODEL, :])) * y_nsa)
        return alpha * x + _dot(mixed.astype(BF16), wo_ref[...])

    def mlp(x1):
        x1b = x1.astype(BF16)
        ff = jnp.zeros_like(x1)
        for c in range(D_FF // FF_CHUNK):
            h = jnp.maximum(_dot(x1b, wup_ref[:, c * FF_CHUNK:(c + 1) * FF_CHUNK]), 0.0)
            ff = ff + _dot((h * h).astype(BF16), wdn_ref[c * FF_CHUNK:(c + 1) * FF_CHUNK, :])
        return alpha * x1 + ff

    zs = [merge(i) for i in range(len(subs))]
    ys = [mlp(_layer_norm(z, g1_ref[...], b1_ref[...])) for z in zs]
    for rs, y in zip(subs, ys):
        out_ref[rs, :] = _layer_norm(y, g2_ref[...], b2_ref[...])


def _tail(x2, mix, o, wg, wco, wno, wo, g1, b1, wup, wdn, g2, b2, alpha):
    n = x2.shape[0]
    tm = TAIL_TM
    row = lambda i: (i, 0)
    const = lambda i: (0, 0)
    resident = lambda shape: pl.BlockSpec(shape, const, pipeline_mode=pl.Buffered(1))
    return pl.pallas_call(
        functools.partial(_tail_kernel, alpha),
        grid=(n // tm,),
        in_specs=[
            pl.BlockSpec((tm, D_MODEL), row), pl.BlockSpec((tm, CONV_DIM), row), pl.BlockSpec((tm, NSA_DIM), row),
            resident((2 * D_MODEL, D_MODEL)), resident((CONV_DIM, D_MODEL)), resident((NSA_DIM, D_MODEL)),
            resident((D_MODEL, D_MODEL)), resident((1, D_MODEL)), resident((1, D_MODEL)),
            resident((D_MODEL, D_FF)), resident((D_FF, D_MODEL)), resident((1, D_MODEL)), resident((1, D_MODEL)),
        ],
        out_specs=pl.BlockSpec((tm, D_MODEL), row),
        out_shape=jax.ShapeDtypeStruct((n, D_MODEL), F32),
        compiler_params=pltpu.CompilerParams(dimension_semantics=("arbitrary",), vmem_limit_bytes=VMEM_LIMIT),
        name="tail",
    )(x2, mix, o, wg, wco, wno, wo, g1, b1, wup, wdn, g2, b2)


def _rope_tables(S):
    inv = ROPE_THETA ** (-jnp.arange(0, ROPE_DIM, 2, dtype=F32) / ROPE_DIM)
    ang = jnp.arange(S, dtype=F32)[:, None] * inv[None, :]
    cos, sin = jnp.cos(ang), jnp.sin(ang)
    pad_c = jnp.ones((S, HEAD_DIM - ROPE_DIM), F32)
    pad_s = jnp.zeros((S, HEAD_DIM - ROPE_DIM), F32)
    ck = jnp.tile(jnp.concatenate([cos, cos, pad_c], axis=1), (1, N_KV_GROUPS))
    sk = jnp.tile(jnp.concatenate([-sin, sin, pad_s], axis=1), (1, N_KV_GROUPS))
    return ck, sk, cos.T, sin.T


def _overlap_matrix_t(ncp, ns):
    nc = ncp - 1
    i = np.arange(ncp)[None, :]
    j = np.arange(ns)[:, None]
    m = (i * CMP_STRIDE < (j + 1) * SEL_BLOCK) & (i * CMP_STRIDE + CMP_BLOCK > j * SEL_BLOCK) & (i < nc)
    return jnp.asarray(m.astype(np.float32))


def _compress_weights(pe, w1, w2):
    eye = jnp.eye(N_KV_GROUPS, dtype=F32)
    peh = pe.reshape(2, CMP_STRIDE, 1, HEAD_DIM)
    pebig = jnp.broadcast_to(peh, (2, CMP_STRIDE, N_KV_GROUPS, HEAD_DIM)).reshape(2, 1, CMP_STRIDE * KV_DIM)
    pebig = jnp.broadcast_to(pebig, (2, SUBLANES, CMP_STRIDE * KV_DIM))
    w2big = jnp.einsum('ab,hd->ahbd', eye, w2).reshape(N_KV_GROUPS * CMP_HIDDEN, KV_DIM)
    return pebig.astype(BF16), w1.astype(BF16), w2big.astype(BF16)


def _layer(x2, B, S, w_in, conv_w, w_conv_out, pe_k, wk1, wk2, pe_v, wv1, wv2, w_nsa_out, w_o,
           ln1_g, ln1_b, w_up, w_down, ln2_g, ln2_b, alpha):
    ncp = S // CMP_STRIDE
    ns = S // SEL_BLOCK
    wstd, wg, wtr = _wprep(jnp.swapaxes(w_in, 0, 1))
    ck, sk, cq, sq = _rope_tables(S)

    mix, kch, vch, ksel, kwin, qT, qrT, vselT, vwinT, gT = _proj(
        x2, wstd, wtr, conv_w.reshape(CONV_WIDTH, CONV_DIM), ck, sk, cq, sq, B, S)

    pek, wk1b, wk2b = _compress_weights(pe_k, wk1, wk2)
    pev, wv1b, wv2b = _compress_weights(pe_v, wv1, wv2)
    kc, vcT = _compress(kch, vch, pek, pev, wk1b, wv1b, wk2b, wv2b.T, B, ncp)

    tk = ATT_TK
    steps = (B // (ATT_MEMBERS if B % ATT_MEMBERS == 0 else 1)) * (S // ATT_TQ)
    big = (w_up, w_down, w_o)
    in_attn = all(w.shape[0] % (steps * BF16_ROWS) == 0 for w in big)
    o, cast = _attn(qT, qrT, kc, vcT, ksel.reshape(B, S // tk, tk, KV_DIM), vselT,
                    kwin.reshape(B, S // tk, tk, KV_DIM), vwinT, gT, _overlap_matrix_t(ncp, ns), B, S,
                    cast=big if in_attn else ())
    w_up_b, w_down_b, w_o_b = cast if in_attn else tuple(w.astype(BF16) for w in big)

    row = lambda v: v.reshape(1, D_MODEL).astype(F32)
    return _tail(x2, mix, o, wg, w_conv_out.astype(BF16), w_nsa_out.astype(BF16), w_o_b,
                 row(ln1_g), row(ln1_b), w_up_b, w_down_b, row(ln2_g), row(ln2_b), alpha)


def kernel(x, w_in, conv_w, w_conv_out, pe_k_cmp, w_k_cmp1, w_k_cmp2, pe_v_cmp, w_v_cmp1, w_v_cmp2,
           w_nsa_out, w_o, ln1_g, ln1_b, w_up, w_down, ln2_g, ln2_b):
    B, S, D = x.shape
    depth = w_in.shape[0]
    assert D == D_MODEL and w_in.shape[2] == O_END
    assert S % PROJ_TM == 0 and S % ATT_TQ == 0 and (B * S) % TAIL_TM == 0 and S >= WINDOW
    alpha = float((2 * depth) ** 0.25)
    x2 = x.reshape(B * S, D)
    for l in range(depth):
        x2 = _layer(x2, B, S, w_in[l], conv_w[l], w_conv_out[l], pe_k_cmp[l], w_k_cmp1[l], w_k_cmp2[l],
                    pe_v_cmp[l], w_v_cmp1[l], w_v_cmp2[l], w_nsa_out[l], w_o[l], ln1_g[l], ln1_b[l],
                    w_up[l], w_down[l], ln2_g[l], ln2_b[l], alpha)
    return x2.reshape(B, S, D)
```

```python
import functools

import numpy as np
import jax
import jax.numpy as jnp
from jax import lax
from jax.experimental import pallas as pl
from jax.experimental.pallas import tpu as pltpu

F32 = jnp.float32
BF16 = jnp.bfloat16

D_MODEL = 1024
CONV_DIM = D_MODEL // 2
CONV_WIDTH = 3
N_HEADS = 8
HEAD_DIM = 64
N_KV_GROUPS = 2
HEADS_PER_GROUP = N_HEADS // N_KV_GROUPS
NSA_DIM = N_HEADS * HEAD_DIM
KV_DIM = N_KV_GROUPS * HEAD_DIM
ROPE_DIM = HEAD_DIM // 4
ROPE_HALF = ROPE_DIM // 2
ROPE_THETA = 500000.0
CMP_BLOCK = 32
CMP_STRIDE = 16
CMP_HIDDEN = 2 * HEAD_DIM
SEL_BLOCK = 64
N_SEL = 16
WINDOW = 512
N_NSA_BRANCHES = 3
D_FF = 4 * D_MODEL
LN_EPS = 1e-5
NEG = -1e30
FORCE = 1e9
SCALE = HEAD_DIM ** -0.5
LOG2E = 1.4426950408889634

SUBLANES = 8
LANES = 128
BF16_ROWS = 16
V_ROWS = HEAD_DIM + BF16_ROWS

_SPLITS = (CONV_DIM, CONV_DIM, CONV_DIM, NSA_DIM, KV_DIM, KV_DIM, KV_DIM, KV_DIM, KV_DIM, KV_DIM,
           N_HEADS * N_NSA_BRANCHES, D_MODEL, D_MODEL)
_OFFS = np.concatenate([[0], np.cumsum(_SPLITS)]).tolist()
(O_H, O_B, O_C, O_Q, O_KCMP, O_VCMP, O_KSEL, O_VSEL, O_KWIN, O_VWIN, O_GATE, O_GCONV, O_GNSA, O_END) = _OFFS

GATE_ROWS = 16
N_STD = 3 * CONV_DIM + 4 * KV_DIM
N_TR = NSA_DIM + 2 * KV_DIM + N_KV_GROUPS * GATE_ROWS

PROJ_TM = 1024
PROJ_PIECE = 256
ATT_TQ = 256
ATT_TK = 256
ATT_MEMBERS = 2
TAIL_TM = 512
TAIL_SUB = 256
FF_CHUNK = 1024
VMEM_LIMIT = 56 * 1024 * 1024


def _dot(a, b):
    return jnp.dot(a, b, preferred_element_type=F32)


def _dot_nt(a, b):
    return lax.dot_general(a, b, (((1,), (1,)), ((), ())), preferred_element_type=F32)


WPREP_COLS = 256


def _wprep_kernel(wT_ref, wstdT_ref, wgT_ref, wtr_ref):
    r = 0
    for a, b in ((O_H, O_Q), (O_KCMP, O_KSEL), (O_KSEL, O_VSEL), (O_KWIN, O_VWIN)):
        wstdT_ref[r:r + b - a, :] = wT_ref[a:b, :].astype(BF16)
        r += b - a
    wgT_ref[...] = wT_ref[O_GCONV:O_END, :].astype(BF16)
    r = 0
    for a, b in ((O_Q, O_KCMP), (O_VSEL, O_KWIN), (O_VWIN, O_GATE)):
        wtr_ref[r:r + b - a, :] = wT_ref[a:b, :].astype(BF16)
        r += b - a
    gates = wT_ref[O_GATE:O_GCONV, :]
    rows = []
    for g in range(N_KV_GROUPS):
        for br in range(N_NSA_BRANCHES):
            for z in range(HEADS_PER_GROUP):
                src = (g * HEADS_PER_GROUP + z) * N_NSA_BRANCHES + br
                rows.append(gates[src:src + 1, :])
        rows.append(jnp.zeros((GATE_ROWS - N_NSA_BRANCHES * HEADS_PER_GROUP, gates.shape[1]), F32))
    wtr_ref[r:r + N_KV_GROUPS * GATE_ROWS, :] = jnp.concatenate(rows, axis=0).astype(BF16)


def _wprep(wT):
    d = wT.shape[1]
    cb = WPREP_COLS
    assert d % cb == 0 and O_GCONV % SUBLANES == 0
    strip = lambda rows: pl.BlockSpec((rows, cb), lambda i: (0, i))
    return pl.pallas_call(
        _wprep_kernel,
        grid=(d // cb,),
        in_specs=[strip(O_END)],
        out_specs=(strip(N_STD), strip(2 * D_MODEL), strip(N_TR)),
        out_shape=(jax.ShapeDtypeStruct((N_STD, d), BF16), jax.ShapeDtypeStruct((2 * D_MODEL, d), BF16),
                   jax.ShapeDtypeStruct((N_TR, d), BF16)),
        compiler_params=pltpu.CompilerParams(dimension_semantics=("arbitrary",), vmem_limit_bytes=VMEM_LIMIT),
        name="wprep",
    )(wT)


def _proj_kernel(x_ref, wstd_ref, wtr_ref, convw_ref, cosk_ref, sink_ref, cosq_ref, sinq_ref,
                 mix_ref, kch_ref, vch_ref, ksel_ref, kwin_ref,
                 qT_ref, qrT_ref, vselT_ref, vwinT_ref, gT_ref, ubuf, kvbuf):
    tm = x_ref.shape[0]

    @pl.when(pl.program_id(1) == 0)
    def _():
        ubuf[...] = jnp.zeros(ubuf.shape, F32)

    xb = x_ref[...].astype(BF16)
    yT = _dot_nt(wtr_ref[...], xb)
    kk = _dot_nt(xb, wstd_ref[3 * CONV_DIM:N_STD, :])
    halves = [slice(i * PROJ_PIECE, (i + 1) * PROJ_PIECE) for i in range(tm // PROJ_PIECE)]
    hbcs = [_dot_nt(xb[rs, :], wstd_ref[0:3 * CONV_DIM, :]) for rs in halves]

    cw = convw_ref[...]
    head_rows = lax.broadcasted_iota(jnp.int32, (SUBLANES, CONV_DIM), 0)
    prev = ubuf[...]
    for rs, hbc in zip(halves, hbcs):
        u = hbc[:, 2 * CONV_DIM:3 * CONV_DIM] * hbc[:, 0:CONV_DIM]

        def shifted(k):
            r = pltpu.roll(u, k, 0)
            first = jnp.where(head_rows < k, pltpu.roll(prev, k, 0), r[0:SUBLANES, :])
            return jnp.concatenate([first, r[SUBLANES:, :]], axis=0)

        y = cw[0:1, :] * shifted(2) + cw[1:2, :] * shifted(1) + cw[2:3, :] * u
        mix_ref[rs, :] = (hbc[:, CONV_DIM:2 * CONV_DIM] * y).astype(BF16)
        prev = u[u.shape[0] - SUBLANES:, :]
    ubuf[...] = prev

    for j, out in enumerate((kch_ref, vch_ref)):
        kvbuf[j] = kk[:, j * KV_DIM:(j + 1) * KV_DIM]
        for r in range(CMP_STRIDE):
            rows = kvbuf[j, pl.ds(r, tm // CMP_STRIDE, stride=CMP_STRIDE), :]
            out[:, r * KV_DIM:(r + 1) * KV_DIM] = rows.astype(BF16)
    lane = lax.broadcasted_iota(jnp.int32, (tm, KV_DIM), 1)
    first_half = (lane % ROPE_DIM) < ROPE_HALF
    ck = cosk_ref[...]
    sk = sink_ref[...]
    for j, out in ((2, ksel_ref), (3, kwin_ref)):
        k = kk[:, j * KV_DIM:(j + 1) * KV_DIM]
        partner = jnp.where(first_half, pltpu.roll(k, KV_DIM - ROPE_HALF, 1), pltpu.roll(k, ROPE_HALF, 1))
        out[...] = (k * ck + partner * sk).astype(BF16)

    q = yT[0:NSA_DIM, :] * (SCALE * LOG2E)
    qT_ref[...] = q.astype(BF16)
    qrT_ref[...] = q.astype(BF16)
    cq = cosq_ref[...]
    sq = sinq_ref[...]
    for h in range(N_HEADS):
        r0 = h * HEAD_DIM
        x1 = q[r0:r0 + ROPE_HALF, :]
        x2 = q[r0 + ROPE_HALF:r0 + ROPE_DIM, :]
        rot = jnp.concatenate([x1 * cq - x2 * sq, x2 * cq + x1 * sq], axis=0)
        qrT_ref[r0:r0 + ROPE_DIM, :] = rot.astype(BF16)
    tk = vselT_ref.shape[-1]
    ones = jnp.ones((BF16_ROWS, tk), BF16)
    for j, out in enumerate((vselT_ref, vwinT_ref)):
        for i in range(tm // tk):
            for g in range(N_KV_GROUPS):
                r0 = NSA_DIM + j * KV_DIM + g * HEAD_DIM
                out[i, g * V_ROWS:g * V_ROWS + HEAD_DIM, :] = yT[r0:r0 + HEAD_DIM, i * tk:(i + 1) * tk].astype(BF16)
                out[i, g * V_ROWS + HEAD_DIM:(g + 1) * V_ROWS, :] = ones
    gT_ref[...] = jax.nn.sigmoid(yT[NSA_DIM + 2 * KV_DIM:N_TR, :])


def _proj(x2, wstd, wtr, convw, cosk, sink, cosq, sinq, B, S):
    tm, tk = PROJ_TM, ATT_TK
    ns = S // tm
    row = lambda b, s: (b * ns + s, 0)
    const = lambda b, s: (0, 0)
    tok = lambda c, dt: jax.ShapeDtypeStruct((B * S, c), dt)
    cw = CMP_STRIDE * KV_DIM
    chunked = jax.ShapeDtypeStruct((B, S // CMP_STRIDE, cw), BF16)
    chunk_spec = pl.BlockSpec((None, tm // CMP_STRIDE, cw), lambda b, s: (b, s, 0))
    out_shape = (
        tok(CONV_DIM, BF16), chunked, chunked, tok(KV_DIM, BF16), tok(KV_DIM, BF16),
        jax.ShapeDtypeStruct((B, NSA_DIM, S), BF16), jax.ShapeDtypeStruct((B, NSA_DIM, S), BF16),
        jax.ShapeDtypeStruct((B, S // tk, N_KV_GROUPS * V_ROWS, tk), BF16),
        jax.ShapeDtypeStruct((B, S // tk, N_KV_GROUPS * V_ROWS, tk), BF16),
        jax.ShapeDtypeStruct((B, N_KV_GROUPS * GATE_ROWS, S), F32),
    )
    feat = lambda r: pl.BlockSpec((None, r, tm), lambda b, s: (b, 0, s))
    vt = pl.BlockSpec((None, tm // tk, N_KV_GROUPS * V_ROWS, tk), lambda b, s: (b, s, 0, 0))
    return pl.pallas_call(
        _proj_kernel,
        grid=(B, ns),
        in_specs=[
            pl.BlockSpec((tm, D_MODEL), row),
            pl.BlockSpec((N_STD, D_MODEL), const),
            pl.BlockSpec((N_TR, D_MODEL), const),
            pl.BlockSpec((CONV_WIDTH, CONV_DIM), const),
            pl.BlockSpec((tm, KV_DIM), lambda b, s: (s, 0)),
            pl.BlockSpec((tm, KV_DIM), lambda b, s: (s, 0)),
            pl.BlockSpec((ROPE_HALF, tm), lambda b, s: (0, s)),
            pl.BlockSpec((ROPE_HALF, tm), lambda b, s: (0, s)),
        ],
        out_specs=(
            pl.BlockSpec((tm, CONV_DIM), row), chunk_spec, chunk_spec,
            pl.BlockSpec((tm, KV_DIM), row), pl.BlockSpec((tm, KV_DIM), row),
            feat(NSA_DIM), feat(NSA_DIM), vt, vt, feat(N_KV_GROUPS * GATE_ROWS),
        ),
        out_shape=out_shape,
        scratch_shapes=[pltpu.VMEM((SUBLANES, CONV_DIM), F32), pltpu.VMEM((2, tm, KV_DIM), F32)],
        compiler_params=pltpu.CompilerParams(
            dimension_semantics=("arbitrary", "arbitrary"), vmem_limit_bytes=VMEM_LIMIT),
        name="proj",
    )(x2, wstd, wtr, convw, cosk, sink, cosq, sinq)


def _gelu_tanh(x):
    return x * (0.5 * (1.0 + jnp.tanh(np.sqrt(2.0 / np.pi).astype(np.float32) * (x + 0.044715 * (x * x * x)))))


def _compress_kernel(kch_ref, vch_ref, pek_ref, pev_ref, wk1_ref, wv1_ref, wk2_ref, wv2t_ref, kc_ref, vcT_ref,
                     wk1_big, wv1_big):
    nb, ncp = kch_ref.shape[0], kch_ref.shape[1]
    for w_ref, big in ((wk1_ref, wk1_big), (wv1_ref, wv1_big)):
        big[...] = jnp.zeros(big.shape, BF16)
        for half in range(2):
            for r in range(CMP_STRIDE):
                for g in range(N_KV_GROUPS):
                    r0 = r * KV_DIM + g * HEAD_DIM
                    big[half, r0:r0 + HEAD_DIM, g * CMP_HIDDEN:(g + 1) * CMP_HIDDEN] = (
                        w_ref[half * CMP_STRIDE + r])

    def hidden(ch_ref, pe_ref, big):
        ch = ch_ref[...].reshape(nb * ncp, ch_ref.shape[2])
        a = _dot(ch, big[0])
        b = _dot(ch, big[1])
        pe = _dot(pe_ref[0], big[0]) + _dot(pe_ref[1], big[1])
        b_next = pltpu.roll(b, b.shape[0] - 1, 0)
        return _gelu_tanh(a + b_next + pe[0:1, :]).astype(BF16)

    kc = _dot(hidden(kch_ref, pek_ref, wk1_big), wk2_ref[...]).astype(BF16)
    kc_ref[...] = kc.reshape(nb, ncp, kc.shape[1])
    vT = _dot_nt(wv2t_ref[...], hidden(vch_ref, pev_ref, wv1_big)).astype(BF16)
    for b in range(nb):
        vcT_ref[b] = vT[:, b * ncp:(b + 1) * ncp]


def _compress(kch, vch, pek, pev, wk1, wv1, wk2, wv2t, B, ncp):
    cw = CMP_STRIDE * KV_DIM
    gh = N_KV_GROUPS * CMP_HIDDEN
    full = lambda shape: pl.BlockSpec(shape, lambda i: (0,) * len(shape))
    return pl.pallas_call(
        _compress_kernel,
        grid=(1,),
        in_specs=[
            full((B, ncp, cw)), full((B, ncp, cw)),
            full((2, SUBLANES, cw)), full((2, SUBLANES, cw)),
            full((CMP_BLOCK, HEAD_DIM, CMP_HIDDEN)), full((CMP_BLOCK, HEAD_DIM, CMP_HIDDEN)),
            full((gh, KV_DIM)), full((KV_DIM, gh)),
        ],
        out_specs=(full((B, ncp, KV_DIM)), full((B, KV_DIM, ncp))),
        out_shape=(jax.ShapeDtypeStruct((B, ncp, KV_DIM), BF16), jax.ShapeDtypeStruct((B, KV_DIM, ncp), BF16)),
        scratch_shapes=[pltpu.VMEM((2, cw, gh), BF16), pltpu.VMEM((2, cw, gh), BF16)],
        compiler_params=pltpu.CompilerParams(dimension_semantics=("arbitrary",), vmem_limit_bytes=VMEM_LIMIT),
        name="compress",
    )(kch, vch, pek, pev, wk1, wv1, wk2, wv2t)


def _attn_kernel(n_cast, qT_ref, qrT_ref, kc_ref, vcT_ref, ksel_ref, vselT_ref, kwin_ref, vwinT_ref, gT_ref,
                 mT_ref, wbias_ref, *rest):
    cast_in, o_ref, cast_out = rest[:n_cast], rest[n_cast], rest[n_cast + 1:2 * n_cast + 1]
    scratch = rest[2 * n_cast + 1:]
    nm = qT_ref.shape[0]
    tq = qT_ref.shape[2]
    tk = ksel_ref.shape[2]
    nsub = tq // LANES
    nchunk = N_HEADS * nsub
    gchunk = HEADS_PER_GROUP * nsub
    gl = HEADS_PER_GROUP * tq
    ns = mT_ref.shape[0]
    bpt = tk // SEL_BLOCK
    p = pl.program_id(1)
    qbase = p * tq
    start = p % 2
    n_back = WINDOW // tk
    case = jnp.minimum(p, n_back)
    lo = p - case
    SEL, WIN = 0, 1
    mT = mT_ref[...]

    def chunk(c):
        return slice(c * LANES, (c + 1) * LANES)

    def sub(tsub):
        return slice(tsub * LANES, (tsub + 1) * LANES)

    def member(mb):
        qT, qrT, kc_r, vcT_r, ksel, vselT, kwin, vwinT, gT, o_out = (
            r.at[mb] for r in (qT_ref, qrT_ref, kc_ref, vcT_ref, ksel_ref, vselT_ref, kwin_ref, vwinT_ref,
                               gT_ref, o_ref))
        selb_scr, q_scr, qr_scr, s_scr, swin_scr, m_scr, acc_scr, oT_scr = (r.at[mb] for r in scratch)

        def stage_q(ref, scr):
            zeros = jnp.zeros((HEAD_DIM, tq), BF16)
            for h in range(N_HEADS):
                qh = ref[h * HEAD_DIM:(h + 1) * HEAD_DIM, :]
                halves = [qh, zeros] if h // HEADS_PER_GROUP == 0 else [zeros, qh]
                scr[:, h * tq:(h + 1) * tq] = jnp.concatenate(halves, axis=0)

        gt = gT[...]

        def gate(br, c):
            h, tsub = divmod(c, nsub)
            g, z = divmod(h, HEADS_PER_GROUP)
            r = g * GATE_ROWS + br * HEADS_PER_GROUP + z
            return gt[r:r + 1, sub(tsub)]

        stage_q(qT, q_scr)
        stage_q(qrT, qr_scr)

        def qk_sel(kt, slot):
            s_all = _dot(ksel[kt], qr_scr[...])
            for c in range(nchunk):
                rb = selb_scr[c // gchunk, kt, 0:bpt, sub(c % nsub)]
                s_scr[slot, :, chunk(c)] = s_all[:, chunk(c)] + jnp.concatenate(
                    [jnp.broadcast_to(rb[i:i + 1, :], (SEL_BLOCK, LANES)) for i in range(bpt)], axis=0)

        kc = kc_r[...]
        ncp = kc.shape[0]
        s_cmp = _dot(kc, q_scr[...])
        k_win = kwin[pl.ds(lo, n_back + 1)].reshape((n_back + 1) * tk, KV_DIM)
        s_win = _dot(k_win, qr_scr[...])
        for c in range(nchunk):
            swin_scr[:, chunk(c)] = s_win[:, chunk(c)] + wbias_ref[case, :, sub(c % nsub)]
        m_scr[...] = jnp.full(m_scr.shape, NEG, F32)
        acc_scr[...] = jnp.zeros(acc_scr.shape, F32)
        yield

        imp = []
        for g in range(N_KV_GROUPS):
            vcT = vcT_r[g * HEAD_DIM:(g + 1) * HEAD_DIM, :]
            imp_parts = []
            for tsub in range(nsub):
                n_idx = lax.broadcasted_iota(jnp.int32, (ncp, LANES), 0)
                t_idx = qbase + tsub * LANES + lax.broadcasted_iota(jnp.int32, (ncp, LANES), 1)
                ok = (n_idx * CMP_STRIDE + (CMP_BLOCK - 1)) <= t_idx
                bias = jnp.where(ok, 0.0, NEG).astype(F32)
                keep = ok.astype(F32)
                p_sum = None
                for z in range(HEADS_PER_GROUP):
                    c = (g * HEADS_PER_GROUP + z) * nsub + tsub
                    s = s_cmp[:, chunk(c)] + bias
                    e = jnp.exp2(s - jnp.max(s, axis=0, keepdims=True)) * keep
                    inv = 1.0 / jnp.maximum(jnp.sum(e, axis=0, keepdims=True), 1e-30)
                    oT_scr[:, chunk(c)] = _dot(vcT, e.astype(BF16)) * (inv * gate(0, c))
                    p_sum = e * inv if p_sum is None else p_sum + e * inv
                imp_parts.append(jnp.dot(mT, p_sum, preferred_element_type=F32, precision=lax.Precision.HIGHEST))
            imp.append(jnp.concatenate(imp_parts, axis=1))
        yield

        j_idx = lax.broadcasted_iota(jnp.int32, (ns, tq), 0)
        cur = (qbase + lax.broadcasted_iota(jnp.int32, (ns, tq), 1)) // SEL_BLOCK
        forced = (j_idx == 0) | (j_idx == cur) | (j_idx == cur - 1)
        for g in range(N_KV_GROUPS):
            impv = jnp.where(j_idx <= cur, jnp.where(forced, FORCE, imp[g]), NEG)
            groups = [impv[a:a + SUBLANES, :] for a in range(0, ns, SUBLANES)]
            sub_idx = lax.broadcasted_iota(jnp.int32, (SUBLANES, tq), 0)
            cnts = [jnp.zeros((SUBLANES, tq), F32) for _ in groups]
            for jp in range(ns):
                row = impv[jp:jp + 1, :]
                for gi, grp in enumerate(groups):
                    if gi > jp // SUBLANES:
                        beats = row >= grp
                    elif gi < jp // SUBLANES:
                        beats = row > grp
                    else:
                        beats = (row > grp) | ((row == grp) & (sub_idx > jp % SUBLANES))
                    cnts[gi] = cnts[gi] + beats.astype(F32)
            cnt = jnp.concatenate(cnts, axis=0)
            sel = (cnt < float(min(N_SEL, ns))) & (impv > 0.5 * NEG)
            selb = jnp.where(sel, 0.0, NEG).astype(F32)
            for kt in range(ns // bpt):
                selb_scr[g, kt, 0:bpt, :] = selb[kt * bpt:(kt + 1) * bpt, :]
        qk_sel(start, 0)
        yield

        def softmax_pv(st, load_s, vT_t, tile_bias=None):
            for g in range(N_KV_GROUPS):
                es, alphas = [], []
                for cg in range(gchunk):
                    c = g * gchunk + cg

                    def scores():
                        s = load_s(c)
                        return s if tile_bias is None else s + tile_bias(c % nsub)

                    m_old = m_scr[st, :, chunk(c)]
                    m_new = jnp.maximum(m_old, jnp.max(scores(), axis=0, keepdims=True))
                    m_scr[st, :, chunk(c)] = m_new
                    es.append(jnp.exp2(scores() - m_new).astype(BF16))
                    alphas.append(jnp.exp2(m_old - m_new))
                lanes = slice(g * gl, (g + 1) * gl)
                pv = _dot(vT_t[g * V_ROWS:(g + 1) * V_ROWS, :], jnp.concatenate(es, axis=1))
                acc_scr[st, :, lanes] = jnp.concatenate(alphas, axis=1) * acc_scr[st, :, lanes] + pv

        def finish(st, br):
            for c in range(nchunk):
                acc = acc_scr[st, :, chunk(c)]
                inv = 1.0 / jnp.maximum(acc[HEAD_DIM:HEAD_DIM + 1, :], 1e-30)
                oT_scr[:, chunk(c)] += acc[0:HEAD_DIM, :] * (inv * gate(br, c))

        for i in range(n_back, -1, -1):
            rows = slice(i * tk, (i + 1) * tk)
            softmax_pv(WIN, lambda c: swin_scr[rows, chunk(c)], vwinT[lo + i])
        finish(WIN, 2)

        def sp_sel(kt, slot, tile_bias=None):
            softmax_pv(SEL, lambda c: s_scr[slot, :, chunk(c)], vselT[kt], tile_bias)

        def finalize():
            finish(SEL, 1)
            for tsub in range(nsub):
                for i in range(N_HEADS // 2):
                    pair = jnp.concatenate([oT_scr[:, chunk((2 * i) * nsub + tsub)],
                                            oT_scr[:, chunk((2 * i + 1) * nsub + tsub)]], axis=0)
                    o_out[sub(tsub), i * 2 * HEAD_DIM:(i + 1) * 2 * HEAD_DIM] = pair.T.astype(BF16)

        yield qk_sel, sp_sel, finalize

    members = [member(mb) for mb in range(nm)]
    for _ in range(3):
        for m in members:
            next(m)
    fns = [next(m) for m in members]

    def each(i, *args):
        for f in fns:
            f[i](*args)

    QK, SP, FINALIZE = 0, 1, 2
    @pl.when(start == 1)
    def _():
        each(QK, 0, 1)
        each(SP, 0, 1)

    def pair_body(j, carry):
        a = start + 2 * j
        each(QK, a + 1, 1)
        each(SP, a, 0)
        each(QK, a + 2, 0)
        each(SP, a + 1, 1)
        return carry

    lax.fori_loop(0, p // 2, pair_body, 0)
    each(SP, p, 0, lambda tsub: wbias_ref[n_back, n_back * tk:(n_back + 1) * tk, sub(tsub)])
    each(FINALIZE)

    for src, dst in zip(cast_in, cast_out):
        dst[...] = src[...].astype(BF16)


def _window_bias(t, n_back):
    r = np.arange(t)[:, None]
    c = np.arange(t)[None, :]
    causal = np.where(r <= c, 0.0, NEG)
    edge = np.where(r > c, 0.0, NEG)
    out = np.zeros((n_back + 1, (n_back + 1) * t, t), np.float32)
    for case in range(n_back + 1):
        for i in range(n_back + 1):
            back = case - i
            tile = NEG if back < 0 else causal if back == 0 else edge if back == n_back else 0.0
            out[case, i * t:(i + 1) * t, :] = tile
    return jnp.asarray(out)


def _attn(qT, qrT, kc, vcT, ksel, vselT, kwin, vwinT, gT, mT, B, S, cast=()):
    tq, tk = ATT_TQ, ATT_TK
    nm = ATT_MEMBERS if B % ATT_MEMBERS == 0 else 1
    assert tq == tk and WINDOW % tk == 0 and tk % SEL_BLOCK == 0 and tq % LANES == 0
    assert tk // SEL_BLOCK <= SUBLANES
    n_back = WINDOW // tk
    assert S // tk > n_back
    P = S // tq
    ncp = kc.shape[1]
    ns = S // SEL_BLOCK
    nq = N_HEADS * tq
    qspec = pl.BlockSpec((nm, NSA_DIM, tq), lambda b, p: (b, 0, p))
    kspec = pl.BlockSpec((nm, S // tk, tk, KV_DIM), lambda b, p: (b, 0, 0, 0))
    vspec = pl.BlockSpec((nm, S // tk, N_KV_GROUPS * V_ROWS, tk), lambda b, p: (b, 0, 0, 0))
    steps = (B // nm) * P
    assert all(w.shape[0] % (steps * BF16_ROWS) == 0 for w in cast)
    slab = lambda w: pl.BlockSpec((w.shape[0] // steps, w.shape[1]), lambda b, p: (b * P + p, 0))
    out = pl.pallas_call(
        functools.partial(_attn_kernel, len(cast)),
        grid=(B // nm, P),
        in_specs=[
            qspec, qspec,
            pl.BlockSpec((nm, ncp, KV_DIM), lambda b, p: (b, 0, 0)),
            pl.BlockSpec((nm, KV_DIM, ncp), lambda b, p: (b, 0, 0)),
            kspec, vspec, kspec, vspec,
            pl.BlockSpec((nm, N_KV_GROUPS * GATE_ROWS, tq), lambda b, p: (b, 0, p)),
            pl.BlockSpec((ns, ncp), lambda b, p: (0, 0)),
            pl.BlockSpec((n_back + 1, (n_back + 1) * tk, tq), lambda b, p: (0, 0, 0)),
        ] + [slab(w) for w in cast],
        out_specs=[pl.BlockSpec((nm, tq, NSA_DIM), lambda b, p: (b, p, 0))] + [slab(w) for w in cast],
        out_shape=[jax.ShapeDtypeStruct((B, S, NSA_DIM), BF16)]
        + [jax.ShapeDtypeStruct(w.shape, BF16) for w in cast],
        scratch_shapes=[
            pltpu.VMEM((nm, N_KV_GROUPS, S // tk, SUBLANES, tq), F32),
            pltpu.VMEM((nm, KV_DIM, nq), BF16),
            pltpu.VMEM((nm, KV_DIM, nq), BF16),
            pltpu.VMEM((nm, 2, tk, nq), F32),
            pltpu.VMEM((nm, (n_back + 1) * tk, nq), F32),
            pltpu.VMEM((nm, 2, 1, nq), F32),
            pltpu.VMEM((nm, 2, V_ROWS, nq), F32),
            pltpu.VMEM((nm, HEAD_DIM, nq), F32),
        ],
        compiler_params=pltpu.CompilerParams(
            dimension_semantics=("arbitrary", "arbitrary"), vmem_limit_bytes=VMEM_LIMIT),
        name="attn",
    )(qT, qrT, kc, vcT, ksel, vselT, kwin, vwinT, gT, mT, _window_bias(tk, n_back), *cast)
    return out[0].reshape(B * S, NSA_DIM), tuple(out[1:])


def _layer_norm(z, g, b):
    mu = jnp.mean(z, axis=-1, keepdims=True)
    zc = z - mu
    var = jnp.mean(zc * zc, axis=-1, keepdims=True)
    return zc * lax.rsqrt(var + LN_EPS) * g + b


def _tail_kernel(alpha, x_ref, mix_ref, o_ref, wg_ref, wco_ref, wno_ref, wo_ref, g1_ref, b1_ref,
                 wup_ref, wdn_ref, g2_ref, b2_ref, out_ref):
    ts = TAIL_SUB
    subs = [slice(i * ts, (i + 1) * ts) for i in range(x_ref.shape[0] // ts)]
    xs = [x_ref[rs, :] for rs in subs]
    xbs = [x.astype(BF16) for x in xs]

    def merge(i):
        rs, x, xb = subs[i], xs[i], xbs[i]
        y_conv = _dot(mix_ref[rs, :], wco_ref[...])
        y_nsa = _dot(o_ref[rs, :], wno_ref[...])
        mixed = (jax.nn.sigmoid(_dot_nt(xb, wg_ref[0:D_MODEL, :])) * y_conv
                 + jax.nn.sigmoid(_dot_nt(xb, wg_ref[D_MODEL:2 * D_MODEL, :])) * y_nsa)
        return alpha * x + _dot(mixed.astype(BF16), wo_ref[...])

    def mlp(x1):
        x1b = x1.astype(BF16)
        ff = jnp.zeros_like(x1)
        for c in range(D_FF // FF_CHUNK):
            h = jnp.maximum(_dot(x1b, wup_ref[:, c * FF_CHUNK:(c + 1) * FF_CHUNK]), 0.0)
            ff = ff + _dot((h * h).astype(BF16), wdn_ref[c * FF_CHUNK:(c + 1) * FF_CHUNK, :])
        return alpha * x1 + ff

    zs = [merge(i) for i in range(len(subs))]
    ys = [mlp(_layer_norm(z, g1_ref[...], b1_ref[...])) for z in zs]
    for rs, y in zip(subs, ys):
        out_ref[rs, :] = _layer_norm(y, g2_ref[...], b2_ref[...])


def _tail(x2, mix, o, wg, wco, wno, wo, g1, b1, wup, wdn, g2, b2, alpha):
    n = x2.shape[0]
    tm = TAIL_TM
    row = lambda i: (i, 0)
    const = lambda i: (0, 0)
    resident = lambda shape: pl.BlockSpec(shape, const, pipeline_mode=pl.Buffered(1))
    return pl.pallas_call(
        functools.partial(_tail_kernel, alpha),
        grid=(n // tm,),
        in_specs=[
            pl.BlockSpec((tm, D_MODEL), row), pl.BlockSpec((tm, CONV_DIM), row), pl.BlockSpec((tm, NSA_DIM), row),
            resident((2 * D_MODEL, D_MODEL)), resident((CONV_DIM, D_MODEL)), resident((NSA_DIM, D_MODEL)),
            resident((D_MODEL, D_MODEL)), resident((1, D_MODEL)), resident((1, D_MODEL)),
            resident((D_MODEL, D_FF)), resident((D_FF, D_MODEL)), resident((1, D_MODEL)), resident((1, D_MODEL)),
        ],
        out_specs=pl.BlockSpec((tm, D_MODEL), row),
        out_shape=jax.ShapeDtypeStruct((n, D_MODEL), F32),
        compiler_params=pltpu.CompilerParams(dimension_semantics=("arbitrary",), vmem_limit_bytes=VMEM_LIMIT),
        name="tail",
    )(x2, mix, o, wg, wco, wno, wo, g1, b1, wup, wdn, g2, b2)


def _rope_tables(S):
    inv = ROPE_THETA ** (-np.arange(0, ROPE_DIM, 2, dtype=np.float64) / ROPE_DIM)
    ang = np.arange(S, dtype=np.float64)[:, None] * inv[None, :]
    cos, sin = np.cos(ang), np.sin(ang)
    pad_c = np.ones((S, HEAD_DIM - ROPE_DIM))
    pad_s = np.zeros((S, HEAD_DIM - ROPE_DIM))
    ck = np.tile(np.concatenate([cos, cos, pad_c], axis=1), (1, N_KV_GROUPS))
    sk = np.tile(np.concatenate([-sin, sin, pad_s], axis=1), (1, N_KV_GROUPS))
    return tuple(jnp.asarray(t.astype(np.float32)) for t in (ck, sk, cos.T, sin.T))


def _overlap_matrix_t(ncp, ns):
    nc = ncp - 1
    i = np.arange(ncp)[None, :]
    j = np.arange(ns)[:, None]
    m = (i * CMP_STRIDE < (j + 1) * SEL_BLOCK) & (i * CMP_STRIDE + CMP_BLOCK > j * SEL_BLOCK) & (i < nc)
    return jnp.asarray(m.astype(np.float32))


def _compress_weights(pe, w1, w2):
    eye = jnp.eye(N_KV_GROUPS, dtype=F32)
    peh = pe.reshape(2, CMP_STRIDE, 1, HEAD_DIM)
    pebig = jnp.broadcast_to(peh, (2, CMP_STRIDE, N_KV_GROUPS, HEAD_DIM)).reshape(2, 1, CMP_STRIDE * KV_DIM)
    pebig = jnp.broadcast_to(pebig, (2, SUBLANES, CMP_STRIDE * KV_DIM))
    w2big = jnp.einsum('ab,hd->ahbd', eye, w2).reshape(N_KV_GROUPS * CMP_HIDDEN, KV_DIM)
    return pebig.astype(BF16), w1.astype(BF16), w2big.astype(BF16)


def _layer(x2, B, S, w_in, conv_w, w_conv_out, pe_k, wk1, wk2, pe_v, wv1, wv2, w_nsa_out, w_o,
           ln1_g, ln1_b, w_up, w_down, ln2_g, ln2_b, alpha):
    ncp = S // CMP_STRIDE
    ns = S // SEL_BLOCK
    wstd, wg, wtr = _wprep(jnp.swapaxes(w_in, 0, 1))
    ck, sk, cq, sq = _rope_tables(S)

    mix, kch, vch, ksel, kwin, qT, qrT, vselT, vwinT, gT = _proj(
        x2, wstd, wtr, conv_w.reshape(CONV_WIDTH, CONV_DIM), ck, sk, cq, sq, B, S)

    pek, wk1b, wk2b = _compress_weights(pe_k, wk1, wk2)
    pev, wv1b, wv2b = _compress_weights(pe_v, wv1, wv2)
    kc, vcT = _compress(kch, vch, pek, pev, wk1b, wv1b, wk2b, wv2b.T, B, ncp)

    tk = ATT_TK
    steps = (B // (ATT_MEMBERS if B % ATT_MEMBERS == 0 else 1)) * (S // ATT_TQ)
    big = (w_up, w_down, w_o, w_conv_out, w_nsa_out)
    in_attn = all(w.shape[0] % (steps * BF16_ROWS) == 0 for w in big)
    o, cast = _attn(qT, qrT, kc, vcT, ksel.reshape(B, S // tk, tk, KV_DIM), vselT,
                    kwin.reshape(B, S // tk, tk, KV_DIM), vwinT, gT, _overlap_matrix_t(ncp, ns), B, S,
                    cast=big if in_attn else ())
    w_up_b, w_down_b, w_o_b, wco_b, wno_b = cast if in_attn else tuple(w.astype(BF16) for w in big)

    row = lambda v: v.reshape(1, D_MODEL).astype(F32)
    return _tail(x2, mix, o, wg, wco_b, wno_b, w_o_b,
                 row(ln1_g), row(ln1_b), w_up_b, w_down_b, row(ln2_g), row(ln2_b), alpha)


def kernel(x, w_in, conv_w, w_conv_out, pe_k_cmp, w_k_cmp1, w_k_cmp2, pe_v_cmp, w_v_cmp1, w_v_cmp2,
           w_nsa_out, w_o, ln1_g, ln1_b, w_up, w_down, ln2_g, ln2_b):
    B, S, D = x.shape
    depth = w_in.shape[0]
    assert D == D_MODEL and w_in.shape[2] == O_END
    assert S % PROJ_TM == 0 and S % ATT_TQ == 0 and (B * S) % TAIL_TM == 0 and S >= WINDOW
    alpha = float((2 * depth) ** 0.25)
    x2 = x.reshape(B * S, D)
    for l in range(depth):
        x2 = _layer(x2, B, S, w_in[l], conv_w[l], w_conv_out[l], pe_k_cmp[l], w_k_cmp1[l], w_k_cmp2[l],
                    pe_v_cmp[l], w_v_cmp1[l], w_v_cmp2[l], w_nsa_out[l], w_o[l], ln1_g[l], ln1_b[l],
                    w_up[l], w_down[l], ln2_g[l], ln2_b[l], alpha)
    return x2.reshape(B, S, D)
```

```python
import functools

import numpy as np
import jax
import jax.numpy as jnp
from jax import lax
from jax.experimental import pallas as pl
from jax.experimental.pallas import tpu as pltpu

F32 = jnp.float32
BF16 = jnp.bfloat16

D_MODEL = 1024
CONV_DIM = D_MODEL // 2
CONV_WIDTH = 3
N_HEADS = 8
HEAD_DIM = 64
N_KV_GROUPS = 2
HEADS_PER_GROUP = N_HEADS // N_KV_GROUPS
NSA_DIM = N_HEADS * HEAD_DIM
KV_DIM = N_KV_GROUPS * HEAD_DIM
ROPE_DIM = HEAD_DIM // 4
ROPE_HALF = ROPE_DIM // 2
ROPE_THETA = 500000.0
CMP_BLOCK = 32
CMP_STRIDE = 16
CMP_HIDDEN = 2 * HEAD_DIM
SEL_BLOCK = 64
N_SEL = 16
WINDOW = 512
N_NSA_BRANCHES = 3
D_FF = 4 * D_MODEL
LN_EPS = 1e-5
NEG = -1e30
FORCE = 1e9
SCALE = HEAD_DIM ** -0.5
LOG2E = 1.4426950408889634

SUBLANES = 8
LANES = 128
BF16_ROWS = 16
V_ROWS = HEAD_DIM + BF16_ROWS

_SPLITS = (CONV_DIM, CONV_DIM, CONV_DIM, NSA_DIM, KV_DIM, KV_DIM, KV_DIM, KV_DIM, KV_DIM, KV_DIM,
           N_HEADS * N_NSA_BRANCHES, D_MODEL, D_MODEL)
_OFFS = np.concatenate([[0], np.cumsum(_SPLITS)]).tolist()
(O_H, O_B, O_C, O_Q, O_KCMP, O_VCMP, O_KSEL, O_VSEL, O_KWIN, O_VWIN, O_GATE, O_GCONV, O_GNSA, O_END) = _OFFS

GATE_ROWS = 16
N_STD = 3 * CONV_DIM + 4 * KV_DIM
N_TR = NSA_DIM + 2 * KV_DIM + N_KV_GROUPS * GATE_ROWS

PROJ_TM = 1024
PROJ_PIECE = 256
ATT_TQ = 256
ATT_TK = 256
ATT_MEMBERS = 2
TAIL_TM = 512
TAIL_SUB = 256
FF_CHUNK = 1024
VMEM_LIMIT = 56 * 1024 * 1024


def _dot(a, b):
    return jnp.dot(a, b, preferred_element_type=F32)


def _dot_nt(a, b):
    return lax.dot_general(a, b, (((1,), (1,)), ((), ())), preferred_element_type=F32)


WPREP_COLS = 256


def _wprep_kernel(wT_ref, wstdT_ref, wgT_ref, wtr_ref):
    r = 0
    for a, b in ((O_H, O_Q), (O_KCMP, O_KSEL), (O_KSEL, O_VSEL), (O_KWIN, O_VWIN)):
        wstdT_ref[r:r + b - a, :] = wT_ref[a:b, :].astype(BF16)
        r += b - a
    wgT_ref[...] = wT_ref[O_GCONV:O_END, :].astype(BF16)
    r = 0
    for a, b in ((O_Q, O_KCMP), (O_VSEL, O_KWIN), (O_VWIN, O_GATE)):
        wtr_ref[r:r + b - a, :] = wT_ref[a:b, :].astype(BF16)
        r += b - a
    gates = wT_ref[O_GATE:O_GCONV, :]
    rows = []
    for g in range(N_KV_GROUPS):
        for br in range(N_NSA_BRANCHES):
            for z in range(HEADS_PER_GROUP):
                src = (g * HEADS_PER_GROUP + z) * N_NSA_BRANCHES + br
                rows.append(gates[src:src + 1, :])
        rows.append(jnp.zeros((GATE_ROWS - N_NSA_BRANCHES * HEADS_PER_GROUP, gates.shape[1]), F32))
    wtr_ref[r:r + N_KV_GROUPS * GATE_ROWS, :] = jnp.concatenate(rows, axis=0).astype(BF16)


def _wprep(wT):
    d = wT.shape[1]
    cb = WPREP_COLS
    assert d % cb == 0 and O_GCONV % SUBLANES == 0
    strip = lambda rows: pl.BlockSpec((rows, cb), lambda i: (0, i))
    return pl.pallas_call(
        _wprep_kernel,
        grid=(d // cb,),
        in_specs=[strip(O_END)],
        out_specs=(strip(N_STD), strip(2 * D_MODEL), strip(N_TR)),
        out_shape=(jax.ShapeDtypeStruct((N_STD, d), BF16), jax.ShapeDtypeStruct((2 * D_MODEL, d), BF16),
                   jax.ShapeDtypeStruct((N_TR, d), BF16)),
        compiler_params=pltpu.CompilerParams(dimension_semantics=("arbitrary",), vmem_limit_bytes=VMEM_LIMIT),
        name="wprep",
    )(wT)


def _proj_kernel(x_ref, wstd_ref, wtr_ref, convw_ref, cosk_ref, sink_ref, cosq_ref, sinq_ref,
                 mix_ref, kch_ref, vch_ref, ksel_ref, kwin_ref,
                 qT_ref, qrT_ref, vselT_ref, vwinT_ref, gT_ref, ubuf, kvbuf):
    tm = x_ref.shape[0]

    @pl.when(pl.program_id(1) == 0)
    def _():
        ubuf[...] = jnp.zeros(ubuf.shape, F32)

    xb = x_ref[...].astype(BF16)
    yT = _dot_nt(wtr_ref[...], xb)
    kk = _dot_nt(xb, wstd_ref[3 * CONV_DIM:N_STD, :])
    pieces = [slice(i * PROJ_PIECE, (i + 1) * PROJ_PIECE) for i in range(tm // PROJ_PIECE)]
    hbcs = [_dot_nt(xb[rs, :], wstd_ref[0:3 * CONV_DIM, :]) for rs in pieces]

    cw = convw_ref[...]
    head_rows = lax.broadcasted_iota(jnp.int32, (SUBLANES, CONV_DIM), 0)
    prev = ubuf[...]
    for rs, hbc in zip(pieces, hbcs):
        u = hbc[:, 2 * CONV_DIM:3 * CONV_DIM] * hbc[:, 0:CONV_DIM]

        def shifted(k):
            r = pltpu.roll(u, k, 0)
            first = jnp.where(head_rows < k, pltpu.roll(prev, k, 0), r[0:SUBLANES, :])
            return jnp.concatenate([first, r[SUBLANES:, :]], axis=0)

        y = cw[0:1, :] * shifted(2) + cw[1:2, :] * shifted(1) + cw[2:3, :] * u
        mix_ref[rs, :] = (hbc[:, CONV_DIM:2 * CONV_DIM] * y).astype(BF16)
        prev = u[u.shape[0] - SUBLANES:, :]
    ubuf[...] = prev

    for j, out in enumerate((kch_ref, vch_ref)):
        kvbuf[j] = kk[:, j * KV_DIM:(j + 1) * KV_DIM]
        for r in range(CMP_STRIDE):
            rows = kvbuf[j, pl.ds(r, tm // CMP_STRIDE, stride=CMP_STRIDE), :]
            out[:, r * KV_DIM:(r + 1) * KV_DIM] = rows.astype(BF16)
    lane = lax.broadcasted_iota(jnp.int32, (tm, KV_DIM), 1)
    first_half = (lane % ROPE_DIM) < ROPE_HALF
    ck = cosk_ref[...]
    sk = sink_ref[...]
    for j, out in ((2, ksel_ref), (3, kwin_ref)):
        k = kk[:, j * KV_DIM:(j + 1) * KV_DIM]
        partner = jnp.where(first_half, pltpu.roll(k, KV_DIM - ROPE_HALF, 1), pltpu.roll(k, ROPE_HALF, 1))
        out[...] = (k * ck + partner * sk).astype(BF16)

    q = yT[0:NSA_DIM, :] * (SCALE * LOG2E)
    qT_ref[...] = q.astype(BF16)
    qrT_ref[...] = q.astype(BF16)
    cq = cosq_ref[...]
    sq = sinq_ref[...]
    for h in range(N_HEADS):
        r0 = h * HEAD_DIM
        x1 = q[r0:r0 + ROPE_HALF, :]
        x2 = q[r0 + ROPE_HALF:r0 + ROPE_DIM, :]
        rot = jnp.concatenate([x1 * cq - x2 * sq, x2 * cq + x1 * sq], axis=0)
        qrT_ref[r0:r0 + ROPE_DIM, :] = rot.astype(BF16)
    tk = vselT_ref.shape[-1]
    ones = jnp.ones((BF16_ROWS, tk), BF16)
    for j, out in enumerate((vselT_ref, vwinT_ref)):
        for i in range(tm // tk):
            for g in range(N_KV_GROUPS):
                r0 = NSA_DIM + j * KV_DIM + g * HEAD_DIM
                out[i, g * V_ROWS:g * V_ROWS + HEAD_DIM, :] = yT[r0:r0 + HEAD_DIM, i * tk:(i + 1) * tk].astype(BF16)
                out[i, g * V_ROWS + HEAD_DIM:(g + 1) * V_ROWS, :] = ones
    gT_ref[...] = jax.nn.sigmoid(yT[NSA_DIM + 2 * KV_DIM:N_TR, :])


def _proj(x2, wstd, wtr, convw, cosk, sink, cosq, sinq, B, S):
    tm, tk = PROJ_TM, ATT_TK
    ns = S // tm
    row = lambda b, s: (b * ns + s, 0)
    const = lambda b, s: (0, 0)
    tok = lambda c, dt: jax.ShapeDtypeStruct((B * S, c), dt)
    cw = CMP_STRIDE * KV_DIM
    chunked = jax.ShapeDtypeStruct((B, S // CMP_STRIDE, cw), BF16)
    chunk_spec = pl.BlockSpec((None, tm // CMP_STRIDE, cw), lambda b, s: (b, s, 0))
    out_shape = (
        tok(CONV_DIM, BF16), chunked, chunked, tok(KV_DIM, BF16), tok(KV_DIM, BF16),
        jax.ShapeDtypeStruct((B, NSA_DIM, S), BF16), jax.ShapeDtypeStruct((B, NSA_DIM, S), BF16),
        jax.ShapeDtypeStruct((B, S // tk, N_KV_GROUPS * V_ROWS, tk), BF16),
        jax.ShapeDtypeStruct((B, S // tk, N_KV_GROUPS * V_ROWS, tk), BF16),
        jax.ShapeDtypeStruct((B, N_KV_GROUPS * GATE_ROWS, S), F32),
    )
    feat = lambda r: pl.BlockSpec((None, r, tm), lambda b, s: (b, 0, s))
    vt = pl.BlockSpec((None, tm // tk, N_KV_GROUPS * V_ROWS, tk), lambda b, s: (b, s, 0, 0))
    return pl.pallas_call(
        _proj_kernel,
        grid=(B, ns),
        in_specs=[
            pl.BlockSpec((tm, D_MODEL), row),
            pl.BlockSpec((N_STD, D_MODEL), const),
            pl.BlockSpec((N_TR, D_MODEL), const),
            pl.BlockSpec((CONV_WIDTH, CONV_DIM), const),
            pl.BlockSpec((tm, KV_DIM), lambda b, s: (s, 0)),
            pl.BlockSpec((tm, KV_DIM), lambda b, s: (s, 0)),
            pl.BlockSpec((ROPE_HALF, tm), lambda b, s: (0, s)),
            pl.BlockSpec((ROPE_HALF, tm), lambda b, s: (0, s)),
        ],
        out_specs=(
            pl.BlockSpec((tm, CONV_DIM), row), chunk_spec, chunk_spec,
            pl.BlockSpec((tm, KV_DIM), row), pl.BlockSpec((tm, KV_DIM), row),
            feat(NSA_DIM), feat(NSA_DIM), vt, vt, feat(N_KV_GROUPS * GATE_ROWS),
        ),
        out_shape=out_shape,
        scratch_shapes=[pltpu.VMEM((SUBLANES, CONV_DIM), F32), pltpu.VMEM((2, tm, KV_DIM), F32)],
        compiler_params=pltpu.CompilerParams(
            dimension_semantics=("arbitrary", "arbitrary"), vmem_limit_bytes=VMEM_LIMIT),
        name="proj",
    )(x2, wstd, wtr, convw, cosk, sink, cosq, sinq)


def _gelu_tanh(x):
    return x * (0.5 * (1.0 + jnp.tanh(np.sqrt(2.0 / np.pi).astype(np.float32) * (x + 0.044715 * (x * x * x)))))


def _compress_kernel(kch_ref, vch_ref, pek_ref, pev_ref, wk1_ref, wv1_ref, wk2_ref, wv2t_ref, kc_ref, vcT_ref,
                     wk1_big, wv1_big):
    nb, ncp = kch_ref.shape[0], kch_ref.shape[1]
    for w_ref, big in ((wk1_ref, wk1_big), (wv1_ref, wv1_big)):
        big[...] = jnp.zeros(big.shape, BF16)
        for half in range(2):
            for r in range(CMP_STRIDE):
                for g in range(N_KV_GROUPS):
                    r0 = r * KV_DIM + g * HEAD_DIM
                    big[half, r0:r0 + HEAD_DIM, g * CMP_HIDDEN:(g + 1) * CMP_HIDDEN] = (
                        w_ref[half * CMP_STRIDE + r].astype(BF16))

    def hidden(ch_ref, pe_ref, big):
        ch = ch_ref[...].reshape(nb * ncp, ch_ref.shape[2])
        a = _dot(ch, big[0])
        b = _dot(ch, big[1])
        pe = _dot(pe_ref[0].astype(BF16), big[0]) + _dot(pe_ref[1].astype(BF16), big[1])
        b_next = pltpu.roll(b, b.shape[0] - 1, 0)
        return _gelu_tanh(a + b_next + pe[0:1, :]).astype(BF16)

    kc = _dot(hidden(kch_ref, pek_ref, wk1_big), wk2_ref[...]).astype(BF16)
    kc_ref[...] = kc.reshape(nb, ncp, kc.shape[1])
    vT = _dot_nt(wv2t_ref[...], hidden(vch_ref, pev_ref, wv1_big)).astype(BF16)
    for b in range(nb):
        vcT_ref[b] = vT[:, b * ncp:(b + 1) * ncp]


def _compress(kch, vch, pek, pev, wk1, wv1, wk2, wv2t, B, ncp):
    cw = CMP_STRIDE * KV_DIM
    gh = N_KV_GROUPS * CMP_HIDDEN
    full = lambda shape: pl.BlockSpec(shape, lambda i: (0,) * len(shape))
    return pl.pallas_call(
        _compress_kernel,
        grid=(1,),
        in_specs=[
            full((B, ncp, cw)), full((B, ncp, cw)),
            full((2, SUBLANES, cw)), full((2, SUBLANES, cw)),
            full((CMP_BLOCK, HEAD_DIM, CMP_HIDDEN)), full((CMP_BLOCK, HEAD_DIM, CMP_HIDDEN)),
            full((gh, KV_DIM)), full((KV_DIM, gh)),
        ],
        out_specs=(full((B, ncp, KV_DIM)), full((B, KV_DIM, ncp))),
        out_shape=(jax.ShapeDtypeStruct((B, ncp, KV_DIM), BF16), jax.ShapeDtypeStruct((B, KV_DIM, ncp), BF16)),
        scratch_shapes=[pltpu.VMEM((2, cw, gh), BF16), pltpu.VMEM((2, cw, gh), BF16)],
        compiler_params=pltpu.CompilerParams(dimension_semantics=("arbitrary",), vmem_limit_bytes=VMEM_LIMIT),
        name="compress",
    )(kch, vch, pek, pev, wk1, wv1, wk2, wv2t)


def _attn_kernel(n_cast, qT_ref, qrT_ref, kc_ref, vcT_ref, ksel_ref, vselT_ref, kwin_ref, vwinT_ref, gT_ref,
                 mT_ref, wbias_ref, *rest):
    cast_in, o_ref, cast_out = rest[:n_cast], rest[n_cast], rest[n_cast + 1:2 * n_cast + 1]
    scratch = rest[2 * n_cast + 1:]
    nm = qT_ref.shape[0]
    tq = qT_ref.shape[2]
    tk = ksel_ref.shape[2]
    nsub = tq // LANES
    nchunk = N_HEADS * nsub
    gchunk = HEADS_PER_GROUP * nsub
    gl = HEADS_PER_GROUP * tq
    ns = mT_ref.shape[0]
    bpt = tk // SEL_BLOCK
    p = pl.program_id(1)
    qbase = p * tq
    start = p % 2
    n_back = WINDOW // tk
    case = jnp.minimum(p, n_back)
    lo = p - case
    SEL, WIN = 0, 1
    ALL_GROUPS = tuple(range(N_KV_GROUPS))
    mT = mT_ref[...]

    def chunk(c):
        return slice(c * LANES, (c + 1) * LANES)

    def sub(tsub):
        return slice(tsub * LANES, (tsub + 1) * LANES)

    def member(mb):
        qT, qrT, kc_r, vcT_r, ksel, vselT, kwin, vwinT, gT, o_out = (
            r.at[mb] for r in (qT_ref, qrT_ref, kc_ref, vcT_ref, ksel_ref, vselT_ref, kwin_ref, vwinT_ref,
                               gT_ref, o_ref))
        selb_scr, q_scr, qr_scr, s_scr, swin_scr, m_scr, acc_scr, oT_scr = (r.at[mb] for r in scratch)

        def stage_q(ref, scr):
            zeros = jnp.zeros((HEAD_DIM, tq), BF16)
            for h in range(N_HEADS):
                qh = ref[h * HEAD_DIM:(h + 1) * HEAD_DIM, :]
                halves = [qh, zeros] if h // HEADS_PER_GROUP == 0 else [zeros, qh]
                scr[0:KV_DIM, h * tq:(h + 1) * tq] = jnp.concatenate(halves, axis=0)

        gt = gT[...]

        def gate(br, c):
            h, tsub = divmod(c, nsub)
            g, z = divmod(h, HEADS_PER_GROUP)
            r = g * GATE_ROWS + br * HEADS_PER_GROUP + z
            return gt[r:r + 1, sub(tsub)]

        stage_q(qT, q_scr)
        stage_q(qrT, qr_scr)

        qr_scr[KV_DIM:, :] = jnp.zeros((KV_DIM, N_HEADS * tq), BF16)
        key_block = lax.broadcasted_iota(jnp.int32, (tk, KV_DIM), 0) // SEL_BLOCK
        onehot = (key_block == lax.broadcasted_iota(jnp.int32, (tk, KV_DIM), 1)).astype(BF16)

        def qk_sel(kt, slot, groups=ALL_GROUPS):
            for g in groups:
                rb = selb_scr[g, kt, 0:bpt, :].astype(BF16)
                for z in range(HEADS_PER_GROUP):
                    h = g * HEADS_PER_GROUP + z
                    qr_scr[KV_DIM:KV_DIM + bpt, h * tq:(h + 1) * tq] = rb
                lanes = slice(g * gl, (g + 1) * gl)
                s_scr[slot, :, lanes] = _dot(jnp.concatenate([ksel[kt], onehot], axis=1), qr_scr[:, lanes])

        kc = kc_r[...]
        ncp = kc.shape[0]
        s_cmp = _dot(kc, q_scr[...])
        k_win = kwin[pl.ds(lo, n_back + 1)].reshape((n_back + 1) * tk, KV_DIM)
        s_win = _dot(k_win, qr_scr[0:KV_DIM, :])
        for c in range(nchunk):
            swin_scr[:, chunk(c)] = s_win[:, chunk(c)] + wbias_ref[case, :, sub(c % nsub)]
        m_scr[...] = jnp.full(m_scr.shape, NEG, F32)
        acc_scr[...] = jnp.zeros(acc_scr.shape, F32)
        yield

        def softmax_pv(st, load_s, vT_t, tile_bias=None, groups=ALL_GROUPS):
            for g in groups:
                es, alphas = [], []
                for cg in range(gchunk):
                    c = g * gchunk + cg

                    def scores():
                        s = load_s(c)
                        return s if tile_bias is None else s + tile_bias(c % nsub)

                    m_old = m_scr[st, :, chunk(c)]
                    m_new = jnp.maximum(m_old, jnp.max(scores(), axis=0, keepdims=True))
                    m_scr[st, :, chunk(c)] = m_new
                    es.append(jnp.exp2(scores() - m_new).astype(BF16))
                    alphas.append(jnp.exp2(m_old - m_new))
                lanes = slice(g * gl, (g + 1) * gl)
                pv = _dot(vT_t[g * V_ROWS:(g + 1) * V_ROWS, :], jnp.concatenate(es, axis=1))
                acc_scr[st, :, lanes] = jnp.concatenate(alphas, axis=1) * acc_scr[st, :, lanes] + pv

        def finish(st, br, first=False):
            for c in range(nchunk):
                acc = acc_scr[st, :, chunk(c)]
                inv = 1.0 / jnp.maximum(acc[HEAD_DIM:HEAD_DIM + 1, :], 1e-30)
                term = acc[0:HEAD_DIM, :] * (inv * gate(br, c))
                oT_scr[:, chunk(c)] = term if first else oT_scr[:, chunk(c)] + term

        for i in range(n_back, -1, -1):
            rows = slice(i * tk, (i + 1) * tk)
            softmax_pv(WIN, lambda c: swin_scr[rows, chunk(c)], vwinT[lo + i])
        finish(WIN, 2, first=True)
        yield

        imp = []
        for g in range(N_KV_GROUPS):
            vcT = vcT_r[g * HEAD_DIM:(g + 1) * HEAD_DIM, :]
            imp_parts = []
            for tsub in range(nsub):
                n_idx = lax.broadcasted_iota(jnp.int32, (ncp, LANES), 0)
                t_idx = qbase + tsub * LANES + lax.broadcasted_iota(jnp.int32, (ncp, LANES), 1)
                ok = (n_idx * CMP_STRIDE + (CMP_BLOCK - 1)) <= t_idx
                bias = jnp.where(ok, 0.0, NEG).astype(F32)
                keep = ok.astype(F32)
                p_sum = None
                for z in range(HEADS_PER_GROUP):
                    c = (g * HEADS_PER_GROUP + z) * nsub + tsub
                    s = s_cmp[:, chunk(c)] + bias
                    e = jnp.exp2(s - jnp.max(s, axis=0, keepdims=True)) * keep
                    inv = 1.0 / jnp.maximum(jnp.sum(e, axis=0, keepdims=True), 1e-30)
                    oT_scr[:, chunk(c)] += _dot(vcT, e.astype(BF16)) * (inv * gate(0, c))
                    p_sum = e * inv if p_sum is None else p_sum + e * inv
                imp_parts.append(jnp.dot(mT, p_sum, preferred_element_type=F32, precision=lax.Precision.HIGHEST))
            imp.append(jnp.concatenate(imp_parts, axis=1))
        yield

        j_idx = lax.broadcasted_iota(jnp.int32, (ns, tq), 0)
        cur = (qbase + lax.broadcasted_iota(jnp.int32, (ns, tq), 1)) // SEL_BLOCK
        forced = (j_idx == 0) | (j_idx == cur) | (j_idx == cur - 1)
        for g in range(N_KV_GROUPS):
            impv = jnp.where(j_idx <= cur, jnp.where(forced, FORCE, imp[g]), NEG)
            groups = [impv[a:a + SUBLANES, :] for a in range(0, ns, SUBLANES)]
            sub_idx = lax.broadcasted_iota(jnp.int32, (SUBLANES, tq), 0)
            cnts = [jnp.zeros((SUBLANES, tq), F32) for _ in groups]
            for jp in range(ns):
                row = impv[jp:jp + 1, :]
                for gi, grp in enumerate(groups):
                    if gi > jp // SUBLANES:
                        beats = row >= grp
                    elif gi < jp // SUBLANES:
                        beats = row > grp
                    else:
                        beats = (row > grp) | ((row == grp) & (sub_idx > jp % SUBLANES))
                    cnts[gi] = cnts[gi] + beats.astype(F32)
            cnt = jnp.concatenate(cnts, axis=0)
            sel = (cnt < float(min(N_SEL, ns))) & (impv > 0.5 * NEG)
            selb = jnp.where(sel, 0.0, NEG).astype(F32)
            for kt in range(ns // bpt):
                selb_scr[g, kt, 0:bpt, :] = selb[kt * bpt:(kt + 1) * bpt, :]
        qk_sel(start, 0)
        yield

        def sp_sel(kt, slot, tile_bias=None, groups=ALL_GROUPS):
            softmax_pv(SEL, lambda c: s_scr[slot, :, chunk(c)], vselT[kt], tile_bias, groups)

        def finalize():
            finish(SEL, 1)
            for tsub in range(nsub):
                for i in range(N_HEADS // 2):
                    pair = jnp.concatenate([oT_scr[:, chunk((2 * i) * nsub + tsub)],
                                            oT_scr[:, chunk((2 * i + 1) * nsub + tsub)]], axis=0)
                    o_out[sub(tsub), i * 2 * HEAD_DIM:(i + 1) * 2 * HEAD_DIM] = pair.T.astype(BF16)

        yield qk_sel, sp_sel, finalize

    members = [member(mb) for mb in range(nm)]
    for _ in range(4):
        for m in members:
            next(m)
    fns = [next(m) for m in members]

    def each(i, *args):
        for f in fns:
            f[i](*args)

    QK, SP, FINALIZE = 0, 1, 2
    @pl.when(start == 1)
    def _():
        each(QK, 0, 1)
        each(SP, 0, 1)

    def half(kt_next, slot_next, kt_cur, slot_cur):
        units = [(f, (g,)) for f in fns for g in ALL_GROUPS]
        for f, g in units[:2]:
            f[QK](kt_next, slot_next, g)
        for i, (f, g) in enumerate(units):
            f[SP](kt_cur, slot_cur, None, g)
            if i + 2 < len(units):
                f2, g2 = units[i + 2]
                f2[QK](kt_next, slot_next, g2)

    def pair_body(j, carry):
        a = start + 2 * j
        half(a + 1, 1, a, 0)
        half(a + 2, 0, a + 1, 1)
        return carry

    lax.fori_loop(0, p // 2, pair_body, 0)
    each(SP, p, 0, lambda tsub: wbias_ref[n_back, n_back * tk:(n_back + 1) * tk, sub(tsub)])
    each(FINALIZE)

    for src, dst in zip(cast_in, cast_out):
        dst[...] = src[...].astype(BF16)


def _window_bias(t, n_back):
    r = np.arange(t)[:, None]
    c = np.arange(t)[None, :]
    causal = np.where(r <= c, 0.0, NEG)
    edge = np.where(r > c, 0.0, NEG)
    out = np.zeros((n_back + 1, (n_back + 1) * t, t), np.float32)
    for case in range(n_back + 1):
        for i in range(n_back + 1):
            back = case - i
            tile = NEG if back < 0 else causal if back == 0 else edge if back == n_back else 0.0
            out[case, i * t:(i + 1) * t, :] = tile
    return jnp.asarray(out)


def _attn(qT, qrT, kc, vcT, ksel, vselT, kwin, vwinT, gT, mT, B, S, cast=()):
    tq, tk = ATT_TQ, ATT_TK
    nm = ATT_MEMBERS if B % ATT_MEMBERS == 0 else 1
    assert tq == tk and WINDOW % tk == 0 and tk % SEL_BLOCK == 0 and tq % LANES == 0
    assert tk // SEL_BLOCK <= SUBLANES
    n_back = WINDOW // tk
    assert S // tk > n_back
    P = S // tq
    ncp = kc.shape[1]
    ns = S // SEL_BLOCK
    nq = N_HEADS * tq
    qspec = pl.BlockSpec((nm, NSA_DIM, tq), lambda b, p: (b, 0, p))
    kspec = pl.BlockSpec((nm, S // tk, tk, KV_DIM), lambda b, p: (b, 0, 0, 0))
    vspec = pl.BlockSpec((nm, S // tk, N_KV_GROUPS * V_ROWS, tk), lambda b, p: (b, 0, 0, 0))
    steps = (B // nm) * P
    assert all(w.shape[0] % (steps * BF16_ROWS) == 0 for w in cast)
    slab = lambda w: pl.BlockSpec((w.shape[0] // steps, w.shape[1]), lambda b, p: (b * P + p, 0))
    out = pl.pallas_call(
        functools.partial(_attn_kernel, len(cast)),
        grid=(B // nm, P),
        in_specs=[
            qspec, qspec,
            pl.BlockSpec((nm, ncp, KV_DIM), lambda b, p: (b, 0, 0)),
            pl.BlockSpec((nm, KV_DIM, ncp), lambda b, p: (b, 0, 0)),
            kspec, vspec, kspec, vspec,
            pl.BlockSpec((nm, N_KV_GROUPS * GATE_ROWS, tq), lambda b, p: (b, 0, p)),
            pl.BlockSpec((ns, ncp), lambda b, p: (0, 0)),
            pl.BlockSpec((n_back + 1, (n_back + 1) * tk, tq), lambda b, p: (0, 0, 0)),
        ] + [slab(w) for w in cast],
        out_specs=[pl.BlockSpec((nm, tq, NSA_DIM), lambda b, p: (b, p, 0))] + [slab(w) for w in cast],
        out_shape=[jax.ShapeDtypeStruct((B, S, NSA_DIM), BF16)]
        + [jax.ShapeDtypeStruct(w.shape, BF16) for w in cast],
        scratch_shapes=[
            pltpu.VMEM((nm, N_KV_GROUPS, S // tk, SUBLANES, tq), F32),
            pltpu.VMEM((nm, KV_DIM, nq), BF16),
            pltpu.VMEM((nm, 2 * KV_DIM, nq), BF16),
            pltpu.VMEM((nm, 2, tk, nq), F32),
            pltpu.VMEM((nm, (n_back + 1) * tk, nq), F32),
            pltpu.VMEM((nm, 2, 1, nq), F32),
            pltpu.VMEM((nm, 2, V_ROWS, nq), F32),
            pltpu.VMEM((nm, HEAD_DIM, nq), F32),
        ],
        compiler_params=pltpu.CompilerParams(
            dimension_semantics=("arbitrary", "arbitrary"), vmem_limit_bytes=VMEM_LIMIT),
        name="attn",
    )(qT, qrT, kc, vcT, ksel, vselT, kwin, vwinT, gT, mT, _window_bias(tk, n_back), *cast)
    return out[0].reshape(B * S, NSA_DIM), tuple(out[1:])


def _layer_norm(z, g, b):
    mu = jnp.mean(z, axis=-1, keepdims=True)
    zc = z - mu
    var = jnp.mean(zc * zc, axis=-1, keepdims=True)
    return zc * lax.rsqrt(var + LN_EPS) * g + b


def _tail_kernel(alpha, x_ref, mix_ref, o_ref, wg_ref, wco_ref, wno_ref, wo_ref, g1_ref, b1_ref,
                 wup_ref, wdn_ref, g2_ref, b2_ref, out_ref):
    ts = TAIL_SUB
    subs = [slice(i * ts, (i + 1) * ts) for i in range(x_ref.shape[0] // ts)]
    xs = [x_ref[rs, :] for rs in subs]
    xbs = [x.astype(BF16) for x in xs]

    def merge(i):
        rs, x, xb = subs[i], xs[i], xbs[i]
        y_conv = _dot(mix_ref[rs, :], wco_ref[...])
        y_nsa = _dot(o_ref[rs, :], wno_ref[...])
        mixed = (jax.nn.sigmoid(_dot_nt(xb, wg_ref[0:D_MODEL, :])) * y_conv
                 + jax.nn.sigmoid(_dot_nt(xb, wg_ref[D_MODEL:2 * D_MODEL, :])) * y_nsa)
        return alpha * x + _dot(mixed.astype(BF16), wo_ref[...])

    def mlp(x1):
        x1b = x1.astype(BF16)
        ff = jnp.zeros_like(x1)
        for c in range(D_FF // FF_CHUNK):
            h = jnp.maximum(_dot(x1b, wup_ref[:, c * FF_CHUNK:(c + 1) * FF_CHUNK]), 0.0)
            ff = ff + _dot((h * h).astype(BF16), wdn_ref[c * FF_CHUNK:(c + 1) * FF_CHUNK, :])
        return alpha * x1 + ff

    zs = [merge(i) for i in range(len(subs))]
    ys = [mlp(_layer_norm(z, g1_ref[...], b1_ref[...])) for z in zs]
    for rs, y in zip(subs, ys):
        out_ref[rs, :] = _layer_norm(y, g2_ref[...], b2_ref[...])


def _tail(x2, mix, o, wg, wco, wno, wo, g1, b1, wup, wdn, g2, b2, alpha):
    n = x2.shape[0]
    tm = TAIL_TM
    row = lambda i: (i, 0)
    const = lambda i: (0, 0)
    resident = lambda shape: pl.BlockSpec(shape, const, pipeline_mode=pl.Buffered(1))
    return pl.pallas_call(
        functools.partial(_tail_kernel, alpha),
        grid=(n // tm,),
        in_specs=[
            pl.BlockSpec((tm, D_MODEL), row), pl.BlockSpec((tm, CONV_DIM), row), pl.BlockSpec((tm, NSA_DIM), row),
            resident((2 * D_MODEL, D_MODEL)), resident((CONV_DIM, D_MODEL)), resident((NSA_DIM, D_MODEL)),
            resident((D_MODEL, D_MODEL)), resident((1, D_MODEL)), resident((1, D_MODEL)),
            resident((D_MODEL, D_FF)), resident((D_FF, D_MODEL)), resident((1, D_MODEL)), resident((1, D_MODEL)),
        ],
        out_specs=pl.BlockSpec((tm, D_MODEL), row),
        out_shape=jax.ShapeDtypeStruct((n, D_MODEL), F32),
        compiler_params=pltpu.CompilerParams(dimension_semantics=("arbitrary",), vmem_limit_bytes=VMEM_LIMIT),
        name="tail",
    )(x2, mix, o, wg, wco, wno, wo, g1, b1, wup, wdn, g2, b2)


def _rope_tables(S):
    inv = ROPE_THETA ** (-np.arange(0, ROPE_DIM, 2, dtype=np.float64) / ROPE_DIM)
    ang = np.arange(S, dtype=np.float64)[:, None] * inv[None, :]
    cos, sin = np.cos(ang), np.sin(ang)
    pad_c = np.ones((S, HEAD_DIM - ROPE_DIM))
    pad_s = np.zeros((S, HEAD_DIM - ROPE_DIM))
    ck = np.tile(np.concatenate([cos, cos, pad_c], axis=1), (1, N_KV_GROUPS))
    sk = np.tile(np.concatenate([-sin, sin, pad_s], axis=1), (1, N_KV_GROUPS))
    return tuple(jnp.asarray(t.astype(np.float32)) for t in (ck, sk, cos.T, sin.T))


def _overlap_matrix_t(ncp, ns):
    nc = ncp - 1
    i = np.arange(ncp)[None, :]
    j = np.arange(ns)[:, None]
    m = (i * CMP_STRIDE < (j + 1) * SEL_BLOCK) & (i * CMP_STRIDE + CMP_BLOCK > j * SEL_BLOCK) & (i < nc)
    return jnp.asarray(m.astype(np.float32))


def _compress_weights(pe, w1, w2):
    eye = jnp.eye(N_KV_GROUPS, dtype=F32)
    peh = pe.reshape(2, CMP_STRIDE, 1, HEAD_DIM)
    pebig = jnp.broadcast_to(peh, (2, CMP_STRIDE, N_KV_GROUPS, HEAD_DIM)).reshape(2, 1, CMP_STRIDE * KV_DIM)
    pebig = jnp.broadcast_to(pebig, (2, SUBLANES, CMP_STRIDE * KV_DIM))
    w2big = jnp.einsum('ab,hd->ahbd', eye, w2).reshape(N_KV_GROUPS * CMP_HIDDEN, KV_DIM)
    return pebig, w1, w2big.astype(BF16)


def _layer(x2, B, S, w_in, conv_w, w_conv_out, pe_k, wk1, wk2, pe_v, wv1, wv2, w_nsa_out, w_o,
           ln1_g, ln1_b, w_up, w_down, ln2_g, ln2_b, alpha):
    ncp = S // CMP_STRIDE
    ns = S // SEL_BLOCK
    wstd, wg, wtr = _wprep(jnp.swapaxes(w_in, 0, 1))
    ck, sk, cq, sq = _rope_tables(S)

    mix, kch, vch, ksel, kwin, qT, qrT, vselT, vwinT, gT = _proj(
        x2, wstd, wtr, conv_w.reshape(CONV_WIDTH, CONV_DIM), ck, sk, cq, sq, B, S)

    pek, wk1b, wk2b = _compress_weights(pe_k, wk1, wk2)
    pev, wv1b, wv2b = _compress_weights(pe_v, wv1, wv2)
    kc, vcT = _compress(kch, vch, pek, pev, wk1b, wv1b, wk2b, wv2b.T, B, ncp)

    tk = ATT_TK
    steps = (B // (ATT_MEMBERS if B % ATT_MEMBERS == 0 else 1)) * (S // ATT_TQ)
    big = (w_up, w_down, w_o, w_conv_out, w_nsa_out)
    in_attn = all(w.shape[0] % (steps * BF16_ROWS) == 0 for w in big)
    o, cast = _attn(qT, qrT, kc, vcT, ksel.reshape(B, S // tk, tk, KV_DIM), vselT,
                    kwin.reshape(B, S // tk, tk, KV_DIM), vwinT, gT, _overlap_matrix_t(ncp, ns), B, S,
                    cast=big if in_attn else ())
    w_up_b, w_down_b, w_o_b, wco_b, wno_b = cast if in_attn else tuple(w.astype(BF16) for w in big)

    row = lambda v: v.reshape(1, D_MODEL).astype(F32)
    return _tail(x2, mix, o, wg, wco_b, wno_b, w_o_b,
                 row(ln1_g), row(ln1_b), w_up_b, w_down_b, row(ln2_g), row(ln2_b), alpha)


def kernel(x, w_in, conv_w, w_conv_out, pe_k_cmp, w_k_cmp1, w_k_cmp2, pe_v_cmp, w_v_cmp1, w_v_cmp2,
           w_nsa_out, w_o, ln1_g, ln1_b, w_up, w_down, ln2_g, ln2_b):
    B, S, D = x.shape
    depth = w_in.shape[0]
    assert D == D_MODEL and w_in.shape[2] == O_END
    assert S % PROJ_TM == 0 and S % ATT_TQ == 0 and (B * S) % TAIL_TM == 0 and S >= WINDOW
    assert PROJ_TM % PROJ_PIECE == 0 and TAIL_TM % TAIL_SUB == 0 and (S // SEL_BLOCK) % SUBLANES == 0
    alpha = float((2 * depth) ** 0.25)
    x2 = x.reshape(B * S, D)
    for l in range(depth):
        x2 = _layer(x2, B, S, w_in[l], conv_w[l], w_conv_out[l], pe_k_cmp[l], w_k_cmp1[l], w_k_cmp2[l],
                    pe_v_cmp[l], w_v_cmp1[l], w_v_cmp2[l], w_nsa_out[l], w_o[l], ln1_g[l], ln1_b[l],
                    w_up[l], w_down[l], ln2_g[l], ln2_b[l], alpha)
    return x2.reshape(B, S, D)
```

```python
import functools

import numpy as np
import jax
import jax.numpy as jnp
from jax import lax
from jax.experimental import pallas as pl
from jax.experimental.pallas import tpu as pltpu

F32 = jnp.float32
BF16 = jnp.bfloat16

D_MODEL = 1024
CONV_DIM = D_MODEL // 2
CONV_WIDTH = 3
N_HEADS = 8
HEAD_DIM = 64
N_KV_GROUPS = 2
HEADS_PER_GROUP = N_HEADS // N_KV_GROUPS
NSA_DIM = N_HEADS * HEAD_DIM
KV_DIM = N_KV_GROUPS * HEAD_DIM
ROPE_DIM = HEAD_DIM // 4
ROPE_HALF = ROPE_DIM // 2
ROPE_THETA = 500000.0
CMP_BLOCK = 32
CMP_STRIDE = 16
CMP_HIDDEN = 2 * HEAD_DIM
SEL_BLOCK = 64
N_SEL = 16
WINDOW = 512
N_NSA_BRANCHES = 3
D_FF = 4 * D_MODEL
LN_EPS = 1e-5
NEG = -1e30
FORCE = 1e9
SCALE = HEAD_DIM ** -0.5
LOG2E = 1.4426950408889634

SUBLANES = 8
LANES = 128
BF16_ROWS = 16
V_ROWS = HEAD_DIM + BF16_ROWS

_SPLITS = (CONV_DIM, CONV_DIM, CONV_DIM, NSA_DIM, KV_DIM, KV_DIM, KV_DIM, KV_DIM, KV_DIM, KV_DIM,
           N_HEADS * N_NSA_BRANCHES, D_MODEL, D_MODEL)
_OFFS = np.concatenate([[0], np.cumsum(_SPLITS)]).tolist()
(O_H, O_B, O_C, O_Q, O_KCMP, O_VCMP, O_KSEL, O_VSEL, O_KWIN, O_VWIN, O_GATE, O_GCONV, O_GNSA, O_END) = _OFFS

GATE_ROWS = 16
N_STD = 3 * CONV_DIM + 4 * KV_DIM
N_TR = NSA_DIM + 2 * KV_DIM + N_KV_GROUPS * GATE_ROWS

PROJ_TM = 1024
PROJ_PIECE = 256
ATT_TQ = 256
ATT_TK = 256
ATT_MEMBERS = 2
TAIL_TM = 512
TAIL_SUB = 256
FF_CHUNK = 1024
VMEM_LIMIT = 56 * 1024 * 1024


def _dot(a, b):
    return jnp.dot(a, b, preferred_element_type=F32)


def _dot_nt(a, b):
    return lax.dot_general(a, b, (((1,), (1,)), ((), ())), preferred_element_type=F32)


WPREP_COLS = 256


def _wprep_kernel(wT_ref, wstdT_ref, wgT_ref, wtr_ref):
    r = 0
    for a, b in ((O_H, O_Q), (O_KCMP, O_KSEL), (O_KSEL, O_VSEL), (O_KWIN, O_VWIN)):
        wstdT_ref[r:r + b - a, :] = wT_ref[a:b, :].astype(BF16)
        r += b - a
    wgT_ref[...] = wT_ref[O_GCONV:O_END, :].astype(BF16)
    r = 0
    for a, b in ((O_Q, O_KCMP), (O_VSEL, O_KWIN), (O_VWIN, O_GATE)):
        wtr_ref[r:r + b - a, :] = wT_ref[a:b, :].astype(BF16)
        r += b - a
    gates = wT_ref[O_GATE:O_GCONV, :]
    rows = []
    for g in range(N_KV_GROUPS):
        for br in range(N_NSA_BRANCHES):
            for z in range(HEADS_PER_GROUP):
                src = (g * HEADS_PER_GROUP + z) * N_NSA_BRANCHES + br
                rows.append(gates[src:src + 1, :])
        rows.append(jnp.zeros((GATE_ROWS - N_NSA_BRANCHES * HEADS_PER_GROUP, gates.shape[1]), F32))
    wtr_ref[r:r + N_KV_GROUPS * GATE_ROWS, :] = jnp.concatenate(rows, axis=0).astype(BF16)


def _wprep(wT):
    d = wT.shape[1]
    cb = WPREP_COLS
    assert d % cb == 0 and O_GCONV % SUBLANES == 0
    strip = lambda rows: pl.BlockSpec((rows, cb), lambda i: (0, i))
    return pl.pallas_call(
        _wprep_kernel,
        grid=(d // cb,),
        in_specs=[strip(O_END)],
        out_specs=(strip(N_STD), strip(2 * D_MODEL), strip(N_TR)),
        out_shape=(jax.ShapeDtypeStruct((N_STD, d), BF16), jax.ShapeDtypeStruct((2 * D_MODEL, d), BF16),
                   jax.ShapeDtypeStruct((N_TR, d), BF16)),
        compiler_params=pltpu.CompilerParams(dimension_semantics=("arbitrary",), vmem_limit_bytes=VMEM_LIMIT),
        name="wprep",
    )(wT)


def _proj_kernel(x_ref, wstd_ref, wtr_ref, convw_ref, cosk_ref, sink_ref, cosq_ref, sinq_ref,
                 mix_ref, kch_ref, vch_ref, ksel_ref, kwin_ref,
                 qT_ref, qrT_ref, vselT_ref, vwinT_ref, gT_ref, ubuf, kvbuf):
    tm = x_ref.shape[0]

    @pl.when(pl.program_id(1) == 0)
    def _():
        ubuf[...] = jnp.zeros(ubuf.shape, F32)

    xb = x_ref[...].astype(BF16)
    yT = _dot_nt(wtr_ref[...], xb)
    kk = _dot_nt(xb, wstd_ref[3 * CONV_DIM:N_STD, :])
    pieces = [slice(i * PROJ_PIECE, (i + 1) * PROJ_PIECE) for i in range(tm // PROJ_PIECE)]
    hbcs = [_dot_nt(xb[rs, :], wstd_ref[0:3 * CONV_DIM, :]) for rs in pieces]

    cw = convw_ref[...]
    head_rows = lax.broadcasted_iota(jnp.int32, (SUBLANES, CONV_DIM), 0)
    prev = ubuf[...]
    for rs, hbc in zip(pieces, hbcs):
        u = hbc[:, 2 * CONV_DIM:3 * CONV_DIM] * hbc[:, 0:CONV_DIM]

        def shifted(k):
            r = pltpu.roll(u, k, 0)
            first = jnp.where(head_rows < k, pltpu.roll(prev, k, 0), r[0:SUBLANES, :])
            return jnp.concatenate([first, r[SUBLANES:, :]], axis=0)

        y = cw[0:1, :] * shifted(2) + cw[1:2, :] * shifted(1) + cw[2:3, :] * u
        mix_ref[rs, :] = (hbc[:, CONV_DIM:2 * CONV_DIM] * y).astype(BF16)
        prev = u[u.shape[0] - SUBLANES:, :]
    ubuf[...] = prev

    for j, out in enumerate((kch_ref, vch_ref)):
        kvbuf[j] = kk[:, j * KV_DIM:(j + 1) * KV_DIM]
        for r in range(CMP_STRIDE):
            rows = kvbuf[j, pl.ds(r, tm // CMP_STRIDE, stride=CMP_STRIDE), :]
            out[:, r * KV_DIM:(r + 1) * KV_DIM] = rows.astype(BF16)
    lane = lax.broadcasted_iota(jnp.int32, (tm, KV_DIM), 1)
    first_half = (lane % ROPE_DIM) < ROPE_HALF
    ck = cosk_ref[...]
    sk = sink_ref[...]
    for j, out in ((2, ksel_ref), (3, kwin_ref)):
        k = kk[:, j * KV_DIM:(j + 1) * KV_DIM]
        partner = jnp.where(first_half, pltpu.roll(k, KV_DIM - ROPE_HALF, 1), pltpu.roll(k, ROPE_HALF, 1))
        out[...] = (k * ck + partner * sk).astype(BF16)

    q = yT[0:NSA_DIM, :] * (SCALE * LOG2E)
    qT_ref[...] = q.astype(BF16)
    qrT_ref[...] = q.astype(BF16)
    cq = cosq_ref[...]
    sq = sinq_ref[...]
    for h in range(N_HEADS):
        r0 = h * HEAD_DIM
        x1 = q[r0:r0 + ROPE_HALF, :]
        x2 = q[r0 + ROPE_HALF:r0 + ROPE_DIM, :]
        rot = jnp.concatenate([x1 * cq - x2 * sq, x2 * cq + x1 * sq], axis=0)
        qrT_ref[r0:r0 + ROPE_DIM, :] = rot.astype(BF16)
    tk = vselT_ref.shape[-1]
    ones = jnp.ones((BF16_ROWS, tk), BF16)
    for j, out in enumerate((vselT_ref, vwinT_ref)):
        for i in range(tm // tk):
            for g in range(N_KV_GROUPS):
                r0 = NSA_DIM + j * KV_DIM + g * HEAD_DIM
                out[i, g * V_ROWS:g * V_ROWS + HEAD_DIM, :] = yT[r0:r0 + HEAD_DIM, i * tk:(i + 1) * tk].astype(BF16)
                out[i, g * V_ROWS + HEAD_DIM:(g + 1) * V_ROWS, :] = ones
    gT_ref[...] = jax.nn.sigmoid(yT[NSA_DIM + 2 * KV_DIM:N_TR, :])


def _proj(x2, wstd, wtr, convw, cosk, sink, cosq, sinq, B, S):
    tm, tk = PROJ_TM, ATT_TK
    ns = S // tm
    row = lambda b, s: (b * ns + s, 0)
    const = lambda b, s: (0, 0)
    tok = lambda c, dt: jax.ShapeDtypeStruct((B * S, c), dt)
    cw = CMP_STRIDE * KV_DIM
    chunked = jax.ShapeDtypeStruct((B, S // CMP_STRIDE, cw), BF16)
    chunk_spec = pl.BlockSpec((None, tm // CMP_STRIDE, cw), lambda b, s: (b, s, 0))
    out_shape = (
        tok(CONV_DIM, BF16), chunked, chunked, tok(KV_DIM, BF16), tok(KV_DIM, BF16),
        jax.ShapeDtypeStruct((B, NSA_DIM, S), BF16), jax.ShapeDtypeStruct((B, NSA_DIM, S), BF16),
        jax.ShapeDtypeStruct((B, S // tk, N_KV_GROUPS * V_ROWS, tk), BF16),
        jax.ShapeDtypeStruct((B, S // tk, N_KV_GROUPS * V_ROWS, tk), BF16),
        jax.ShapeDtypeStruct((B, N_KV_GROUPS * GATE_ROWS, S), F32),
    )
    feat = lambda r: pl.BlockSpec((None, r, tm), lambda b, s: (b, 0, s))
    vt = pl.BlockSpec((None, tm // tk, N_KV_GROUPS * V_ROWS, tk), lambda b, s: (b, s, 0, 0))
    return pl.pallas_call(
        _proj_kernel,
        grid=(B, ns),
        in_specs=[
            pl.BlockSpec((tm, D_MODEL), row),
            pl.BlockSpec((N_STD, D_MODEL), const),
            pl.BlockSpec((N_TR, D_MODEL), const),
            pl.BlockSpec((CONV_WIDTH, CONV_DIM), const),
            pl.BlockSpec((tm, KV_DIM), lambda b, s: (s, 0)),
            pl.BlockSpec((tm, KV_DIM), lambda b, s: (s, 0)),
            pl.BlockSpec((ROPE_HALF, tm), lambda b, s: (0, s)),
            pl.BlockSpec((ROPE_HALF, tm), lambda b, s: (0, s)),
        ],
        out_specs=(
            pl.BlockSpec((tm, CONV_DIM), row), chunk_spec, chunk_spec,
            pl.BlockSpec((tm, KV_DIM), row), pl.BlockSpec((tm, KV_DIM), row),
            feat(NSA_DIM), feat(NSA_DIM), vt, vt, feat(N_KV_GROUPS * GATE_ROWS),
        ),
        out_shape=out_shape,
        scratch_shapes=[pltpu.VMEM((SUBLANES, CONV_DIM), F32), pltpu.VMEM((2, tm, KV_DIM), F32)],
        compiler_params=pltpu.CompilerParams(
            dimension_semantics=("arbitrary", "arbitrary"), vmem_limit_bytes=VMEM_LIMIT),
        name="proj",
    )(x2, wstd, wtr, convw, cosk, sink, cosq, sinq)


def _gelu_tanh(x):
    return x * (0.5 * (1.0 + jnp.tanh(np.sqrt(2.0 / np.pi).astype(np.float32) * (x + 0.044715 * (x * x * x)))))


def _compress_kernel(kch_ref, vch_ref, pek_ref, pev_ref, wk1_ref, wv1_ref, wk2t_ref, wv2t_ref, kc_ref, vcT_ref,
                     wk1_big, wv1_big, wk2t_big, wv2t_big):
    nb, ncp = kch_ref.shape[0], kch_ref.shape[1]
    for w_ref, big in ((wk1_ref, wk1_big), (wv1_ref, wv1_big)):
        big[...] = jnp.zeros(big.shape, BF16)
        for half in range(2):
            for r in range(CMP_STRIDE):
                for g in range(N_KV_GROUPS):
                    r0 = r * KV_DIM + g * HEAD_DIM
                    big[half, r0:r0 + HEAD_DIM, g * CMP_HIDDEN:(g + 1) * CMP_HIDDEN] = (
                        w_ref[half * CMP_STRIDE + r].astype(BF16))
    for w_ref, big in ((wk2t_ref, wk2t_big), (wv2t_ref, wv2t_big)):
        big[...] = jnp.zeros(big.shape, BF16)
        for g in range(N_KV_GROUPS):
            big[g * HEAD_DIM:(g + 1) * HEAD_DIM, g * CMP_HIDDEN:(g + 1) * CMP_HIDDEN] = w_ref[...].astype(BF16)

    def hidden(ch_ref, pe_ref, w_ref, big):
        ch = ch_ref[...].reshape(nb * ncp, ch_ref.shape[2])
        a = _dot(ch, big[0])
        b = _dot(ch, big[1])
        w_flat = w_ref[...].reshape(CMP_BLOCK * HEAD_DIM, CMP_HIDDEN).astype(BF16)
        pe_rows = jnp.broadcast_to(pe_ref[...], (SUBLANES, CMP_BLOCK * HEAD_DIM)).astype(BF16)
        pe = jnp.concatenate([_dot(pe_rows, w_flat)[0:1, :]] * N_KV_GROUPS, axis=1)
        b_next = pltpu.roll(b, b.shape[0] - 1, 0)
        return _gelu_tanh(a + b_next + pe).astype(BF16)

    kc = _dot_nt(hidden(kch_ref, pek_ref, wk1_ref, wk1_big), wk2t_big[...]).astype(BF16)
    kc_ref[...] = kc.reshape(nb, ncp, kc.shape[1])
    vT = _dot_nt(wv2t_big[...], hidden(vch_ref, pev_ref, wv1_ref, wv1_big)).astype(BF16)
    for b in range(nb):
        vcT_ref[b] = vT[:, b * ncp:(b + 1) * ncp]


def _compress(kch, vch, pek, pev, wk1, wv1, wk2t, wv2t, B, ncp):
    cw = CMP_STRIDE * KV_DIM
    gh = N_KV_GROUPS * CMP_HIDDEN
    full = lambda shape: pl.BlockSpec(shape, lambda i: (0,) * len(shape))
    return pl.pallas_call(
        _compress_kernel,
        grid=(1,),
        in_specs=[
            full((B, ncp, cw)), full((B, ncp, cw)),
            full((1, CMP_BLOCK * HEAD_DIM)), full((1, CMP_BLOCK * HEAD_DIM)),
            full((CMP_BLOCK, HEAD_DIM, CMP_HIDDEN)), full((CMP_BLOCK, HEAD_DIM, CMP_HIDDEN)),
            full((HEAD_DIM, CMP_HIDDEN)), full((HEAD_DIM, CMP_HIDDEN)),
        ],
        out_specs=(full((B, ncp, KV_DIM)), full((B, KV_DIM, ncp))),
        out_shape=(jax.ShapeDtypeStruct((B, ncp, KV_DIM), BF16), jax.ShapeDtypeStruct((B, KV_DIM, ncp), BF16)),
        scratch_shapes=[pltpu.VMEM((2, cw, gh), BF16), pltpu.VMEM((2, cw, gh), BF16),
                        pltpu.VMEM((KV_DIM, gh), BF16), pltpu.VMEM((KV_DIM, gh), BF16)],
        compiler_params=pltpu.CompilerParams(dimension_semantics=("arbitrary",), vmem_limit_bytes=VMEM_LIMIT),
        name="compress",
    )(kch, vch, pek, pev, wk1, wv1, wk2t, wv2t)


def _attn_kernel(n_cast, qT_ref, qrT_ref, kc_ref, vcT_ref, ksel_ref, vselT_ref, kwin_ref, vwinT_ref, gT_ref,
                 mT_ref, wbias_ref, *rest):
    cast_in, o_ref, cast_out = rest[:n_cast], rest[n_cast], rest[n_cast + 1:2 * n_cast + 1]
    scratch = rest[2 * n_cast + 1:]
    nm = qT_ref.shape[0]
    tq = qT_ref.shape[2]
    tk = ksel_ref.shape[2]
    nsub = tq // LANES
    nchunk = N_HEADS * nsub
    gchunk = HEADS_PER_GROUP * nsub
    gl = HEADS_PER_GROUP * tq
    ns = mT_ref.shape[0]
    bpt = tk // SEL_BLOCK
    p = pl.program_id(1)
    qbase = p * tq
    start = p % 2
    n_back = WINDOW // tk
    case = jnp.minimum(p, n_back)
    lo = p - case
    SEL, WIN = 0, 1
    ALL_GROUPS = tuple(range(N_KV_GROUPS))
    mT = mT_ref[...]

    def chunk(c):
        return slice(c * LANES, (c + 1) * LANES)

    def sub(tsub):
        return slice(tsub * LANES, (tsub + 1) * LANES)

    def member(mb):
        qT, qrT, kc_r, vcT_r, ksel, vselT, kwin, vwinT, gT, o_out = (
            r.at[mb] for r in (qT_ref, qrT_ref, kc_ref, vcT_ref, ksel_ref, vselT_ref, kwin_ref, vwinT_ref,
                               gT_ref, o_ref))
        selb_scr, q_scr, qr_scr, s_scr, swin_scr, m_scr, acc_scr, oT_scr = (r.at[mb] for r in scratch)

        def stage_q(ref, scr):
            zeros = jnp.zeros((HEAD_DIM, tq), BF16)
            for h in range(N_HEADS):
                qh = ref[h * HEAD_DIM:(h + 1) * HEAD_DIM, :]
                halves = [qh, zeros] if h // HEADS_PER_GROUP == 0 else [zeros, qh]
                scr[0:KV_DIM, h * tq:(h + 1) * tq] = jnp.concatenate(halves, axis=0)

        gt = gT[...]

        def gate(br, c):
            h, tsub = divmod(c, nsub)
            g, z = divmod(h, HEADS_PER_GROUP)
            r = g * GATE_ROWS + br * HEADS_PER_GROUP + z
            return gt[r:r + 1, sub(tsub)]

        stage_q(qT, q_scr)
        stage_q(qrT, qr_scr)

        qr_scr[KV_DIM:, :] = jnp.zeros((KV_DIM, N_HEADS * tq), BF16)
        key_block = lax.broadcasted_iota(jnp.int32, (tk, KV_DIM), 0) // SEL_BLOCK
        onehot = (key_block == lax.broadcasted_iota(jnp.int32, (tk, KV_DIM), 1)).astype(BF16)

        def qk_sel(kt, slot, groups=ALL_GROUPS):
            for g in groups:
                rb = selb_scr[g, kt, 0:bpt, :].astype(BF16)
                for z in range(HEADS_PER_GROUP):
                    h = g * HEADS_PER_GROUP + z
                    qr_scr[KV_DIM:KV_DIM + bpt, h * tq:(h + 1) * tq] = rb
                lanes = slice(g * gl, (g + 1) * gl)
                s_scr[slot, :, lanes] = _dot(jnp.concatenate([ksel[kt], onehot], axis=1), qr_scr[:, lanes])

        kc = kc_r[...]
        ncp = kc.shape[0]
        s_cmp = _dot(kc, q_scr[...])
        k_win = kwin[pl.ds(lo, n_back + 1)].reshape((n_back + 1) * tk, KV_DIM)
        s_win = _dot(k_win, qr_scr[0:KV_DIM, :])
        for c in range(nchunk):
            swin_scr[:, chunk(c)] = s_win[:, chunk(c)] + wbias_ref[case, :, sub(c % nsub)]
        m_scr[...] = jnp.full(m_scr.shape, NEG, F32)
        acc_scr[...] = jnp.zeros(acc_scr.shape, F32)
        yield

        def softmax_pv(st, load_s, vT_t, tile_bias=None, groups=ALL_GROUPS):
            for g in groups:
                es, alphas = [], []
                for cg in range(gchunk):
                    c = g * gchunk + cg

                    def scores():
                        s = load_s(c)
                        return s if tile_bias is None else s + tile_bias(c % nsub)

                    m_old = m_scr[st, :, chunk(c)]
                    m_new = jnp.maximum(m_old, jnp.max(scores(), axis=0, keepdims=True))
                    m_scr[st, :, chunk(c)] = m_new
                    es.append(jnp.exp2(scores() - m_new).astype(BF16))
                    alphas.append(jnp.exp2(m_old - m_new))
                lanes = slice(g * gl, (g + 1) * gl)
                pv = _dot(vT_t[g * V_ROWS:(g + 1) * V_ROWS, :], jnp.concatenate(es, axis=1))
                acc_scr[st, :, lanes] = jnp.concatenate(alphas, axis=1) * acc_scr[st, :, lanes] + pv

        def finish(st, br, first=False):
            for c in range(nchunk):
                acc = acc_scr[st, :, chunk(c)]
                inv = 1.0 / jnp.maximum(acc[HEAD_DIM:HEAD_DIM + 1, :], 1e-30)
                term = acc[0:HEAD_DIM, :] * (inv * gate(br, c))
                oT_scr[:, chunk(c)] = term if first else oT_scr[:, chunk(c)] + term

        for i in range(n_back, -1, -1):
            rows = slice(i * tk, (i + 1) * tk)
            softmax_pv(WIN, lambda c: swin_scr[rows, chunk(c)], vwinT[lo + i])
        finish(WIN, 2, first=True)
        yield

        imp = []
        for g in range(N_KV_GROUPS):
            vcT = vcT_r[g * HEAD_DIM:(g + 1) * HEAD_DIM, :]
            imp_parts = []
            for tsub in range(nsub):
                n_idx = lax.broadcasted_iota(jnp.int32, (ncp, LANES), 0)
                t_idx = qbase + tsub * LANES + lax.broadcasted_iota(jnp.int32, (ncp, LANES), 1)
                ok = (n_idx * CMP_STRIDE + (CMP_BLOCK - 1)) <= t_idx
                bias = jnp.where(ok, 0.0, NEG).astype(F32)
                keep = ok.astype(F32)
                p_sum = None
                for z in range(HEADS_PER_GROUP):
                    c = (g * HEADS_PER_GROUP + z) * nsub + tsub
                    s = s_cmp[:, chunk(c)] + bias
                    e = jnp.exp2(s - jnp.max(s, axis=0, keepdims=True)) * keep
                    inv = 1.0 / jnp.maximum(jnp.sum(e, axis=0, keepdims=True), 1e-30)
                    oT_scr[:, chunk(c)] += _dot(vcT, e.astype(BF16)) * (inv * gate(0, c))
                    p_sum = e * inv if p_sum is None else p_sum + e * inv
                imp_parts.append(jnp.dot(mT, p_sum, preferred_element_type=F32, precision=lax.Precision.HIGHEST))
            imp.append(jnp.concatenate(imp_parts, axis=1))
        yield

        j_idx = lax.broadcasted_iota(jnp.int32, (ns, tq), 0)
        cur = (qbase + lax.broadcasted_iota(jnp.int32, (ns, tq), 1)) // SEL_BLOCK
        forced = (j_idx == 0) | (j_idx == cur) | (j_idx == cur - 1)
        for g in range(N_KV_GROUPS):
            impv = jnp.where(j_idx <= cur, jnp.where(forced, FORCE, imp[g]), NEG)
            groups = [impv[a:a + SUBLANES, :] for a in range(0, ns, SUBLANES)]
            sub_idx = lax.broadcasted_iota(jnp.int32, (SUBLANES, tq), 0)
            cnts = [jnp.zeros((SUBLANES, tq), F32) for _ in groups]
            for jp in range(ns):
                row = impv[jp:jp + 1, :]
                for gi, grp in enumerate(groups):
                    if gi > jp // SUBLANES:
                        beats = row >= grp
                    elif gi < jp // SUBLANES:
                        beats = row > grp
                    else:
                        beats = (row > grp) | ((row == grp) & (sub_idx > jp % SUBLANES))
                    cnts[gi] = cnts[gi] + beats.astype(F32)
            cnt = jnp.concatenate(cnts, axis=0)
            sel = (cnt < float(min(N_SEL, ns))) & (impv > 0.5 * NEG)
            selb = jnp.where(sel, 0.0, NEG).astype(F32)
            for kt in range(ns // bpt):
                selb_scr[g, kt, 0:bpt, :] = selb[kt * bpt:(kt + 1) * bpt, :]
        qk_sel(start, 0)
        yield

        def sp_sel(kt, slot, tile_bias=None, groups=ALL_GROUPS):
            softmax_pv(SEL, lambda c: s_scr[slot, :, chunk(c)], vselT[kt], tile_bias, groups)

        def finalize():
            finish(SEL, 1)
            for tsub in range(nsub):
                for i in range(N_HEADS // 2):
                    pair = jnp.concatenate([oT_scr[:, chunk((2 * i) * nsub + tsub)],
                                            oT_scr[:, chunk((2 * i + 1) * nsub + tsub)]], axis=0)
                    o_out[sub(tsub), i * 2 * HEAD_DIM:(i + 1) * 2 * HEAD_DIM] = pair.T.astype(BF16)

        yield qk_sel, sp_sel, finalize

    members = [member(mb) for mb in range(nm)]
    for _ in range(4):
        for m in members:
            next(m)
    fns = [next(m) for m in members]

    def each(i, *args):
        for f in fns:
            f[i](*args)

    QK, SP, FINALIZE = 0, 1, 2
    @pl.when(start == 1)
    def _():
        each(QK, 0, 1)
        each(SP, 0, 1)

    def half(kt_next, slot_next, kt_cur, slot_cur):
        units = [(f, (g,)) for f in fns for g in ALL_GROUPS]
        for f, g in units[:2]:
            f[QK](kt_next, slot_next, g)
        for i, (f, g) in enumerate(units):
            f[SP](kt_cur, slot_cur, None, g)
            if i + 2 < len(units):
                f2, g2 = units[i + 2]
                f2[QK](kt_next, slot_next, g2)

    def pair_body(j, carry):
        a = start + 2 * j
        half(a + 1, 1, a, 0)
        half(a + 2, 0, a + 1, 1)
        return carry

    lax.fori_loop(0, p // 2, pair_body, 0)
    each(SP, p, 0, lambda tsub: wbias_ref[n_back, n_back * tk:(n_back + 1) * tk, sub(tsub)])
    each(FINALIZE)

    for src, dst in zip(cast_in, cast_out):
        dst[...] = src[...].astype(BF16)


def _window_bias(t, n_back):
    r = np.arange(t)[:, None]
    c = np.arange(t)[None, :]
    causal = np.where(r <= c, 0.0, NEG)
    edge = np.where(r > c, 0.0, NEG)
    out = np.zeros((n_back + 1, (n_back + 1) * t, t), np.float32)
    for case in range(n_back + 1):
        for i in range(n_back + 1):
            back = case - i
            tile = NEG if back < 0 else causal if back == 0 else edge if back == n_back else 0.0
            out[case, i * t:(i + 1) * t, :] = tile
    return jnp.asarray(out)


def _attn(qT, qrT, kc, vcT, ksel, vselT, kwin, vwinT, gT, mT, B, S, cast=()):
    tq, tk = ATT_TQ, ATT_TK
    nm = ATT_MEMBERS if B % ATT_MEMBERS == 0 else 1
    assert tq == tk and WINDOW % tk == 0 and tk % SEL_BLOCK == 0 and tq % LANES == 0
    assert tk // SEL_BLOCK <= SUBLANES
    n_back = WINDOW // tk
    assert S // tk > n_back
    P = S // tq
    ncp = kc.shape[1]
    ns = S // SEL_BLOCK
    nq = N_HEADS * tq
    qspec = pl.BlockSpec((nm, NSA_DIM, tq), lambda b, p: (b, 0, p))
    kspec = pl.BlockSpec((nm, S // tk, tk, KV_DIM), lambda b, p: (b, 0, 0, 0))
    vspec = pl.BlockSpec((nm, S // tk, N_KV_GROUPS * V_ROWS, tk), lambda b, p: (b, 0, 0, 0))
    steps = (B // nm) * P
    assert all(w.shape[0] % (steps * BF16_ROWS) == 0 for w in cast)
    slab = lambda w: pl.BlockSpec((w.shape[0] // steps, w.shape[1]), lambda b, p: (b * P + p, 0))
    out = pl.pallas_call(
        functools.partial(_attn_kernel, len(cast)),
        grid=(B // nm, P),
        in_specs=[
            qspec, qspec,
            pl.BlockSpec((nm, ncp, KV_DIM), lambda b, p: (b, 0, 0)),
            pl.BlockSpec((nm, KV_DIM, ncp), lambda b, p: (b, 0, 0)),
            kspec, vspec, kspec, vspec,
            pl.BlockSpec((nm, N_KV_GROUPS * GATE_ROWS, tq), lambda b, p: (b, 0, p)),
            pl.BlockSpec((ns, ncp), lambda b, p: (0, 0)),
            pl.BlockSpec((n_back + 1, (n_back + 1) * tk, tq), lambda b, p: (0, 0, 0)),
        ] + [slab(w) for w in cast],
        out_specs=[pl.BlockSpec((nm, tq, NSA_DIM), lambda b, p: (b, p, 0))] + [slab(w) for w in cast],
        out_shape=[jax.ShapeDtypeStruct((B, S, NSA_DIM), BF16)]
        + [jax.ShapeDtypeStruct(w.shape, BF16) for w in cast],
        scratch_shapes=[
            pltpu.VMEM((nm, N_KV_GROUPS, S // tk, SUBLANES, tq), F32),
            pltpu.VMEM((nm, KV_DIM, nq), BF16),
            pltpu.VMEM((nm, 2 * KV_DIM, nq), BF16),
            pltpu.VMEM((nm, 2, tk, nq), F32),
            pltpu.VMEM((nm, (n_back + 1) * tk, nq), F32),
            pltpu.VMEM((nm, 2, 1, nq), F32),
            pltpu.VMEM((nm, 2, V_ROWS, nq), F32),
            pltpu.VMEM((nm, HEAD_DIM, nq), F32),
        ],
        compiler_params=pltpu.CompilerParams(
            dimension_semantics=("arbitrary", "arbitrary"), vmem_limit_bytes=VMEM_LIMIT),
        name="attn",
    )(qT, qrT, kc, vcT, ksel, vselT, kwin, vwinT, gT, mT, _window_bias(tk, n_back), *cast)
    return out[0].reshape(B * S, NSA_DIM), tuple(out[1:])


def _layer_norm(z, g, b):
    mu = jnp.mean(z, axis=-1, keepdims=True)
    zc = z - mu
    var = jnp.mean(zc * zc, axis=-1, keepdims=True)
    return zc * lax.rsqrt(var + LN_EPS) * g + b


def _tail_kernel(alpha, x_ref, mix_ref, o_ref, wg_ref, wco_ref, wno_ref, wo_ref, g1_ref, b1_ref,
                 wup_ref, wdn_ref, g2_ref, b2_ref, out_ref):
    ts = TAIL_SUB
    subs = [slice(i * ts, (i + 1) * ts) for i in range(x_ref.shape[0] // ts)]
    xs = [x_ref[rs, :] for rs in subs]
    xbs = [x.astype(BF16) for x in xs]

    def merge(i):
        rs, x, xb = subs[i], xs[i], xbs[i]
        y_conv = _dot(mix_ref[rs, :], wco_ref[...])
        y_nsa = _dot(o_ref[rs, :], wno_ref[...])
        mixed = (jax.nn.sigmoid(_dot_nt(xb, wg_ref[0:D_MODEL, :])) * y_conv
                 + jax.nn.sigmoid(_dot_nt(xb, wg_ref[D_MODEL:2 * D_MODEL, :])) * y_nsa)
        return alpha * x + _dot(mixed.astype(BF16), wo_ref[...])

    def mlp(x1):
        x1b = x1.astype(BF16)
        ff = jnp.zeros_like(x1)
        for c in range(D_FF // FF_CHUNK):
            h = jnp.maximum(_dot(x1b, wup_ref[:, c * FF_CHUNK:(c + 1) * FF_CHUNK]), 0.0)
            ff = ff + _dot((h * h).astype(BF16), wdn_ref[c * FF_CHUNK:(c + 1) * FF_CHUNK, :])
        return alpha * x1 + ff

    zs = [merge(i) for i in range(len(subs))]
    ys = [mlp(_layer_norm(z, g1_ref[...], b1_ref[...])) for z in zs]
    for rs, y in zip(subs, ys):
        out_ref[rs, :] = _layer_norm(y, g2_ref[...], b2_ref[...])


def _tail(x2, mix, o, wg, wco, wno, wo, g1, b1, wup, wdn, g2, b2, alpha):
    n = x2.shape[0]
    tm = TAIL_TM
    row = lambda i: (i, 0)
    const = lambda i: (0, 0)
    resident = lambda shape: pl.BlockSpec(shape, const, pipeline_mode=pl.Buffered(1))
    return pl.pallas_call(
        functools.partial(_tail_kernel, alpha),
        grid=(n // tm,),
        in_specs=[
            pl.BlockSpec((tm, D_MODEL), row), pl.BlockSpec((tm, CONV_DIM), row), pl.BlockSpec((tm, NSA_DIM), row),
            resident((2 * D_MODEL, D_MODEL)), resident((CONV_DIM, D_MODEL)), resident((NSA_DIM, D_MODEL)),
            resident((D_MODEL, D_MODEL)), resident((1, D_MODEL)), resident((1, D_MODEL)),
            resident((D_MODEL, D_FF)), resident((D_FF, D_MODEL)), resident((1, D_MODEL)), resident((1, D_MODEL)),
        ],
        out_specs=pl.BlockSpec((tm, D_MODEL), row),
        out_shape=jax.ShapeDtypeStruct((n, D_MODEL), F32),
        compiler_params=pltpu.CompilerParams(dimension_semantics=("arbitrary",), vmem_limit_bytes=VMEM_LIMIT),
        name="tail",
    )(x2, mix, o, wg, wco, wno, wo, g1, b1, wup, wdn, g2, b2)


def _rope_tables(S):
    inv = ROPE_THETA ** (-np.arange(0, ROPE_DIM, 2, dtype=np.float64) / ROPE_DIM)
    ang = np.arange(S, dtype=np.float64)[:, None] * inv[None, :]
    cos, sin = np.cos(ang), np.sin(ang)
    pad_c = np.ones((S, HEAD_DIM - ROPE_DIM))
    pad_s = np.zeros((S, HEAD_DIM - ROPE_DIM))
    ck = np.tile(np.concatenate([cos, cos, pad_c], axis=1), (1, N_KV_GROUPS))
    sk = np.tile(np.concatenate([-sin, sin, pad_s], axis=1), (1, N_KV_GROUPS))
    return tuple(jnp.asarray(t.astype(np.float32)) for t in (ck, sk, cos.T, sin.T))


def _overlap_matrix_t(ncp, ns):
    nc = ncp - 1
    i = np.arange(ncp)[None, :]
    j = np.arange(ns)[:, None]
    m = (i * CMP_STRIDE < (j + 1) * SEL_BLOCK) & (i * CMP_STRIDE + CMP_BLOCK > j * SEL_BLOCK) & (i < nc)
    return jnp.asarray(m.astype(np.float32))


def _layer(x2, B, S, w_in, conv_w, w_conv_out, pe_k, wk1, wk2, pe_v, wv1, wv2, w_nsa_out, w_o,
           ln1_g, ln1_b, w_up, w_down, ln2_g, ln2_b, alpha):
    ncp = S // CMP_STRIDE
    ns = S // SEL_BLOCK
    wstd, wg, wtr = _wprep(jnp.swapaxes(w_in, 0, 1))
    ck, sk, cq, sq = _rope_tables(S)

    mix, kch, vch, ksel, kwin, qT, qrT, vselT, vwinT, gT = _proj(
        x2, wstd, wtr, conv_w.reshape(CONV_WIDTH, CONV_DIM), ck, sk, cq, sq, B, S)

    pe_row = lambda pe: pe.reshape(1, CMP_BLOCK * HEAD_DIM)
    kc, vcT = _compress(kch, vch, pe_row(pe_k), pe_row(pe_v), wk1, wv1,
                        jnp.swapaxes(wk2, 0, 1), jnp.swapaxes(wv2, 0, 1), B, ncp)

    tk = ATT_TK
    steps = (B // (ATT_MEMBERS if B % ATT_MEMBERS == 0 else 1)) * (S // ATT_TQ)
    big = (w_up, w_down, w_o, w_conv_out, w_nsa_out)
    in_attn = all(w.shape[0] % (steps * BF16_ROWS) == 0 for w in big)
    o, cast = _attn(qT, qrT, kc, vcT, ksel.reshape(B, S // tk, tk, KV_DIM), vselT,
                    kwin.reshape(B, S // tk, tk, KV_DIM), vwinT, gT, _overlap_matrix_t(ncp, ns), B, S,
                    cast=big if in_attn else ())
    w_up_b, w_down_b, w_o_b, wco_b, wno_b = cast if in_attn else tuple(w.astype(BF16) for w in big)

    row = lambda v: v.reshape(1, D_MODEL).astype(F32)
    return _tail(x2, mix, o, wg, wco_b, wno_b, w_o_b,
                 row(ln1_g), row(ln1_b), w_up_b, w_down_b, row(ln2_g), row(ln2_b), alpha)


def kernel(x, w_in, conv_w, w_conv_out, pe_k_cmp, w_k_cmp1, w_k_cmp2, pe_v_cmp, w_v_cmp1, w_v_cmp2,
           w_nsa_out, w_o, ln1_g, ln1_b, w_up, w_down, ln2_g, ln2_b):
    B, S, D = x.shape
    depth = w_in.shape[0]
    assert D == D_MODEL and w_in.shape[2] == O_END
    assert S % PROJ_TM == 0 and S % ATT_TQ == 0 and (B * S) % TAIL_TM == 0 and S >= WINDOW
    assert PROJ_TM % PROJ_PIECE == 0 and TAIL_TM % TAIL_SUB == 0 and (S // SEL_BLOCK) % SUBLANES == 0
    alpha = float((2 * depth) ** 0.25)
    x2 = x.reshape(B * S, D)
    for l in range(depth):
        x2 = _layer(x2, B, S, w_in[l], conv_w[l], w_conv_out[l], pe_k_cmp[l], w_k_cmp1[l], w_k_cmp2[l],
                    pe_v_cmp[l], w_v_cmp1[l], w_v_cmp2[l], w_nsa_out[l], w_o[l], ln1_g[l], ln1_b[l],
                    w_up[l], w_down[l], ln2_g[l], ln2_b[l], alpha)
    return x2.reshape(B, S, D)
```
